```python
import math
import jax, jax.numpy as jnp
from jax import lax
import numpy as np

D_MODEL = 2048
BATCH = 2
SEQ = 16384
DEPTH = 2
DEC_BATCH = 32
DEC_SEQ = 64
PAST_LEN = 2048

CHUNK = 64
D_MIX = D_MODEL
SSM_WIDTH = D_MIX // 4
ATTN_WIDTH = D_MIX // 2
CONV_WIDTH = D_MIX - SSM_WIDTH - ATTN_WIDTH
SSM_GROUP = 16
SSM_GROUPS = SSM_WIDTH // SSM_GROUP
SSM_STATE = 64
HEAD_DIM = 64
N_HEADS = ATTN_WIDTH // HEAD_DIM
N_KV_HEADS = 2
Q_PER_KV = N_HEADS // N_KV_HEADS
KV_WIDTH = N_KV_HEADS * HEAD_DIM
WINDOW = 128
WINDOW_CHUNKS = WINDOW // CHUNK
BAND = (WINDOW_CHUNKS + 1) * CHUNK
CONV_K = 3
D_FF = 4 * D_MODEL
REL_BUCKETS = 32
REL_MAX_DIST = 64
EPS = 1e-6
NEG_INF = -1e30
IN_COLS = SSM_WIDTH + ATTN_WIDTH + 2 * KV_WIDTH + 3 * CONV_WIDTH

kernel_name = 'hybrid_streaming_encoder_step'


def rmsnorm(x, gain=None):
    xf = x.astype(jnp.float32)
    y = xf * lax.rsqrt(jnp.mean(xf * xf, axis=-1, keepdims=True) + EPS)
    if gain is not None:
        y = y * gain.astype(jnp.float32)
    return y.astype(x.dtype)


def rel_bias_block(table, n_q, n_k, key_offset):
    qi = jnp.arange(n_q)[:, None]
    ks = jnp.arange(n_k)[None, :]
    rel = ks - key_offset - qi
    half = REL_BUCKETS // 2
    exact = half // 2
    n = jnp.abs(rel)
    nf = jnp.maximum(n, 1).astype(jnp.float32)
    far = exact + (jnp.log(nf / exact) / math.log(REL_MAX_DIST / exact) * (half - exact)).astype(jnp.int32)
    far = jnp.minimum(far, half - 1)
    bucket = jnp.where(rel > 0, half, 0) + jnp.where(n < exact, n, far)
    bias = table[bucket].astype(jnp.float32)
    return jnp.transpose(bias, (2, 0, 1)).reshape(N_KV_HEADS, Q_PER_KV, n_q, n_k)


def sink_attention(q, k, v, bias, sinks, key_mask):
    logits = jnp.einsum('bnqkgd,bnskd->bnkgqs', q, k, preferred_element_type=jnp.float32)
    logits = logits * (HEAD_DIM ** -0.5) + bias
    if key_mask is not None:
        logits = jnp.where(key_mask[None, :, None, None, None, :], logits, NEG_INF)
    sink = sinks.astype(jnp.float32).reshape(1, 1, N_KV_HEADS, Q_PER_KV, 1, 1)
    m = jnp.maximum(jnp.max(logits, axis=-1, keepdims=True), sink)
    p = jnp.exp(logits - m)
    probs = p / (jnp.sum(p, axis=-1, keepdims=True) + jnp.exp(sink - m))
    return jnp.einsum('bnkgqs,bnskd->bnqkgd', probs.astype(v.dtype), v)


def band_attention_prompt(q, k, v, bias, sinks):
    b, L = q.shape[0], q.shape[1]
    nc = L // CHUNK
    qc = q.reshape(b, nc, CHUNK, N_KV_HEADS, Q_PER_KV, HEAD_DIM)

    def band(t):
        tp = jnp.pad(t, ((0, 0), (WINDOW, 0), (0, 0), (0, 0)))
        tp = tp.reshape(b, nc + WINDOW_CHUNKS, CHUNK, N_KV_HEADS, HEAD_DIM)
        return jnp.concatenate([tp[:, j:j + nc] for j in range(WINDOW_CHUNKS + 1)], axis=2)

    key_pos = jnp.arange(nc)[:, None] * CHUNK - WINDOW + jnp.arange(BAND)[None, :]
    out = sink_attention(qc, band(k), band(v), bias, sinks, key_pos >= 0)
    return out.reshape(b, L, ATTN_WIDTH)


def _complex_affine_combine(e1, e2):
    a1r, a1i, b1r, b1i = e1
    a2r, a2i, b2r, b2i = e2
    ar = a1r * a2r - a1i * a2i
    ai = a1r * a2i + a1i * a2r
    br = a2r * b1r - a2i * b1i + b2r
    bi = a2r * b1i + a2i * b1r + b2i
    return ar, ai, br, bi


def ssm_mixer(u, a_re, a_im, log_dt, b_re, b_im, c_re, c_im, d_skip, w_glu, h0_re, h0_im):
    f32 = jnp.float32
    b, L = u.shape[0], u.shape[1]
    uf = u.astype(f32).reshape(b, L, SSM_GROUPS, SSM_GROUP)
    ar, ai = a_re.astype(f32), a_im.astype(f32)
    dt = jnp.exp(log_dt.astype(f32))[:, None]
    mag = jnp.exp(dt * ar)
    abar_re, abar_im = mag * jnp.cos(dt * ai), mag * jnp.sin(dt * ai)
    den = ar * ar + ai * ai
    f_re = ((abar_re - 1.0) * ar + abar_im * ai) / den
    f_im = (abar_im * ar - (abar_re - 1.0) * ai) / den
    bu_re = jnp.einsum('blgh,gph->blgp', uf, b_re.astype(f32))
    bu_im = jnp.einsum('blgh,gph->blgp', uf, b_im.astype(f32))
    bb_re = f_re * bu_re - f_im * bu_im
    bb_im = f_re * bu_im + f_im * bu_re
    shape = (1, L, SSM_GROUPS, SSM_STATE)
    ac_re, ac_im, h_re, h_im = lax.associative_scan(
        _complex_affine_combine,
        (jnp.broadcast_to(abar_re, shape), jnp.broadcast_to(abar_im, shape), bb_re, bb_im),
        axis=1)
    if h0_re is not None:
        g_re = h0_re.astype(f32)[:, None]
        g_im = h0_im.astype(f32)[:, None]
        h_re, h_im = (h_re + ac_re * g_re - ac_im * g_im,
                      h_im + ac_re * g_im + ac_im * g_re)
    y = (jnp.einsum('blgp,ghp->blgh', h_re, c_re.astype(f32))
         - jnp.einsum('blgp,ghp->blgh', h_im, c_im.astype(f32))
         + d_skip.astype(f32).reshape(SSM_GROUPS, SSM_GROUP) * uf)
    y = jax.nn.gelu(y.reshape(b, L, SSM_WIDTH))
    y = y * jax.nn.sigmoid(y @ w_glu.astype(f32))
    return y.astype(u.dtype), h_re[:, -1], h_im[:, -1]


def causal_conv(zp, w, L):
    return sum(zp[:, j:j + L] * w[:, j] for j in range(CONV_K))


def trunk_layer(x, mod, bias, w_in, a_re, a_im, log_dt, b_re, b_im, c_re, c_im, d_skip, w_glu,
                q_g, k_g, sinks, conv_w, out_g, w_out, w_ff1, w_ff2,
                kv_k=None, kv_v=None, h0_re=None, h0_im=None, conv_buf=None):
    b, L = x.shape[0], x.shape[1]
    shift1, scale1, gate1, shift2, scale2, gate2 = jnp.split(mod[:, None, :], 6, axis=-1)
    h = rmsnorm(x) * (1.0 + scale1) + shift1
    proj = h @ w_in
    sizes = [SSM_WIDTH, ATTN_WIDTH, KV_WIDTH, KV_WIDTH, CONV_WIDTH, CONV_WIDTH, CONV_WIDTH]
    u, q, k, v, gb, gc, xc = jnp.split(proj, np.cumsum(sizes)[:-1].tolist(), axis=-1)

    y_ssm, new_re, new_im = ssm_mixer(u, a_re, a_im, log_dt, b_re, b_im, c_re, c_im, d_skip, w_glu, h0_re, h0_im)

    q = rmsnorm(q.reshape(b, L, N_HEADS, HEAD_DIM), q_g)
    k = rmsnorm(k.reshape(b, L, N_KV_HEADS, HEAD_DIM), k_g)
    v = v.reshape(b, L, N_KV_HEADS, HEAD_DIM)
    if kv_k is None:
        y_attn = band_attention_prompt(q, k, v, bias, sinks)
        new_k, new_v = k[:, -WINDOW:], v[:, -WINDOW:]
    else:
        n_buf = kv_k.shape[1]
        kf = jnp.concatenate([kv_k.astype(k.dtype), k], axis=1)
        vf = jnp.concatenate([kv_v.astype(v.dtype), v], axis=1)
        qg = q.reshape(b, 1, L, N_KV_HEADS, Q_PER_KV, HEAD_DIM)
        y_attn = sink_attention(qg, kf[:, None], vf[:, None], bias, sinks, None).reshape(b, L, ATTN_WIDTH)
        new_k, new_v = kf[:, -n_buf:], vf[:, -n_buf:]

    z = gc * xc
    if conv_buf is None:
        conv_buf = jnp.zeros((b, CONV_K - 1, CONV_WIDTH), z.dtype)
    zp = jnp.concatenate([conv_buf.astype(z.dtype), z], axis=1)
    y_conv = gb * causal_conv(zp, conv_w, L)
    new_conv = zp[:, -(CONV_K - 1):]

    y = jnp.concatenate([rmsnorm(y_ssm), rmsnorm(y_attn), rmsnorm(y_conv)], axis=-1) * out_g
    x = x + gate1 * (y @ w_out)

    h2 = rmsnorm(x) * (1.0 + scale2) + shift2
    x = x + gate2 * (jnp.square(jax.nn.relu(h2 @ w_ff1)) @ w_ff2)
    return x, new_k, new_v, new_re, new_im, new_conv


def setup_inputs(seed: int = 0) -> dict:
    key = jax.random.key(seed)
    ks = jax.random.split(key, 32)
    f32 = jnp.float32

    def nrm(k, shape, scale):
        return scale * jax.random.normal(k, shape, f32)

    n_buf = min(WINDOW, PAST_LEN)
    p_idx = jnp.arange(SSM_STATE, dtype=f32)
    return {
        'x_prompt': nrm(ks[0], (BATCH, SEQ, D_MODEL), 1.0),
        'x_sample': nrm(ks[1], (DEC_BATCH, DEC_SEQ, D_MODEL), 1.0),
        'cache_k': nrm(ks[2], (DEPTH, DEC_BATCH, n_buf, N_KV_HEADS, HEAD_DIM), 1.0),
        'cache_v': nrm(ks[3], (DEPTH, DEC_BATCH, n_buf, N_KV_HEADS, HEAD_DIM), 1.0),
        'state_ssm_re': nrm(ks[4], (DEPTH, DEC_BATCH, SSM_GROUPS, SSM_STATE), 0.3),
        'state_ssm_im': nrm(ks[5], (DEPTH, DEC_BATCH, SSM_GROUPS, SSM_STATE), 0.3),
        'state_conv': nrm(ks[6], (DEPTH, DEC_BATCH, CONV_K - 1, CONV_WIDTH), 1.0),
        'c_prompt': nrm(ks[7], (BATCH, D_MODEL), 1.0),
        'c_sample': nrm(ks[8], (DEC_BATCH, D_MODEL), 1.0),
        'rel_bias': nrm(ks[9], (REL_BUCKETS, N_HEADS), 0.5),
        'w_ada': nrm(ks[10], (DEPTH, D_MODEL, 6 * D_MODEL), 0.3 * D_MODEL ** -0.5),
        'b_ada': nrm(ks[11], (DEPTH, 6 * D_MODEL), 0.02),
        'w_in': nrm(ks[12], (DEPTH, D_MODEL, IN_COLS), D_MODEL ** -0.5),
        'ssm_a_re': -0.5 + nrm(ks[13], (DEPTH, SSM_GROUPS, SSM_STATE), 0.01),
        'ssm_a_im': math.pi * p_idx + nrm(ks[14], (DEPTH, SSM_GROUPS, SSM_STATE), 0.01),
        'ssm_log_dt': jax.random.uniform(ks[15], (DEPTH, SSM_GROUPS), f32, math.log(0.001), math.log(0.1)),
        'ssm_b_re': nrm(ks[16], (DEPTH, SSM_GROUPS, SSM_STATE, SSM_GROUP), (2 * SSM_GROUP) ** -0.5),
        'ssm_b_im': nrm(ks[17], (DEPTH, SSM_GROUPS, SSM_STATE, SSM_GROUP), (2 * SSM_GROUP) ** -0.5),
        'ssm_c_re': nrm(ks[18], (DEPTH, SSM_GROUPS, SSM_GROUP, SSM_STATE), (2 * SSM_STATE) ** -0.5),
        'ssm_c_im': nrm(ks[19], (DEPTH, SSM_GROUPS, SSM_GROUP, SSM_STATE), (2 * SSM_STATE) ** -0.5),
        'ssm_d': nrm(ks[20], (DEPTH, SSM_WIDTH), 1.0),
        'ssm_w_glu': nrm(ks[21], (DEPTH, SSM_WIDTH, SSM_WIDTH), SSM_WIDTH ** -0.5),
        'q_norm_g': 1.0 + nrm(ks[22], (DEPTH, HEAD_DIM), 0.05),
        'k_norm_g': 1.0 + nrm(ks[23], (DEPTH, HEAD_DIM), 0.05),
        'attn_sinks': nrm(ks[24], (DEPTH, N_HEADS), 1.0),
        'conv_w': nrm(ks[25], (DEPTH, CONV_WIDTH, CONV_K), CONV_K ** -0.5),
        'out_norm_g': 1.0 + nrm(ks[26], (DEPTH, D_MIX), 0.05),
        'w_out': nrm(ks[27], (DEPTH, D_MIX, D_MODEL), D_MIX ** -0.5),
        'w_ff1': nrm(ks[28], (DEPTH, D_MODEL, D_FF), D_MODEL ** -0.5),
        'w_ff2': nrm(ks[29], (DEPTH, D_FF, D_MODEL), D_FF ** -0.5),
    }


def reference(x_prompt, x_sample, cache_k, cache_v, state_ssm_re, state_ssm_im, state_conv,
              c_prompt, c_sample, rel_bias, w_ada, b_ada, w_in, ssm_a_re, ssm_a_im, ssm_log_dt,
              ssm_b_re, ssm_b_im, ssm_c_re, ssm_c_im, ssm_d, ssm_w_glu, q_norm_g, k_norm_g,
              attn_sinks, conv_w, out_norm_g, w_out, w_ff1, w_ff2):
    n_buf = cache_k.shape[2]
    s_new = x_sample.shape[1]
    bias_prompt = rel_bias_block(rel_bias, CHUNK, BAND, WINDOW)
    bias_sample = rel_bias_block(rel_bias, s_new, n_buf + s_new, n_buf)
    xp, xs = x_prompt, x_sample
    kp_l, vp_l, rp_l, ip_l, cp_l = [], [], [], [], []
    ks_l, vs_l, rs_l, is_l, cs_l = [], [], [], [], []
    for l in range(DEPTH):
        lw = (w_in[l], ssm_a_re[l], ssm_a_im[l], ssm_log_dt[l], ssm_b_re[l], ssm_b_im[l],
              ssm_c_re[l], ssm_c_im[l], ssm_d[l], ssm_w_glu[l], q_norm_g[l], k_norm_g[l],
              attn_sinks[l], conv_w[l], out_norm_g[l], w_out[l], w_ff1[l], w_ff2[l])
        mod_p = jax.nn.silu(c_prompt) @ w_ada[l] + b_ada[l]
        mod_s = jax.nn.silu(c_sample) @ w_ada[l] + b_ada[l]
        xp, kp, vp, rp, ip, cp = trunk_layer(xp, mod_p, bias_prompt, *lw)
        xs, ks, vs, rs, is_, cs = trunk_layer(xs, mod_s, bias_sample, *lw,
                                              cache_k[l], cache_v[l], state_ssm_re[l],
                                              state_ssm_im[l], state_conv[l])
        kp_l.append(kp); vp_l.append(vp); rp_l.append(rp); ip_l.append(ip); cp_l.append(cp)
        ks_l.append(ks); vs_l.append(vs); rs_l.append(rs); is_l.append(is_); cs_l.append(cs)
    return (xp, xs,
            jnp.stack(kp_l), jnp.stack(vp_l), jnp.stack(rp_l), jnp.stack(ip_l), jnp.stack(cp_l),
            jnp.stack(ks_l), jnp.stack(vs_l), jnp.stack(rs_l), jnp.stack(is_l), jnp.stack(cs_l))
```

```python
import functools
import math

import numpy as np
import jax
import jax.numpy as jnp
from jax import lax
from jax.experimental import pallas as pl
from jax.experimental.pallas import tpu as pltpu

F32 = jnp.float32
BF16 = jnp.bfloat16

CHUNK = 64
SSM_GROUP = 16
SSM_STATE = 64
HEAD_DIM = 64
N_KV_HEADS = 2
WINDOW = 128
CONV_K = 3
REL_BUCKETS = 32
REL_MAX_DIST = 64
EPS = 1e-6
NEG_INF = -1e30

LANES = 128
SUBLANES = 8
PAIR = 2 * CHUNK
VMEM_LIMIT_BYTES = 56 * 1024 * 1024


def _params(*sem):
    return pltpu.CompilerParams(dimension_semantics=sem, vmem_limit_bytes=VMEM_LIMIT_BYTES)


def _const_spec(shape):
    nd = len(shape)
    return pl.BlockSpec(shape, lambda *_: (0,) * nd, pipeline_mode=pl.Buffered(1))


def _tok_blocks(nb, r, tm):
    if r >= tm:
        assert r % tm == 0
        return 1, tm
    bb = min(tm // r, nb)
    assert nb % bb == 0
    return bb, r


def _mod_kernel(c_ref, w_ref, b_ref, o_ref):
    c = c_ref[...]
    s = (c * jax.nn.sigmoid(c)).astype(BF16)
    o_ref[...] = jnp.dot(s, w_ref[...].astype(BF16), preferred_element_type=F32) + b_ref[...]


def _mod_call(c_all, w_ada, b_ada):
    depth, d, n = w_ada.shape
    nc = c_all.shape[0]
    tn = 1024
    return pl.pallas_call(
        _mod_kernel,
        grid=(depth, n // tn),
        in_specs=[pl.BlockSpec((nc, d), lambda l, j: (0, 0)),
                  pl.BlockSpec((None, d, tn), lambda l, j: (l, 0, j)),
                  pl.BlockSpec((None, 1, tn), lambda l, j: (l, 0, j))],
        out_specs=pl.BlockSpec((None, nc, tn), lambda l, j: (l, 0, j)),
        out_shape=jax.ShapeDtypeStruct((depth, nc, n), F32),
        compiler_params=_params("arbitrary", "arbitrary"),
    )(c_all, w_ada, b_ada.reshape(depth, 1, n))


def _bucket_maps(sample):
    lane = np.arange(PAIR)[None, :]
    if sample:
        nk = 2 * WINDOW + PAIR
        row = np.arange(nk)[:, None]
        own = row >= 2 * WINDOW
        key_b = np.where(own, (row - 2 * WINDOW) // CHUNK, row // WINDOW)
        key_s = np.where(own, WINDOW + (row - 2 * WINDOW) % CHUNK, row % WINDOW)
        rel = key_s - WINDOW - lane % CHUNK
        visible = key_b == lane // CHUNK
    else:
        nk = WINDOW + PAIR
        row = np.arange(nk)[:, None]
        rel = row - WINDOW - lane
        dc = row // CHUNK - lane // CHUNK
        visible = (dc >= 0) & (dc <= WINDOW // CHUNK)
    half = REL_BUCKETS // 2
    exact = half // 2
    n = np.abs(rel)
    nf = np.maximum(n, 1).astype(np.float32)
    far = exact + (np.log(nf / np.float32(exact)) / np.float32(math.log(REL_MAX_DIST / exact))
                   * np.float32(half - exact)).astype(np.int32)
    far = np.minimum(far, half - 1)
    bucket = np.where(rel > 0, half, 0) + np.where(n < exact, n, far)
    full = np.where(visible, bucket, -1).astype(np.int32)
    first = np.where(row >= WINDOW, full, -1).astype(np.int32)
    return np.stack([full, first])


def _bias_kernel(table_ref, bucket_ref, o_ref):
    h = pl.program_id(1)
    bucket = bucket_ref[...]
    acc = jnp.full(bucket.shape, NEG_INF, F32)
    for b in range(REL_BUCKETS):
        acc = jnp.where(bucket == b, table_ref[b, h], acc)
    o_ref[...] = acc


def _bias_call(rel_bias, sample):
    buckets = jnp.asarray(_bucket_maps(sample))
    _, nk, _ = buckets.shape
    n_heads = rel_bias.shape[1]
    qpg = n_heads // N_KV_HEADS
    return pl.pallas_call(
        _bias_kernel,
        grid=(2, n_heads),
        in_specs=[pl.BlockSpec(memory_space=pltpu.SMEM),
                  pl.BlockSpec((None, nk, PAIR), lambda s, h: (s, 0, 0))],
        out_specs=pl.BlockSpec((None, None, nk, PAIR), lambda s, h: (s, h // qpg, 0, h % qpg)),
        out_shape=jax.ShapeDtypeStruct((2, N_KV_HEADS, nk, qpg * PAIR), F32),
        compiler_params=_params("arbitrary", "arbitrary"),
    )(rel_bias, buckets)


def _in_proj_kernel(x_ref, shift_ref, scale_ref, wrm_ref, wt_ref, kg_ref,
                    u_ref, gb_ref, gc_ref, xc_ref, k_ref, v_ref, qt_ref, vt_ref, *, widths):
    bb, tr, d = x_ref.shape
    x = x_ref[...]
    ms = jnp.mean(x * x, axis=-1, keepdims=True)
    h = x * lax.rsqrt(ms + EPS) * (1.0 + scale_ref[...]) + shift_ref[...]
    h = h.reshape(bb * tr, d).astype(BF16)

    p = jnp.dot(h, wrm_ref[...], preferred_element_type=F32)
    w_ssm, w_conv, w_kv, w_attn = widths
    o = 0
    u_ref[...] = p[:, o:o + w_ssm]
    o += w_ssm
    gb_ref[...] = p[:, o:o + w_conv]
    o += w_conv
    gc_ref[...] = p[:, o:o + w_conv]
    o += w_conv
    xc_ref[...] = p[:, o:o + w_conv]
    o += w_conv
    k = p[:, o:o + w_kv]
    o += w_kv
    v_ref[...] = p[:, o:o + w_kv]

    lo = lax.broadcasted_iota(jnp.int32, k.shape, 1) < HEAD_DIM
    k2 = k * k
    s_lo = jnp.sum(jnp.where(lo, k2, 0.0), axis=-1, keepdims=True)
    s_hi = jnp.sum(jnp.where(lo, 0.0, k2), axis=-1, keepdims=True)
    ssq = jnp.where(lo, s_lo, s_hi)
    k_ref[...] = k * lax.rsqrt(ssq * (1.0 / HEAD_DIM) + EPS) * kg_ref[...]

    pt = lax.dot_general(wt_ref[...], h, (((1,), (1,)), ((), ())), preferred_element_type=F32)
    qt_ref[...] = pt[:w_attn]
    vt_ref[...] = pt[w_attn:]


def _in_proj_call(x3, mod3, wrm, wt, kg2, widths, tm):
    nb, r, d = x3.shape
    w_ssm, w_conv, w_kv, w_attn = widths
    bb, tr = _tok_blocks(nb, r, tm)
    tmm = bb * tr
    nt = r // tr
    ttot = nb * r
    tok = lambda b, t: (b * nt + t, 0)
    tok_t = lambda b, t: (0, b * nt + t)
    out_shape = [jax.ShapeDtypeStruct((ttot, w_ssm), F32)] + [jax.ShapeDtypeStruct((ttot, w_conv), F32)] * 3 \
        + [jax.ShapeDtypeStruct((ttot, w_kv), F32)] * 2 \
        + [jax.ShapeDtypeStruct((w_attn, ttot), F32), jax.ShapeDtypeStruct((w_kv, ttot), F32)]
    out_specs = [pl.BlockSpec((tmm, w_ssm), tok)] + [pl.BlockSpec((tmm, w_conv), tok)] * 3 \
        + [pl.BlockSpec((tmm, w_kv), tok)] * 2 \
        + [pl.BlockSpec((w_attn, tmm), tok_t), pl.BlockSpec((w_kv, tmm), tok_t)]
    return pl.pallas_call(
        functools.partial(_in_proj_kernel, widths=widths),
        grid=(nb // bb, nt),
        in_specs=[pl.BlockSpec((bb, tr, d), lambda b, t: (b, t, 0)),
                  pl.BlockSpec((bb, 1, d), lambda b, t: (b, 0, 0)),
                  pl.BlockSpec((bb, 1, d), lambda b, t: (b, 0, 1)),
                  _const_spec(wrm.shape), _const_spec(wt.shape), _const_spec(kg2.shape)],
        out_specs=out_specs,
        out_shape=out_shape,
        compiler_params=_params("arbitrary", "arbitrary"),
    )(x3, mod3, mod3, wrm, wt, kg2)


SCAN_LANE_BLOCKS = 4


def _gelu_tanh(x):
    return 0.5 * x * (1.0 + jnp.tanh(math.sqrt(2.0 / math.pi) * (x + 0.044715 * (x * x * x))))


def _ssm_kernel(u_ref, h0r_ref, h0i_ref, fb_ref, sc_ref, cb_ref, dskip_ref, wglu_ref,
                y_ref, hre_ref, him_ref, bre, bim, car):
    ts, w = u_ref.shape
    ns = bre.shape[1]
    half_u = w // 2
    half_s = ns // 2
    t = pl.program_id(1)

    @pl.when(t == 0)
    def _():
        car[0:1, :] = h0r_ref[...]
        car[1:2, :] = h0i_ref[...]

    u = u_ref[...]
    ub = u.astype(BF16)
    for j in range(2):
        bb = jnp.dot(ub[:, j * half_u:(j + 1) * half_u], fb_ref[j], preferred_element_type=F32)
        bre[:, j * half_s:(j + 1) * half_s] = bb[:, :half_s]
        bim[:, j * half_s:(j + 1) * half_s] = bb[:, half_s:]

    nblk = SCAN_LANE_BLOCKS
    for c0 in range(0, ns // LANES, nblk):
        sls = [pl.ds((c0 + i) * LANES, LANES) for i in range(nblk)]
        mults = [[sc_ref[q, :, sl] for q in range(8)] for sl in sls]
        init = tuple((car[0:1, sl], car[1:2, sl]) for sl in sls)

        def body(r, carry, sls=sls, mults=mults):
            rows = pl.ds(pl.multiple_of(r * SUBLANES, SUBLANES), SUBLANES)
            new = []
            for sl, m, (cr, ci) in zip(sls, mults, carry):
                xr = bre[rows, sl]
                xi = bim[rows, sl]
                for k in range(3):
                    sr = pltpu.roll(xr, 1 << k, axis=0)
                    si = pltpu.roll(xi, 1 << k, axis=0)
                    mr, mi = m[2 * k], m[2 * k + 1]
                    xr, xi = xr + (mr * sr - mi * si), xi + (mr * si + mi * sr)
                ar, ai = m[6], m[7]
                xr, xi = xr + (ar * cr - ai * ci), xi + (ar * ci + ai * cr)
                bre[rows, sl] = xr
                bim[rows, sl] = xi
                new.append((xr[SUBLANES - 1:SUBLANES, :], xi[SUBLANES - 1:SUBLANES, :]))
            return tuple(new)

        fin = lax.fori_loop(0, ts // SUBLANES, body, init, unroll=2)
        for sl, (cr, ci) in zip(sls, fin):
            car[0:1, sl] = cr
            car[1:2, sl] = ci

    hre_ref[...] = car[0:1, :]
    him_ref[...] = car[1:2, :]

    ys = []
    for j in range(2):
        hcat = jnp.concatenate([bre[:, j * half_s:(j + 1) * half_s].astype(BF16),
                                bim[:, j * half_s:(j + 1) * half_s].astype(BF16)], axis=-1)
        ys.append(jnp.dot(hcat, cb_ref[j], preferred_element_type=F32))
    y = jnp.concatenate(ys, axis=-1) + dskip_ref[...] * u
    y = _gelu_tanh(y)
    y = y * jax.nn.sigmoid(jnp.dot(y.astype(BF16), wglu_ref[...], preferred_element_type=F32))
    y_ref[...] = y * lax.rsqrt(jnp.mean(y * y, axis=-1, keepdims=True) + EPS)


def _ssm_call(u, h0r, h0i, fb, sc, cb, dskip, wglu, nb, r):
    ttot, w = u.shape
    ns = sc.shape[-1]
    ts = min(256, r)
    nt = r // ts
    st_spec = pl.BlockSpec((None, 1, ns), lambda b, t: (b, 0, 0))
    return pl.pallas_call(
        _ssm_kernel,
        grid=(nb, nt),
        in_specs=[pl.BlockSpec((ts, w), lambda b, t: (b * nt + t, 0)), st_spec, st_spec,
                  _const_spec(fb.shape), _const_spec(sc.shape), _const_spec(cb.shape),
                  _const_spec(dskip.shape), _const_spec(wglu.shape)],
        out_specs=[pl.BlockSpec((ts, w), lambda b, t: (b * nt + t, 0)), st_spec, st_spec],
        out_shape=[jax.ShapeDtypeStruct((ttot, w), F32),
                   jax.ShapeDtypeStruct((nb, 1, ns), F32), jax.ShapeDtypeStruct((nb, 1, ns), F32)],
        scratch_shapes=[pltpu.VMEM((ts, ns), F32), pltpu.VMEM((ts, ns), F32), pltpu.VMEM((SUBLANES, ns), F32)],
        compiler_params=_params("arbitrary", "arbitrary"),
    )(u, h0r, h0i, fb, sc, cb, dskip, wglu)


def _ssm_consts(a_re, a_im, log_dt, b_re, b_im, c_re, c_im):
    g, p = a_re.shape
    hh = b_re.shape[-1]
    ar, ai = a_re.astype(F32), a_im.astype(F32)
    dt = jnp.exp(log_dt.astype(F32))[:, None]
    mag = jnp.exp(dt * ar)
    abar_re, abar_im = mag * jnp.cos(dt * ai), mag * jnp.sin(dt * ai)
    den = ar * ar + ai * ai
    f_re = ((abar_re - 1.0) * ar + abar_im * ai) / den
    f_im = (abar_im * ar - (abar_re - 1.0) * ai) / den
    fb_re = f_re[..., None] * b_re - f_im[..., None] * b_im
    fb_im = f_re[..., None] * b_im + f_im[..., None] * b_re
    gh = g // 2
    eye = jnp.eye(gh, dtype=F32)

    def in_blk(m):
        return jnp.einsum('gph,gk->ghkp', m, eye).reshape(gh * hh, gh * p)

    def out_blk(m):
        return jnp.einsum('ghp,gk->kpgh', m, eye).reshape(gh * p, gh * hh)

    fb = jnp.stack([jnp.concatenate([in_blk(fb_re[j * gh:(j + 1) * gh]), in_blk(fb_im[j * gh:(j + 1) * gh])], axis=1)
                    for j in range(2)]).astype(BF16)
    cb = jnp.stack([jnp.concatenate([out_blk(c_re[j * gh:(j + 1) * gh]), -out_blk(c_im[j * gh:(j + 1) * gh])], axis=0)
                    for j in range(2)]).astype(BF16)

    def cmul(x, y):
        return x[0] * y[0] - x[1] * y[1], x[0] * y[1] + x[1] * y[0]

    a1 = (abar_re.reshape(-1), abar_im.reshape(-1))
    a2 = cmul(a1, a1)
    a4 = cmul(a2, a2)
    row = jnp.arange(SUBLANES)[:, None]
    rows = []
    for k, a in enumerate((a1, a2, a4)):
        keep = row >= (1 << k)
        rows += [jnp.where(keep, a[0][None, :], 0.0), jnp.where(keep, a[1][None, :], 0.0)]
    pw = [a1]
    for _ in range(SUBLANES - 1):
        pw.append(cmul(pw[-1], a1))
    rows += [jnp.stack([q[0] for q in pw]), jnp.stack([q[1] for q in pw])]
    sc = jnp.stack(rows).astype(F32)
    return fb, sc, cb


def _attn_kernel(qt_ref, k_ref, vt_ref, kc_ref, vtc_ref, bias_ref, gq_ref, sink_ref, yt_ref,
                 *, sample, pairs_per_seq):
    n_rows, tq = qt_ref.shape
    npairs = tq // PAIR
    qpg = n_rows // (N_KV_HEADS * HEAD_DIM)
    i = pl.program_id(0)
    gq = gq_ref[...]
    if sample:
        k_all, vt_all = k_ref[...], vt_ref[...]
    else:
        k_all = jnp.concatenate([kc_ref[...], k_ref[...]], axis=0)
        vt_all = jnp.concatenate([vtc_ref[...], vt_ref[...]], axis=1)
    for p in range(npairs):
        if sample:
            kb = jnp.concatenate([kc_ref[p * 2 * WINDOW:(p + 1) * 2 * WINDOW, :],
                                  k_all[p * PAIR:(p + 1) * PAIR, :]], axis=0)
            vtb = jnp.concatenate([vtc_ref[:, p * 2 * WINDOW:(p + 1) * 2 * WINDOW],
                                   vt_all[:, p * PAIR:(p + 1) * PAIR]], axis=1)
            sel = 0
        else:
            kb = k_all[p * PAIR:p * PAIR + WINDOW + PAIR, :]
            vtb = vt_all[:, p * PAIR:p * PAIR + WINDOW + PAIR]
            sel = ((i * npairs + p) % pairs_per_seq == 0).astype(jnp.int32)
        kb = kb.astype(BF16)
        vtb = vtb.astype(BF16)
        for g in range(N_KV_HEADS):
            qs = []
            for hh in range(qpg):
                r0 = (g * qpg + hh) * HEAD_DIM
                q = qt_ref[r0:r0 + HEAD_DIM, p * PAIR:(p + 1) * PAIR]
                ms = jnp.sum(q * q, axis=0, keepdims=True) * (1.0 / HEAD_DIM)
                qs.append((q * lax.rsqrt(ms + EPS) * gq).astype(BF16))
            qg = jnp.concatenate(qs, axis=1)
            zero = jnp.zeros_like(qg)
            qpad = jnp.concatenate([qg, zero] if g == 0 else [zero, qg], axis=0)
            s = jnp.dot(kb, qpad, preferred_element_type=F32) + bias_ref[sel, g]
            sink = sink_ref[g]
            m = jnp.maximum(jnp.max(s, axis=0, keepdims=True), sink)
            e = jnp.exp(s - m)
            den = jnp.sum(e, axis=0, keepdims=True) + jnp.exp(sink - m)
            o = jnp.dot(vtb[g * HEAD_DIM:(g + 1) * HEAD_DIM, :], e.astype(BF16), preferred_element_type=F32)
            o = o / den
            for hh in range(qpg):
                r0 = (g * qpg + hh) * HEAD_DIM
                yt_ref[r0:r0 + HEAD_DIM, p * PAIR:(p + 1) * PAIR] = o[:, hh * PAIR:(hh + 1) * PAIR]


def _attn_call(qt, k, vt, kc, vtc, bias, gq, sink, sample, seq_len):
    n_rows, ttot = qt.shape
    kvw = k.shape[1]
    tq = min(512, ttot)
    assert ttot % tq == 0 and (sample or seq_len % tq == 0)
    npairs = tq // PAIR
    if sample:
        ctx = 2 * WINDOW * npairs
        kc_spec = pl.BlockSpec((ctx, kvw), lambda i: (i, 0))
        vtc_spec = pl.BlockSpec((kvw, ctx), lambda i: (0, i))
    else:
        kc_spec = pl.BlockSpec((WINDOW, kvw), lambda i: (jnp.maximum(i * npairs - 1, 0), 0))
        vtc_spec = pl.BlockSpec((kvw, WINDOW), lambda i: (0, jnp.maximum(i * npairs - 1, 0)))
    return pl.pallas_call(
        functools.partial(_attn_kernel, sample=sample, pairs_per_seq=seq_len // PAIR),
        grid=(ttot // tq,),
        in_specs=[pl.BlockSpec((n_rows, tq), lambda i: (0, i)),
                  pl.BlockSpec((tq, kvw), lambda i: (i, 0)),
                  pl.BlockSpec((kvw, tq), lambda i: (0, i)),
                  kc_spec, vtc_spec,
                  _const_spec(bias.shape), _const_spec(gq.shape), _const_spec(sink.shape)],
        out_specs=pl.BlockSpec((n_rows, tq), lambda i: (0, i)),
        out_shape=jax.ShapeDtypeStruct((n_rows, ttot), F32),
        compiler_params=_params("arbitrary"),
    )(qt, k, vt, kc, vtc, bias, gq, sink)


def _rms(y):
    return y * lax.rsqrt(jnp.mean(y * y, axis=-1, keepdims=True) + EPS)


def _out_proj_kernel(x_ref, gate_ref, ys_ref, yt_ref, gb_ref, gc_ref, xc_ref, hgc_ref, hxc_ref,
                     cw_ref, og_ref, wo_ref, xo_ref, zt_ref, *, sample):
    bb, tr, d = x_ref.shape
    wc = gc_ref.shape[-1]
    z = gc_ref[...] * xc_ref[...]
    if sample:
        zh = hgc_ref[...]
    else:
        zh = hgc_ref[...] * hxc_ref[...] * (pl.program_id(1) > 0).astype(F32)
    zp = jnp.concatenate([zh, z], axis=1)
    cw = cw_ref[...]
    conv = (zp[:, SUBLANES - 2:SUBLANES - 2 + tr] * cw[0:1] + zp[:, SUBLANES - 1:SUBLANES - 1 + tr] * cw[1:2]
            + z * cw[2:3])
    yc = _rms(gb_ref[...] * conv).reshape(bb * tr, wc)
    zt_ref[...] = z[:, tr - SUBLANES:, :]

    ya = _rms(yt_ref[...].T)
    y = jnp.concatenate([ys_ref[...], ya, yc], axis=-1) * og_ref[...]
    o = jnp.dot(y.astype(BF16), wo_ref[...], preferred_element_type=F32)
    xo_ref[...] = x_ref[...] + gate_ref[...] * o.reshape(bb, tr, d)


def _out_proj_call(x3, mod3, ys, yt, gb, gc, xc, halo_gc, halo_xc, cw, og, wo, sample, tm):
    nb, r, d = x3.shape
    ws, wa, wc = ys.shape[1], yt.shape[0], gb.shape[1]
    bb, tr = _tok_blocks(nb, r, tm)
    tmm = bb * tr
    nt = r // tr
    gb3, gc3, xc3 = (a.reshape(nb, r, wc) for a in (gb, gc, xc))
    tok3 = lambda b, t: (b, t, 0)
    if sample:
        halo_spec = pl.BlockSpec((bb, SUBLANES, wc), lambda b, t: (b, 0, 0))
    else:
        per = tr // SUBLANES
        halo_spec = pl.BlockSpec((bb, SUBLANES, wc), lambda b, t: (b, jnp.maximum(t * per - 1, 0), 0))
    return pl.pallas_call(
        functools.partial(_out_proj_kernel, sample=sample),
        grid=(nb // bb, nt),
        in_specs=[pl.BlockSpec((bb, tr, d), tok3),
                  pl.BlockSpec((bb, 1, d), lambda b, t: (b, 0, 2)),
                  pl.BlockSpec((tmm, ws), lambda b, t: (b * nt + t, 0)),
                  pl.BlockSpec((wa, tmm), lambda b, t: (0, b * nt + t)),
                  pl.BlockSpec((bb, tr, wc), tok3), pl.BlockSpec((bb, tr, wc), tok3),
                  pl.BlockSpec((bb, tr, wc), tok3),
                  halo_spec, halo_spec,
                  _const_spec(cw.shape), _const_spec(og.shape), _const_spec(wo.shape)],
        out_specs=[pl.BlockSpec((bb, tr, d), tok3),
                   pl.BlockSpec((bb, SUBLANES, wc), lambda b, t: (b, 0, 0))],
        out_shape=[jax.ShapeDtypeStruct((nb, r, d), F32), jax.ShapeDtypeStruct((nb, SUBLANES, wc), F32)],
        compiler_params=_params("arbitrary", "arbitrary"),
    )(x3, mod3, ys, yt, gb3, gc3, xc3, halo_gc, halo_xc, cw, og, wo)


def _ffn_kernel(x_ref, shift_ref, scale_ref, gate_ref, w1_ref, w2_ref, o_ref, h_s):
    bb, tr, d = x_ref.shape
    j = pl.program_id(2)

    @pl.when(j == 0)
    def _():
        x = x_ref[...]
        h = x * lax.rsqrt(jnp.mean(x * x, axis=-1, keepdims=True) + EPS) * (1.0 + scale_ref[...]) + shift_ref[...]
        h_s[...] = h.reshape(bb * tr, d).astype(BF16)

    a = jnp.maximum(jnp.dot(h_s[...], w1_ref[...], preferred_element_type=F32), 0.0)
    part = jnp.dot((a * a).astype(BF16), w2_ref[...], preferred_element_type=F32).reshape(bb, tr, d)

    @pl.when(j == 0)
    def _():
        o_ref[...] = part

    @pl.when(j > 0)
    def _():
        o_ref[...] += part

    @pl.when(j == pl.num_programs(2) - 1)
    def _():
        o_ref[...] = x_ref[...] + gate_ref[...] * o_ref[...]


def _ffn_call(x3, mod3, w1, w2, tm, tf):
    nb, r, d = x3.shape
    dff = w1.shape[1]
    bb, tr = _tok_blocks(nb, r, tm)
    tok3 = lambda b, t, j: (b, t, 0)
    mod_spec = lambda c: pl.BlockSpec((bb, 1, d), lambda b, t, j: (b, 0, c))
    return pl.pallas_call(
        _ffn_kernel,
        grid=(nb // bb, r // tr, dff // tf),
        in_specs=[pl.BlockSpec((bb, tr, d), tok3), mod_spec(3), mod_spec(4), mod_spec(5),
                  pl.BlockSpec((d, tf), lambda b, t, j: (0, j)),
                  pl.BlockSpec((tf, d), lambda b, t, j: (j, 0))],
        out_specs=pl.BlockSpec((bb, tr, d), tok3),
        out_shape=jax.ShapeDtypeStruct((nb, r, d), F32),
        scratch_shapes=[pltpu.VMEM((bb * tr, d), BF16)],
        compiler_params=_params("arbitrary", "arbitrary", "arbitrary"),
    )(x3, mod3, mod3, mod3, w1, w2)


def _layer(x3, mod3, lw, bias, sample, state):
    nb, r, d = x3.shape
    widths = lw["widths"]
    u, gb, gc, xc, k, v, qt, vt = _in_proj_call(x3, mod3, lw["wrm"], lw["wt"], lw["kg2"], widths, tm=256)

    ns = lw["sc"].shape[-1]
    if sample:
        cache_k, cache_v, h0r, h0i, conv_buf = state
        h0r = h0r.reshape(nb, 1, ns)
        h0i = h0i.reshape(nb, 1, ns)
    else:
        h0r = h0i = jnp.zeros((nb, 1, ns), F32)
    ys, hre, him = _ssm_call(u, h0r, h0i, lw["fb"], lw["sc"], lw["cb"], lw["dskip"], lw["wglu"], nb, r)

    kvw = k.shape[1]
    if sample:
        n_buf = cache_k.shape[1]
        kc = cache_k.reshape(nb * n_buf, kvw)
        vtc = cache_v.reshape(nb * n_buf, kvw).T
    else:
        kc, vtc = k, vt
    yt = _attn_call(qt, k, vt, kc, vtc, bias, lw["gq"], lw["sink"], sample, r)

    wc = gb.shape[1]
    if sample:
        halo = jnp.pad(conv_buf, ((0, 0), (SUBLANES - (CONV_K - 1), 0), (0, 0)))
        halo_gc = halo_xc = halo
    else:
        halo_gc, halo_xc = gc.reshape(nb, r, wc), xc.reshape(nb, r, wc)
    x3, ztail = _out_proj_call(x3, mod3, ys, yt, gb, gc, xc, halo_gc, halo_xc,
                               lw["cw"], lw["og"], lw["wo"], sample, tm=256)
    x3 = _ffn_call(x3, mod3, lw["w1"], lw["w2"], tm=1024, tf=512)

    k4 = k.reshape(nb, r, N_KV_HEADS, HEAD_DIM)
    v4 = v.reshape(nb, r, N_KV_HEADS, HEAD_DIM)
    if sample:
        new_k = jnp.concatenate([cache_k, k4], axis=1)[:, -n_buf:]
        new_v = jnp.concatenate([cache_v, v4], axis=1)[:, -n_buf:]
    else:
        new_k, new_v = k4[:, -WINDOW:], v4[:, -WINDOW:]
    g = ns // SSM_STATE
    return (x3, new_k, new_v, hre.reshape(nb, g, SSM_STATE), him.reshape(nb, g, SSM_STATE),
            ztail[:, -(CONV_K - 1):])


def _layer_weights(l, w_in, ssm_a_re, ssm_a_im, ssm_log_dt, ssm_b_re, ssm_b_im, ssm_c_re, ssm_c_im, ssm_d,
                   ssm_w_glu, q_norm_g, k_norm_g, attn_sinks, conv_w, out_norm_g, w_out, w_ff1, w_ff2):
    w_ssm = ssm_d.shape[1]
    n_heads = attn_sinks.shape[1]
    w_attn = n_heads * HEAD_DIM
    w_kv = N_KV_HEADS * HEAD_DIM
    w_conv = conv_w.shape[1]
    qpg = n_heads // N_KV_HEADS
    wi = w_in[l]
    o = np.cumsum([0, w_ssm, w_attn, w_kv, w_kv, w_conv, w_conv, w_conv])
    wu, wq, wk, wv, wgb, wgc, wxc = (wi[:, o[i]:o[i + 1]] for i in range(7))
    fb, sc, cb = _ssm_consts(ssm_a_re[l], ssm_a_im[l], ssm_log_dt[l], ssm_b_re[l], ssm_b_im[l],
                             ssm_c_re[l], ssm_c_im[l])
    return {
        "widths": (w_ssm, w_conv, w_kv, w_attn),
        "wrm": jnp.concatenate([wu, wgb, wgc, wxc, wk, wv], axis=1).astype(BF16),
        "wt": jnp.concatenate([wq, wv], axis=1).T.astype(BF16),
        "kg2": jnp.tile(k_norm_g[l].astype(F32), N_KV_HEADS)[None, :],
        "fb": fb, "sc": sc, "cb": cb,
        "dskip": ssm_d[l].astype(F32)[None, :],
        "wglu": ssm_w_glu[l].astype(BF16),
        "gq": jnp.broadcast_to((q_norm_g[l].astype(F32) * HEAD_DIM ** -0.5)[:, None], (HEAD_DIM, PAIR)),
        "sink": jnp.broadcast_to(attn_sinks[l].astype(F32).reshape(N_KV_HEADS, 1, qpg, 1),
                                 (N_KV_HEADS, 1, qpg, PAIR)).reshape(N_KV_HEADS, 1, qpg * PAIR),
        "cw": conv_w[l].astype(F32).T,
        "og": out_norm_g[l].astype(F32)[None, :],
        "wo": w_out[l].astype(BF16),
        "w1": w_ff1[l].astype(BF16),
        "w2": w_ff2[l].astype(BF16),
    }


def kernel(x_prompt, x_sample, cache_k, cache_v, state_ssm_re, state_ssm_im, state_conv, c_prompt, c_sample, rel_bias, w_ada, b_ada, w_in, ssm_a_re, ssm_a_im, ssm_log_dt, ssm_b_re, ssm_b_im, ssm_c_re, ssm_c_im, ssm_d, ssm_w_glu, q_norm_g, k_norm_g, attn_sinks, conv_w, out_norm_g, w_out, w_ff1, w_ff2):
    depth = w_in.shape[0]
    nbp, nbs = x_prompt.shape[0], x_sample.shape[0]
    assert x_sample.shape[1] == CHUNK and cache_k.shape[2] == WINDOW and nbs % 2 == 0

    nc = nbp + nbs
    ncp = -(-nc // SUBLANES) * SUBLANES
    c_all = jnp.pad(jnp.concatenate([c_prompt, c_sample], axis=0), ((0, ncp - nc), (0, 0)))
    mod = _mod_call(c_all, w_ada, b_ada)

    bias_p = _bias_call(rel_bias, sample=False)
    bias_s = _bias_call(rel_bias, sample=True)

    xp, xs = x_prompt, x_sample
    outs_p, outs_s = [], []
    for l in range(depth):
        lw = _layer_weights(l, w_in, ssm_a_re, ssm_a_im, ssm_log_dt, ssm_b_re, ssm_b_im, ssm_c_re, ssm_c_im,
                            ssm_d, ssm_w_glu, q_norm_g, k_norm_g, attn_sinks, conv_w, out_norm_g, w_out,
                            w_ff1, w_ff2)
        mod_p = mod[l, :nbp][:, None, :]
        mod_s = mod[l, nbp:nc][:, None, :]
        xp, *rest_p = _layer(xp, mod_p, lw, bias_p, False, None)
        xs, *rest_s = _layer(xs, mod_s, lw, bias_s, True,
                             (cache_k[l], cache_v[l], state_ssm_re[l], state_ssm_im[l], state_conv[l]))
        outs_p.append(rest_p)
        outs_s.append(rest_s)
    stack = lambda outs, i: jnp.stack([o[i] for o in outs])
    return (xp, xs,
            *(stack(outs_p, i) for i in range(5)),
            *(stack(outs_s, i) for i in range(5)))
```

```python
import functools
import math

import numpy as np
import jax
import jax.numpy as jnp
from jax import lax
from jax.experimental import pallas as pl
from jax.experimental.pallas import tpu as pltpu

F32 = jnp.float32
BF16 = jnp.bfloat16

CHUNK = 64
SSM_GROUP = 16
SSM_STATE = 64
HEAD_DIM = 64
N_KV_HEADS = 2
WINDOW = 128
CONV_K = 3
REL_BUCKETS = 32
REL_MAX_DIST = 64
EPS = 1e-6
NEG_INF = -1e30

LANES = 128
SUBLANES = 8
PAIR = 2 * CHUNK
VMEM_LIMIT_BYTES = 56 * 1024 * 1024


def _params(*sem):
    return pltpu.CompilerParams(dimension_semantics=sem, vmem_limit_bytes=VMEM_LIMIT_BYTES)


def _const_spec(shape):
    nd = len(shape)
    return pl.BlockSpec(shape, lambda *_: (0,) * nd, pipeline_mode=pl.Buffered(1))


def _tok_blocks(nb, r, tm):
    if r >= tm:
        assert r % tm == 0
        return 1, tm
    bb = min(tm // r, nb)
    assert nb % bb == 0
    return bb, r


def _mod_kernel(c_ref, w_ref, b_ref, o_ref):
    c = c_ref[...]
    s = (c * jax.nn.sigmoid(c)).astype(BF16)
    o_ref[...] = jnp.dot(s, w_ref[...].astype(BF16), preferred_element_type=F32) + b_ref[...]


def _mod_call(c_all, w_ada, b_ada):
    depth, d, n = w_ada.shape
    nc = c_all.shape[0]
    tn = 1024
    return pl.pallas_call(
        _mod_kernel,
        grid=(depth, n // tn),
        in_specs=[pl.BlockSpec((nc, d), lambda l, j: (0, 0)),
                  pl.BlockSpec((None, d, tn), lambda l, j: (l, 0, j)),
                  pl.BlockSpec((None, 1, tn), lambda l, j: (l, 0, j))],
        out_specs=pl.BlockSpec((None, nc, tn), lambda l, j: (l, 0, j)),
        out_shape=jax.ShapeDtypeStruct((depth, nc, n), F32),
        compiler_params=_params("arbitrary", "arbitrary"),
    )(c_all, w_ada, b_ada.reshape(depth, 1, n))


def _bucket_maps(sample):
    lane = np.arange(PAIR)[None, :]
    if sample:
        nk = 2 * WINDOW + PAIR
        row = np.arange(nk)[:, None]
        own = row >= 2 * WINDOW
        key_b = np.where(own, (row - 2 * WINDOW) // CHUNK, row // WINDOW)
        key_s = np.where(own, WINDOW + (row - 2 * WINDOW) % CHUNK, row % WINDOW)
        rel = key_s - WINDOW - lane % CHUNK
        visible = key_b == lane // CHUNK
    else:
        nk = WINDOW + PAIR
        row = np.arange(nk)[:, None]
        rel = row - WINDOW - lane
        dc = row // CHUNK - lane // CHUNK
        visible = (dc >= 0) & (dc <= WINDOW // CHUNK)
    half = REL_BUCKETS // 2
    exact = half // 2
    n = np.abs(rel)
    nf = np.maximum(n, 1).astype(np.float32)
    far = exact + (np.log(nf / np.float32(exact)) / np.float32(math.log(REL_MAX_DIST / exact))
                   * np.float32(half - exact)).astype(np.int32)
    far = np.minimum(far, half - 1)
    bucket = np.where(rel > 0, half, 0) + np.where(n < exact, n, far)
    full = np.where(visible, bucket, -1).astype(np.int32)
    first = np.where(row >= WINDOW, full, -1).astype(np.int32)
    return np.stack([full, first])


def _bias_kernel(table_ref, bucket_ref, o_ref):
    h = pl.program_id(1)
    bucket = bucket_ref[...]
    acc = jnp.full(bucket.shape, NEG_INF, F32)
    for b in range(REL_BUCKETS):
        acc = jnp.where(bucket == b, table_ref[b, h], acc)
    o_ref[...] = acc


def _bias_call(rel_bias, sample):
    buckets = jnp.asarray(_bucket_maps(sample))
    _, nk, _ = buckets.shape
    n_heads = rel_bias.shape[1]
    qpg = n_heads // N_KV_HEADS
    return pl.pallas_call(
        _bias_kernel,
        grid=(2, n_heads),
        in_specs=[pl.BlockSpec(memory_space=pltpu.SMEM),
                  pl.BlockSpec((None, nk, PAIR), lambda s, h: (s, 0, 0))],
        out_specs=pl.BlockSpec((None, None, nk, PAIR), lambda s, h: (s, h // qpg, 0, h % qpg)),
        out_shape=jax.ShapeDtypeStruct((2, N_KV_HEADS, nk, qpg * PAIR), F32),
        compiler_params=_params("arbitrary", "arbitrary"),
    )(rel_bias, buckets)


def _in_proj_kernel(x_ref, shift_ref, scale_ref, wrm_ref, wt_ref, kg_ref,
                    u_ref, gb_ref, gc_ref, xc_ref, k_ref, v_ref, qt_ref, vt_ref, *, widths):
    bb, tr, d = x_ref.shape
    x = x_ref[...]
    ms = jnp.mean(x * x, axis=-1, keepdims=True)
    h = x * lax.rsqrt(ms + EPS) * (1.0 + scale_ref[...]) + shift_ref[...]
    h = h.reshape(bb * tr, d).astype(BF16)

    p = jnp.dot(h, wrm_ref[...], preferred_element_type=F32)
    w_ssm, w_conv, w_kv, w_attn = widths
    o = 0
    u_ref[...] = p[:, o:o + w_ssm]
    o += w_ssm
    gb_ref[...] = p[:, o:o + w_conv]
    o += w_conv
    gc_ref[...] = p[:, o:o + w_conv]
    o += w_conv
    xc_ref[...] = p[:, o:o + w_conv]
    o += w_conv
    k = p[:, o:o + w_kv]
    o += w_kv
    v_ref[...] = p[:, o:o + w_kv]

    lo = lax.broadcasted_iota(jnp.int32, k.shape, 1) < HEAD_DIM
    k2 = k * k
    s_lo = jnp.sum(jnp.where(lo, k2, 0.0), axis=-1, keepdims=True)
    s_hi = jnp.sum(jnp.where(lo, 0.0, k2), axis=-1, keepdims=True)
    ssq = jnp.where(lo, s_lo, s_hi)
    k_ref[...] = k * lax.rsqrt(ssq * (1.0 / HEAD_DIM) + EPS) * kg_ref[...]

    pt = lax.dot_general(wt_ref[...], h, (((1,), (1,)), ((), ())), preferred_element_type=F32)
    qt_ref[...] = pt[:w_attn]
    vt_ref[...] = pt[w_attn:]


def _in_proj_call(x3, mod3, wrm, wt, kg2, widths, tm):
    nb, r, d = x3.shape
    w_ssm, w_conv, w_kv, w_attn = widths
    bb, tr = _tok_blocks(nb, r, tm)
    tmm = bb * tr
    nt = r // tr
    ttot = nb * r
    tok = lambda b, t: (b * nt + t, 0)
    tok_t = lambda b, t: (0, b * nt + t)
    out_shape = [jax.ShapeDtypeStruct((ttot, w_ssm), F32)] + [jax.ShapeDtypeStruct((ttot, w_conv), F32)] * 3 \
        + [jax.ShapeDtypeStruct((ttot, w_kv), F32)] * 2 \
        + [jax.ShapeDtypeStruct((w_attn, ttot), F32), jax.ShapeDtypeStruct((w_kv, ttot), F32)]
    out_specs = [pl.BlockSpec((tmm, w_ssm), tok)] + [pl.BlockSpec((tmm, w_conv), tok)] * 3 \
        + [pl.BlockSpec((tmm, w_kv), tok)] * 2 \
        + [pl.BlockSpec((w_attn, tmm), tok_t), pl.BlockSpec((w_kv, tmm), tok_t)]
    return pl.pallas_call(
        functools.partial(_in_proj_kernel, widths=widths),
        grid=(nb // bb, nt),
        in_specs=[pl.BlockSpec((bb, tr, d), lambda b, t: (b, t, 0)),
                  pl.BlockSpec((bb, 1, d), lambda b, t: (b, 0, 0)),
                  pl.BlockSpec((bb, 1, d), lambda b, t: (b, 0, 1)),
                  _const_spec(wrm.shape), _const_spec(wt.shape), _const_spec(kg2.shape)],
        out_specs=out_specs,
        out_shape=out_shape,
        compiler_params=_params("arbitrary", "arbitrary"),
    )(x3, mod3, mod3, wrm, wt, kg2)


SCAN_LANE_BLOCKS = 8


def _gelu_tanh(x):
    return 0.5 * x * (1.0 + jnp.tanh(math.sqrt(2.0 / math.pi) * (x + 0.044715 * (x * x * x))))


def _segment_perm(ts):
    lseg = ts // SUBLANES
    p = np.zeros((ts, ts), np.float32)
    i, j = np.meshgrid(np.arange(lseg), np.arange(SUBLANES), indexing="ij")
    p[(i * SUBLANES + j).ravel(), (j * lseg + i).ravel()] = 1.0
    return p


def _ssm_kernel(u_ref, h0r_ref, h0i_ref, perm_ref, permt_ref, fb_ref, ab_ref, sg_ref, pw_ref, cb_ref, dskip_ref,
                wglu_ref, y_ref, hre_ref, him_ref, bre, bim, car):
    ts, w = u_ref.shape
    ns = bre.shape[1]
    lseg = ts // SUBLANES
    half_u = w // 2
    half_s = ns // 2
    t = pl.program_id(1)

    @pl.when(t == 0)
    def _():
        car[0:1, :] = h0r_ref[...]
        car[1:2, :] = h0i_ref[...]

    u = u_ref[...]
    up = jnp.dot(perm_ref[...], u.astype(BF16), preferred_element_type=F32).astype(BF16)
    for j in range(2):
        bb = jnp.dot(up[:, j * half_u:(j + 1) * half_u], fb_ref[j], preferred_element_type=F32)
        bre[:, j * half_s:(j + 1) * half_s] = bb[:, :half_s]
        bim[:, j * half_s:(j + 1) * half_s] = bb[:, half_s:]

    row0 = lax.broadcasted_iota(jnp.int32, (SUBLANES, LANES), 0) == 0
    zero = jnp.zeros((SUBLANES, LANES), F32)
    nblk = SCAN_LANE_BLOCKS
    for c0 in range(0, ns // LANES, nblk):
        sls = [pl.ds((c0 + i) * LANES, LANES) for i in range(nblk)]
        ab = [(ab_ref[0, :, sl], ab_ref[1, :, sl]) for sl in sls]

        def pass1(i, carry, sls=sls, ab=ab):
            rows = pl.ds(pl.multiple_of(i * SUBLANES, SUBLANES), SUBLANES)
            new = []
            for sl, (ar, ai), (hr, hi) in zip(sls, ab, carry):
                hr, hi = ar * hr - ai * hi + bre[rows, sl], ar * hi + ai * hr + bim[rows, sl]
                bre[rows, sl] = hr
                bim[rows, sl] = hi
                new.append((hr, hi))
            return tuple(new)

        ends = lax.fori_loop(0, lseg, pass1, tuple((zero, zero) for _ in sls), unroll=2)

        starts = []
        for sl, (er, ei) in zip(sls, ends):
            xr = jnp.where(row0, car[0:1, sl], pltpu.roll(er, 1, axis=0))
            xi = jnp.where(row0, car[1:2, sl], pltpu.roll(ei, 1, axis=0))
            for k in range(3):
                sr = pltpu.roll(xr, 1 << k, axis=0)
                si = pltpu.roll(xi, 1 << k, axis=0)
                mr, mi = sg_ref[2 * k, :, sl], sg_ref[2 * k + 1, :, sl]
                xr, xi = xr + (mr * sr - mi * si), xi + (mr * si + mi * sr)
            lr, li = sg_ref[6, :, sl], sg_ref[7, :, sl]
            nr = lr * xr - li * xi + er
            ni = lr * xi + li * xr + ei
            car[0:1, sl] = nr[SUBLANES - 1:SUBLANES, :]
            car[1:2, sl] = ni[SUBLANES - 1:SUBLANES, :]
            starts.append((xr, xi))

        def pass2(i, c, sls=sls, starts=starts):
            rows = pl.ds(pl.multiple_of(i * SUBLANES, SUBLANES), SUBLANES)
            for sl, (sr, si) in zip(sls, starts):
                pr = pw_ref[0, rows, sl]
                pi = pw_ref[1, rows, sl]
                bre[rows, sl] += pr * sr - pi * si
                bim[rows, sl] += pr * si + pi * sr
            return c

        lax.fori_loop(0, lseg, pass2, 0, unroll=2)

    hre_ref[...] = car[0:1, :]
    him_ref[...] = car[1:2, :]

    ys = []
    for j in range(2):
        hcat = jnp.concatenate([bre[:, j * half_s:(j + 1) * half_s].astype(BF16),
                                bim[:, j * half_s:(j + 1) * half_s].astype(BF16)], axis=-1)
        ys.append(jnp.dot(hcat, cb_ref[j], preferred_element_type=F32))
    yp = jnp.concatenate(ys, axis=-1)
    p0 = yp.astype(BF16)
    r1 = yp - p0.astype(F32)
    p1 = r1.astype(BF16)
    p2 = (r1 - p1.astype(F32)).astype(BF16)
    parts = jnp.dot(permt_ref[...], jnp.concatenate([p0, p1, p2], axis=-1), preferred_element_type=F32)
    y = (parts[:, :w] + parts[:, w:2 * w]) + parts[:, 2 * w:] + dskip_ref[...] * u
    y = _gelu_tanh(y)
    y = y * jax.nn.sigmoid(jnp.dot(y.astype(BF16), wglu_ref[...], preferred_element_type=F32))
    y_ref[...] = y * lax.rsqrt(jnp.mean(y * y, axis=-1, keepdims=True) + EPS)


SSM_TILE = 256


def _ssm_call(u, h0r, h0i, fb, scan, cb, dskip, wglu, nb, r):
    ttot, w = u.shape
    ab, sg, pw = scan
    ns = ab.shape[-1]
    ts = min(SSM_TILE, r)
    assert pw.shape[1] == ts
    nt = r // ts
    perm = _segment_perm(ts)
    perm, permt = jnp.asarray(perm, BF16), jnp.asarray(perm.T, BF16)
    st_spec = pl.BlockSpec((None, 1, ns), lambda b, t: (b, 0, 0))
    consts = (perm, permt, fb, ab, sg, pw, cb, dskip, wglu)
    return pl.pallas_call(
        _ssm_kernel,
        grid=(nb, nt),
        in_specs=[pl.BlockSpec((ts, w), lambda b, t: (b * nt + t, 0)), st_spec, st_spec]
        + [_const_spec(c.shape) for c in consts],
        out_specs=[pl.BlockSpec((ts, w), lambda b, t: (b * nt + t, 0)), st_spec, st_spec],
        out_shape=[jax.ShapeDtypeStruct((ttot, w), F32),
                   jax.ShapeDtypeStruct((nb, 1, ns), F32), jax.ShapeDtypeStruct((nb, 1, ns), F32)],
        scratch_shapes=[pltpu.VMEM((ts, ns), F32), pltpu.VMEM((ts, ns), F32), pltpu.VMEM((SUBLANES, ns), F32)],
        compiler_params=_params("arbitrary", "arbitrary"),
    )(u, h0r, h0i, *consts)


def _scan_consts(a1, lseg):
    def cmul(x, y):
        return x[0] * y[0] - x[1] * y[1], x[0] * y[1] + x[1] * y[0]

    tr, ti = a1[0][None], a1[1][None]
    while tr.shape[0] < lseg:
        nr, ni = cmul((tr, ti), (tr[-1], ti[-1]))
        tr, ti = jnp.concatenate([tr, nr]), jnp.concatenate([ti, ni])
    pw = jnp.repeat(jnp.stack([tr[:lseg], ti[:lseg]]), SUBLANES, axis=1)
    ns = a1[0].shape[0]
    ab = jnp.stack([jnp.broadcast_to(a1[0], (SUBLANES, ns)), jnp.broadcast_to(a1[1], (SUBLANES, ns))])
    al = (tr[lseg - 1], ti[lseg - 1])
    al2 = cmul(al, al)
    al4 = cmul(al2, al2)
    row = jnp.arange(SUBLANES)[:, None]
    rows = []
    for k, a in enumerate((al, al2, al4)):
        keep = row >= (1 << k)
        rows += [jnp.where(keep, a[0][None, :], 0.0), jnp.where(keep, a[1][None, :], 0.0)]
    rows += [jnp.broadcast_to(al[0], (SUBLANES, ns)), jnp.broadcast_to(al[1], (SUBLANES, ns))]
    return ab, jnp.stack(rows), pw


def _ssm_consts(a_re, a_im, log_dt, b_re, b_im, c_re, c_im):
    g, p = a_re.shape
    hh = b_re.shape[-1]
    ar, ai = a_re.astype(F32), a_im.astype(F32)
    dt = jnp.exp(log_dt.astype(F32))[:, None]
    mag = jnp.exp(dt * ar)
    abar_re, abar_im = mag * jnp.cos(dt * ai), mag * jnp.sin(dt * ai)
    den = ar * ar + ai * ai
    f_re = ((abar_re - 1.0) * ar + abar_im * ai) / den
    f_im = (abar_im * ar - (abar_re - 1.0) * ai) / den
    fb_re = f_re[..., None] * b_re - f_im[..., None] * b_im
    fb_im = f_re[..., None] * b_im + f_im[..., None] * b_re
    gh = g // 2
    eye = jnp.eye(gh, dtype=F32)

    def in_blk(m):
        return jnp.einsum('gph,gk->ghkp', m, eye).reshape(gh * hh, gh * p)

    def out_blk(m):
        return jnp.einsum('ghp,gk->kpgh', m, eye).reshape(gh * p, gh * hh)

    fb = jnp.stack([jnp.concatenate([in_blk(fb_re[j * gh:(j + 1) * gh]), in_blk(fb_im[j * gh:(j + 1) * gh])], axis=1)
                    for j in range(2)]).astype(BF16)
    cb = jnp.stack([jnp.concatenate([out_blk(c_re[j * gh:(j + 1) * gh]), -out_blk(c_im[j * gh:(j + 1) * gh])], axis=0)
                    for j in range(2)]).astype(BF16)

    return fb, (abar_re.reshape(-1), abar_im.reshape(-1)), cb


def _attn_kernel(qt_ref, k_ref, vt_ref, kc_ref, vtc_ref, bias_ref, gq_ref, sink_ref, yt_ref,
                 *, sample, pairs_per_seq):
    n_rows, tq = qt_ref.shape
    npairs = tq // PAIR
    qpg = n_rows // (N_KV_HEADS * HEAD_DIM)
    i = pl.program_id(0)
    gq = gq_ref[...]
    if sample:
        k_all, vt_all = k_ref[...], vt_ref[...]
    else:
        k_all = jnp.concatenate([kc_ref[...], k_ref[...]], axis=0)
        vt_all = jnp.concatenate([vtc_ref[...], vt_ref[...]], axis=1)
    for p in range(npairs):
        if sample:
            kb = jnp.concatenate([kc_ref[p * 2 * WINDOW:(p + 1) * 2 * WINDOW, :],
                                  k_all[p * PAIR:(p + 1) * PAIR, :]], axis=0)
            vtb = jnp.concatenate([vtc_ref[:, p * 2 * WINDOW:(p + 1) * 2 * WINDOW],
                                   vt_all[:, p * PAIR:(p + 1) * PAIR]], axis=1)
            sel = 0
        else:
            kb = k_all[p * PAIR:p * PAIR + WINDOW + PAIR, :]
            vtb = vt_all[:, p * PAIR:p * PAIR + WINDOW + PAIR]
            sel = ((i * npairs + p) % pairs_per_seq == 0).astype(jnp.int32)
        kb = kb.astype(BF16)
        vtb = vtb.astype(BF16)
        for g in range(N_KV_HEADS):
            qs = []
            for hh in range(qpg):
                r0 = (g * qpg + hh) * HEAD_DIM
                q = qt_ref[r0:r0 + HEAD_DIM, p * PAIR:(p + 1) * PAIR]
                ms = jnp.sum(q * q, axis=0, keepdims=True) * (1.0 / HEAD_DIM)
                qs.append((q * lax.rsqrt(ms + EPS) * gq).astype(BF16))
            qg = jnp.concatenate(qs, axis=1)
            zero = jnp.zeros_like(qg)
            qpad = jnp.concatenate([qg, zero] if g == 0 else [zero, qg], axis=0)
            s = jnp.dot(kb, qpad, preferred_element_type=F32) + bias_ref[sel, g]
            sink = sink_ref[g]
            m = jnp.maximum(jnp.max(s, axis=0, keepdims=True), sink)
            e = jnp.exp(s - m)
            den = jnp.sum(e, axis=0, keepdims=True) + jnp.exp(sink - m)
            o = jnp.dot(vtb[g * HEAD_DIM:(g + 1) * HEAD_DIM, :], e.astype(BF16), preferred_element_type=F32)
            o = o / den
            for hh in range(qpg):
                r0 = (g * qpg + hh) * HEAD_DIM
                yt_ref[r0:r0 + HEAD_DIM, p * PAIR:(p + 1) * PAIR] = o[:, hh * PAIR:(hh + 1) * PAIR]


def _attn_call(qt, k, vt, kc, vtc, bias, gq, sink, sample, seq_len):
    n_rows, ttot = qt.shape
    kvw = k.shape[1]
    tq = min(512, ttot)
    assert ttot % tq == 0 and (sample or seq_len % tq == 0)
    npairs = tq // PAIR
    if sample:
        ctx = 2 * WINDOW * npairs
        kc_spec = pl.BlockSpec((ctx, kvw), lambda i: (i, 0))
        vtc_spec = pl.BlockSpec((kvw, ctx), lambda i: (0, i))
    else:
        kc_spec = pl.BlockSpec((WINDOW, kvw), lambda i: (jnp.maximum(i * npairs - 1, 0), 0))
        vtc_spec = pl.BlockSpec((kvw, WINDOW), lambda i: (0, jnp.maximum(i * npairs - 1, 0)))
    return pl.pallas_call(
        functools.partial(_attn_kernel, sample=sample, pairs_per_seq=seq_len // PAIR),
        grid=(ttot // tq,),
        in_specs=[pl.BlockSpec((n_rows, tq), lambda i: (0, i)),
                  pl.BlockSpec((tq, kvw), lambda i: (i, 0)),
                  pl.BlockSpec((kvw, tq), lambda i: (0, i)),
                  kc_spec, vtc_spec,
                  _const_spec(bias.shape), _const_spec(gq.shape), _const_spec(sink.shape)],
        out_specs=pl.BlockSpec((n_rows, tq), lambda i: (0, i)),
        out_shape=jax.ShapeDtypeStruct((n_rows, ttot), F32),
        compiler_params=_params("arbitrary"),
    )(qt, k, vt, kc, vtc, bias, gq, sink)


def _rms(y):
    return y * lax.rsqrt(jnp.mean(y * y, axis=-1, keepdims=True) + EPS)


def _out_proj_kernel(x_ref, gate_ref, ys_ref, yt_ref, gb_ref, gc_ref, xc_ref, hgc_ref, hxc_ref,
                     cw_ref, og_ref, wo_ref, xo_ref, zt_ref, *, sample):
    bb, tr, d = x_ref.shape
    wc = gc_ref.shape[-1]
    z = gc_ref[...] * xc_ref[...]
    if sample:
        zh = hgc_ref[...]
    else:
        zh = hgc_ref[...] * hxc_ref[...] * (pl.program_id(1) > 0).astype(F32)
    zp = jnp.concatenate([zh, z], axis=1)
    cw = cw_ref[...]
    conv = (zp[:, SUBLANES - 2:SUBLANES - 2 + tr] * cw[0:1] + zp[:, SUBLANES - 1:SUBLANES - 1 + tr] * cw[1:2]
            + z * cw[2:3])
    yc = _rms(gb_ref[...] * conv).reshape(bb * tr, wc)
    zt_ref[...] = z[:, tr - SUBLANES:, :]

    ya = _rms(yt_ref[...].T)
    y = jnp.concatenate([ys_ref[...], ya, yc], axis=-1) * og_ref[...]
    o = jnp.dot(y.astype(BF16), wo_ref[...], preferred_element_type=F32)
    xo_ref[...] = x_ref[...] + gate_ref[...] * o.reshape(bb, tr, d)


def _out_proj_call(x3, mod3, ys, yt, gb, gc, xc, halo_gc, halo_xc, cw, og, wo, sample, tm):
    nb, r, d = x3.shape
    ws, wa, wc = ys.shape[1], yt.shape[0], gb.shape[1]
    bb, tr = _tok_blocks(nb, r, tm)
    tmm = bb * tr
    nt = r // tr
    gb3, gc3, xc3 = (a.reshape(nb, r, wc) for a in (gb, gc, xc))
    tok3 = lambda b, t: (b, t, 0)
    if sample:
        halo_spec = pl.BlockSpec((bb, SUBLANES, wc), lambda b, t: (b, 0, 0))
    else:
        per = tr // SUBLANES
        halo_spec = pl.BlockSpec((bb, SUBLANES, wc), lambda b, t: (b, jnp.maximum(t * per - 1, 0), 0))
    return pl.pallas_call(
        functools.partial(_out_proj_kernel, sample=sample),
        grid=(nb // bb, nt),
        in_specs=[pl.BlockSpec((bb, tr, d), tok3),
                  pl.BlockSpec((bb, 1, d), lambda b, t: (b, 0, 2)),
                  pl.BlockSpec((tmm, ws), lambda b, t: (b * nt + t, 0)),
                  pl.BlockSpec((wa, tmm), lambda b, t: (0, b * nt + t)),
                  pl.BlockSpec((bb, tr, wc), tok3), pl.BlockSpec((bb, tr, wc), tok3),
                  pl.BlockSpec((bb, tr, wc), tok3),
                  halo_spec, halo_spec,
                  _const_spec(cw.shape), _const_spec(og.shape), _const_spec(wo.shape)],
        out_specs=[pl.BlockSpec((bb, tr, d), tok3),
                   pl.BlockSpec((bb, SUBLANES, wc), lambda b, t: (b, 0, 0))],
        out_shape=[jax.ShapeDtypeStruct((nb, r, d), F32), jax.ShapeDtypeStruct((nb, SUBLANES, wc), F32)],
        compiler_params=_params("arbitrary", "arbitrary"),
    )(x3, mod3, ys, yt, gb3, gc3, xc3, halo_gc, halo_xc, cw, og, wo)


def _ffn_kernel(x_ref, shift_ref, scale_ref, gate_ref, w1_ref, w2_ref, o_ref, h_s):
    bb, tr, d = x_ref.shape
    j = pl.program_id(2)

    @pl.when(j == 0)
    def _():
        x = x_ref[...]
        h = x * lax.rsqrt(jnp.mean(x * x, axis=-1, keepdims=True) + EPS) * (1.0 + scale_ref[...]) + shift_ref[...]
        h_s[...] = h.reshape(bb * tr, d).astype(BF16)
        o_ref[...] = jnp.zeros_like(o_ref)

    a = jnp.maximum(jnp.dot(h_s[...], w1_ref[...], preferred_element_type=F32), 0.0)
    o_ref[...] += jnp.dot((a * a).astype(BF16), w2_ref[...], preferred_element_type=F32).reshape(bb, tr, d)

    @pl.when(j == pl.num_programs(2) - 1)
    def _():
        o_ref[...] = x_ref[...] + gate_ref[...] * o_ref[...]


def _ffn_call(x3, mod3, w1, w2, tm, tf):
    nb, r, d = x3.shape
    dff = w1.shape[1]
    bb, tr = _tok_blocks(nb, r, tm)
    tok3 = lambda b, t, j: (b, t, 0)
    mod_spec = lambda c: pl.BlockSpec((bb, 1, d), lambda b, t, j: (b, 0, c))
    return pl.pallas_call(
        _ffn_kernel,
        grid=(nb // bb, r // tr, dff // tf),
        in_specs=[pl.BlockSpec((bb, tr, d), tok3), mod_spec(3), mod_spec(4), mod_spec(5),
                  pl.BlockSpec((d, tf), lambda b, t, j: (0, j)),
                  pl.BlockSpec((tf, d), lambda b, t, j: (j, 0))],
        out_specs=pl.BlockSpec((bb, tr, d), tok3),
        out_shape=jax.ShapeDtypeStruct((nb, r, d), F32),
        scratch_shapes=[pltpu.VMEM((bb * tr, d), BF16)],
        compiler_params=_params("arbitrary", "arbitrary", "arbitrary"),
    )(x3, mod3, mod3, mod3, w1, w2)


def _layer(x3, mod3, lw, bias, sample, state):
    nb, r, d = x3.shape
    widths = lw["widths"]
    u, gb, gc, xc, k, v, qt, vt = _in_proj_call(x3, mod3, lw["wrm"], lw["wt"], lw["kg2"], widths, tm=256)

    ns = lw["abar"][0].shape[0]
    scan = _scan_consts(lw["abar"], min(SSM_TILE, r) // SUBLANES)
    if sample:
        cache_k, cache_v, h0r, h0i, conv_buf = state
        h0r = h0r.reshape(nb, 1, ns)
        h0i = h0i.reshape(nb, 1, ns)
    else:
        h0r = h0i = jnp.zeros((nb, 1, ns), F32)
    ys, hre, him = _ssm_call(u, h0r, h0i, lw["fb"], scan, lw["cb"], lw["dskip"], lw["wglu"], nb, r)

    kvw = k.shape[1]
    if sample:
        n_buf = cache_k.shape[1]
        kc = cache_k.reshape(nb * n_buf, kvw)
        vtc = cache_v.reshape(nb * n_buf, kvw).T
    else:
        kc, vtc = k, vt
    yt = _attn_call(qt, k, vt, kc, vtc, bias, lw["gq"], lw["sink"], sample, r)

    wc = gb.shape[1]
    if sample:
        halo = jnp.pad(conv_buf, ((0, 0), (SUBLANES - (CONV_K - 1), 0), (0, 0)))
        halo_gc = halo_xc = halo
    else:
        halo_gc, halo_xc = gc.reshape(nb, r, wc), xc.reshape(nb, r, wc)
    x3, ztail = _out_proj_call(x3, mod3, ys, yt, gb, gc, xc, halo_gc, halo_xc,
                               lw["cw"], lw["og"], lw["wo"], sample, tm=256)
    x3 = _ffn_call(x3, mod3, lw["w1"], lw["w2"], tm=1024, tf=512)

    keep = min(r, WINDOW)
    k4 = k.reshape(nb, r, kvw)[:, r - keep:].reshape(nb, keep, N_KV_HEADS, HEAD_DIM)
    v4 = v.reshape(nb, r, kvw)[:, r - keep:].reshape(nb, keep, N_KV_HEADS, HEAD_DIM)
    if sample:
        new_k = jnp.concatenate([cache_k, k4], axis=1)[:, -n_buf:]
        new_v = jnp.concatenate([cache_v, v4], axis=1)[:, -n_buf:]
    else:
        new_k, new_v = k4, v4
    g = ns // SSM_STATE
    return (x3, new_k, new_v, hre.reshape(nb, g, SSM_STATE), him.reshape(nb, g, SSM_STATE),
            ztail[:, -(CONV_K - 1):])


def _layer_weights(l, w_in, ssm_a_re, ssm_a_im, ssm_log_dt, ssm_b_re, ssm_b_im, ssm_c_re, ssm_c_im, ssm_d,
                   ssm_w_glu, q_norm_g, k_norm_g, attn_sinks, conv_w, out_norm_g, w_out, w_ff1, w_ff2):
    w_ssm = ssm_d.shape[1]
    n_heads = attn_sinks.shape[1]
    w_attn = n_heads * HEAD_DIM
    w_kv = N_KV_HEADS * HEAD_DIM
    w_conv = conv_w.shape[1]
    qpg = n_heads // N_KV_HEADS
    wi = w_in[l]
    o = np.cumsum([0, w_ssm, w_attn, w_kv, w_kv, w_conv, w_conv, w_conv])
    wu, wq, wk, wv, wgb, wgc, wxc = (wi[:, o[i]:o[i + 1]] for i in range(7))
    fb, abar, cb = _ssm_consts(ssm_a_re[l], ssm_a_im[l], ssm_log_dt[l], ssm_b_re[l], ssm_b_im[l],
                             ssm_c_re[l], ssm_c_im[l])
    return {
        "widths": (w_ssm, w_conv, w_kv, w_attn),
        "wrm": jnp.concatenate([wu, wgb, wgc, wxc, wk, wv], axis=1).astype(BF16),
        "wt": jnp.concatenate([wq, wv], axis=1).T.astype(BF16),
        "kg2": jnp.tile(k_norm_g[l].astype(F32), N_KV_HEADS)[None, :],
        "fb": fb, "abar": abar, "cb": cb,
        "dskip": ssm_d[l].astype(F32)[None, :],
        "wglu": ssm_w_glu[l].astype(BF16),
        "gq": jnp.broadcast_to((q_norm_g[l].astype(F32) * HEAD_DIM ** -0.5)[:, None], (HEAD_DIM, PAIR)),
        "sink": jnp.broadcast_to(attn_sinks[l].astype(F32).reshape(N_KV_HEADS, 1, qpg, 1),
                                 (N_KV_HEADS, 1, qpg, PAIR)).reshape(N_KV_HEADS, 1, qpg * PAIR),
        "cw": conv_w[l].astype(F32).T,
        "og": out_norm_g[l].astype(F32)[None, :],
        "wo": w_out[l].astype(BF16),
        "w1": w_ff1[l].astype(BF16),
        "w2": w_ff2[l].astype(BF16),
    }


def kernel(x_prompt, x_sample, cache_k, cache_v, state_ssm_re, state_ssm_im, state_conv, c_prompt, c_sample, rel_bias, w_ada, b_ada, w_in, ssm_a_re, ssm_a_im, ssm_log_dt, ssm_b_re, ssm_b_im, ssm_c_re, ssm_c_im, ssm_d, ssm_w_glu, q_norm_g, k_norm_g, attn_sinks, conv_w, out_norm_g, w_out, w_ff1, w_ff2):
    depth = w_in.shape[0]
    nbp, nbs = x_prompt.shape[0], x_sample.shape[0]
    assert x_sample.shape[1] == CHUNK and cache_k.shape[2] == WINDOW and nbs % 2 == 0

    nc = nbp + nbs
    ncp = -(-nc // SUBLANES) * SUBLANES
    c_all = jnp.pad(jnp.concatenate([c_prompt, c_sample], axis=0), ((0, ncp - nc), (0, 0)))
    mod = _mod_call(c_all, w_ada, b_ada)

    bias_p = _bias_call(rel_bias, sample=False)
    bias_s = _bias_call(rel_bias, sample=True)

    xp, xs = x_prompt, x_sample
    outs_p, outs_s = [], []
    for l in range(depth):
        lw = _layer_weights(l, w_in, ssm_a_re, ssm_a_im, ssm_log_dt, ssm_b_re, ssm_b_im, ssm_c_re, ssm_c_im,
                            ssm_d, ssm_w_glu, q_norm_g, k_norm_g, attn_sinks, conv_w, out_norm_g, w_out,
                            w_ff1, w_ff2)
        mod_p = mod[l, :nbp][:, None, :]
        mod_s = mod[l, nbp:nc][:, None, :]
        xp, *rest_p = _layer(xp, mod_p, lw, bias_p, False, None)
        xs, *rest_s = _layer(xs, mod_s, lw, bias_s, True,
                             (cache_k[l], cache_v[l], state_ssm_re[l], state_ssm_im[l], state_conv[l]))
        outs_p.append(rest_p)
        outs_s.append(rest_s)
    stack = lambda outs, i: jnp.stack([o[i] for o in outs])
    return (xp, xs,
            *(stack(outs_p, i) for i in range(5)),
            *(stack(outs_s, i) for i in range(5)))
```

```python
import functools
import math

import numpy as np
import jax
import jax.numpy as jnp
from jax import lax
from jax.experimental import pallas as pl
from jax.experimental.pallas import tpu as pltpu

F32 = jnp.float32
BF16 = jnp.bfloat16

CHUNK = 64
SSM_GROUP = 16
SSM_STATE = 64
HEAD_DIM = 64
N_KV_HEADS = 2
WINDOW = 128
CONV_K = 3
REL_BUCKETS = 32
REL_MAX_DIST = 64
EPS = 1e-6
NEG_INF = -1e30
LOG2E = math.log2(math.e)

LANES = 128
SUBLANES = 8
PAIR = 2 * CHUNK
VMEM_LIMIT_BYTES = 56 * 1024 * 1024


def _params(*sem):
    return pltpu.CompilerParams(dimension_semantics=sem, vmem_limit_bytes=VMEM_LIMIT_BYTES)


def _const_spec(shape):
    nd = len(shape)
    return pl.BlockSpec(shape, lambda *_: (0,) * nd, pipeline_mode=pl.Buffered(1))


def _tok_blocks(nb, r, tm):
    if r >= tm:
        assert r % tm == 0
        return 1, tm
    bb = min(tm // r, nb)
    assert nb % bb == 0
    return bb, r


def _mod_kernel(c_ref, w_ref, b_ref, o_ref):
    c = c_ref[...]
    s = (c * jax.nn.sigmoid(c)).astype(BF16)
    o_ref[...] = jnp.dot(s, w_ref[...].astype(BF16), preferred_element_type=F32) + b_ref[...]


def _mod_call(c_all, w_ada, b_ada):
    depth, d, n = w_ada.shape
    nc = c_all.shape[0]
    tn = 1024
    return pl.pallas_call(
        _mod_kernel,
        grid=(depth, n // tn),
        in_specs=[pl.BlockSpec((nc, d), lambda l, j: (0, 0)),
                  pl.BlockSpec((None, d, tn), lambda l, j: (l, 0, j)),
                  pl.BlockSpec((None, 1, tn), lambda l, j: (l, 0, j))],
        out_specs=pl.BlockSpec((None, nc, tn), lambda l, j: (l, 0, j)),
        out_shape=jax.ShapeDtypeStruct((depth, nc, n), F32),
        compiler_params=_params("arbitrary", "arbitrary"),
    )(c_all, w_ada, b_ada.reshape(depth, 1, n))


def _bucket_maps(sample):
    lane = np.arange(PAIR)[None, :]
    if sample:
        nk = 2 * WINDOW + PAIR
        row = np.arange(nk)[:, None]
        own = row >= 2 * WINDOW
        key_b = np.where(own, (row - 2 * WINDOW) // CHUNK, row // WINDOW)
        key_s = np.where(own, WINDOW + (row - 2 * WINDOW) % CHUNK, row % WINDOW)
        rel = key_s - WINDOW - lane % CHUNK
        visible = key_b == lane // CHUNK
    else:
        nk = WINDOW + PAIR
        row = np.arange(nk)[:, None]
        rel = row - WINDOW - lane
        dc = row // CHUNK - lane // CHUNK
        visible = (dc >= 0) & (dc <= WINDOW // CHUNK)
    half = REL_BUCKETS // 2
    exact = half // 2
    n = np.abs(rel)
    nf = np.maximum(n, 1).astype(np.float32)
    far = exact + (np.log(nf / np.float32(exact)) / np.float32(math.log(REL_MAX_DIST / exact))
                   * np.float32(half - exact)).astype(np.int32)
    far = np.minimum(far, half - 1)
    bucket = np.where(rel > 0, half, 0) + np.where(n < exact, n, far)
    full = np.where(visible, bucket, -1).astype(np.int32)
    first = np.where(row >= WINDOW, full, -1).astype(np.int32)
    return np.stack([full, first])


def _bias_kernel(table_ref, bucket_ref, o_ref):
    h = pl.program_id(1)
    bucket = bucket_ref[...]
    acc = jnp.full(bucket.shape, NEG_INF, F32)
    for b in range(REL_BUCKETS):
        acc = jnp.where(bucket == b, table_ref[b, h] * LOG2E, acc)
    o_ref[...] = acc


def _bias_call(rel_bias, sample):
    buckets = jnp.asarray(_bucket_maps(sample))
    _, nk, _ = buckets.shape
    n_heads = rel_bias.shape[1]
    qpg = n_heads // N_KV_HEADS
    return pl.pallas_call(
        _bias_kernel,
        grid=(2, n_heads),
        in_specs=[pl.BlockSpec(memory_space=pltpu.SMEM),
                  pl.BlockSpec((None, nk, PAIR), lambda s, h: (s, 0, 0))],
        out_specs=pl.BlockSpec((None, None, nk, PAIR), lambda s, h: (s, h // qpg, 0, h % qpg)),
        out_shape=jax.ShapeDtypeStruct((2, N_KV_HEADS, nk, qpg * PAIR), F32),
        compiler_params=_params("arbitrary", "arbitrary"),
    )(rel_bias, buckets)


def _rms(y):
    return y * lax.rsqrt(jnp.mean(y * y, axis=-1, keepdims=True) + EPS)


def _in_proj_kernel(x_ref, shift_ref, scale_ref, zinit_ref, wrm_ref, wt_ref, kg_ref, cw_ref, ogc_ref,
                    u_ref, yc_ref, zt_ref, k_ref, v_ref, qt_ref, vt_ref, zprev, *, widths):
    bb, tr, d = x_ref.shape
    x = x_ref[...]
    ms = jnp.mean(x * x, axis=-1, keepdims=True)
    h = x * lax.rsqrt(ms + EPS) * (1.0 + scale_ref[...]) + shift_ref[...]
    h = h.reshape(bb * tr, d).astype(BF16)

    w_ssm, w_conv, w_kv, w_attn = widths
    pc = jnp.dot(h, wrm_ref[:, 0:3 * w_conv], preferred_element_type=F32)
    gb = pc[:, 0:w_conv].reshape(bb, tr, w_conv)
    z = (pc[:, w_conv:2 * w_conv] * pc[:, 2 * w_conv:3 * w_conv]).reshape(bb, tr, w_conv)
    p = jnp.dot(h, wrm_ref[:, 3 * w_conv:], preferred_element_type=F32)
    u_ref[...] = p[:, 0:w_ssm]
    k = p[:, w_ssm:w_ssm + w_kv]
    v_ref[...] = p[:, w_ssm + w_kv:]

    @pl.when(pl.program_id(1) == 0)
    def _():
        zprev[...] = zinit_ref[...]

    zp = jnp.concatenate([zprev[...], z], axis=1)
    cw = cw_ref[...]
    conv = (zp[:, SUBLANES - 2:SUBLANES - 2 + tr] * cw[0:1] + zp[:, SUBLANES - 1:SUBLANES - 1 + tr] * cw[1:2]
            + z * cw[2:3])
    yc_ref[...] = (_rms(gb * conv) * ogc_ref[...]).reshape(bb * tr, w_conv).astype(yc_ref.dtype)
    ztail = z[:, tr - SUBLANES:, :]
    zprev[...] = ztail
    zt_ref[...] = ztail

    lo = lax.broadcasted_iota(jnp.int32, k.shape, 1) < HEAD_DIM
    k2 = k * k
    s_lo = jnp.sum(jnp.where(lo, k2, 0.0), axis=-1, keepdims=True)
    s_hi = jnp.sum(jnp.where(lo, 0.0, k2), axis=-1, keepdims=True)
    ssq = jnp.where(lo, s_lo, s_hi)
    k_ref[...] = k * lax.rsqrt(ssq * (1.0 / HEAD_DIM) + EPS) * kg_ref[...]

    pt = lax.dot_general(wt_ref[...], h, (((1,), (1,)), ((), ())), preferred_element_type=F32)
    qt_ref[...] = pt[:w_attn]
    vt_ref[...] = pt[w_attn:]


def _in_proj_call(x3, mod3, zinit, wrm, wt, kg2, cw, ogc, widths, tm):
    nb, r, d = x3.shape
    w_ssm, w_conv, w_kv, w_attn = widths
    bb, tr = _tok_blocks(nb, r, tm)
    tmm = bb * tr
    nt = r // tr
    ttot = nb * r
    tok = lambda b, t: (b * nt + t, 0)
    tok_t = lambda b, t: (0, b * nt + t)
    halo_spec = pl.BlockSpec((bb, SUBLANES, w_conv), lambda b, t: (b, 0, 0))
    out_shape = [jax.ShapeDtypeStruct((ttot, w_ssm), F32), jax.ShapeDtypeStruct((ttot, w_conv), BF16),
                 jax.ShapeDtypeStruct((nb, SUBLANES, w_conv), F32),
                 jax.ShapeDtypeStruct((ttot, w_kv), F32), jax.ShapeDtypeStruct((ttot, w_kv), F32),
                 jax.ShapeDtypeStruct((w_attn, ttot), F32), jax.ShapeDtypeStruct((w_kv, ttot), F32)]
    out_specs = [pl.BlockSpec((tmm, w_ssm), tok), pl.BlockSpec((tmm, w_conv), tok), halo_spec,
                 pl.BlockSpec((tmm, w_kv), tok), pl.BlockSpec((tmm, w_kv), tok),
                 pl.BlockSpec((w_attn, tmm), tok_t), pl.BlockSpec((w_kv, tmm), tok_t)]
    consts = (wrm, wt, kg2, cw, ogc)
    return pl.pallas_call(
        functools.partial(_in_proj_kernel, widths=widths),
        grid=(nb // bb, nt),
        in_specs=[pl.BlockSpec((bb, tr, d), lambda b, t: (b, t, 0)),
                  pl.BlockSpec((bb, 1, d), lambda b, t: (b, 0, 0)),
                  pl.BlockSpec((bb, 1, d), lambda b, t: (b, 0, 1)),
                  halo_spec] + [_const_spec(c.shape) for c in consts],
        out_specs=out_specs,
        out_shape=out_shape,
        scratch_shapes=[pltpu.VMEM((bb, SUBLANES, w_conv), F32)],
        compiler_params=_params("arbitrary", "arbitrary"),
    )(x3, mod3, mod3, zinit, *consts)


SCAN_LANE_BLOCKS = 8


def _gelu_tanh(x):
    return 0.5 * x * (1.0 + jnp.tanh(math.sqrt(2.0 / math.pi) * (x + 0.044715 * (x * x * x))))


def _segment_perm(ts):
    lseg = ts // SUBLANES
    p = np.zeros((ts, ts), np.float32)
    i, j = np.meshgrid(np.arange(lseg), np.arange(SUBLANES), indexing="ij")
    p[(i * SUBLANES + j).ravel(), (j * lseg + i).ravel()] = 1.0
    return p


def _ssm_kernel(u_ref, h0r_ref, h0i_ref, perm_ref, permt_ref, fb_ref, ab_ref, sg_ref, pw_ref, cb_ref, dskip_ref,
                wglu_ref, og_ref, y_ref, hre_ref, him_ref, bre, bim, car):
    ts, w = u_ref.shape
    ns = bre.shape[1]
    lseg = ts // SUBLANES
    half_u = w // 2
    half_s = ns // 2
    t = pl.program_id(1)

    @pl.when(t == 0)
    def _():
        car[0:1, :] = h0r_ref[...]
        car[1:2, :] = h0i_ref[...]

    u = u_ref[...]
    up = jnp.dot(perm_ref[...], u.astype(BF16), preferred_element_type=F32).astype(BF16)
    for j in range(2):
        bb = jnp.dot(up[:, j * half_u:(j + 1) * half_u], fb_ref[j], preferred_element_type=F32)
        bre[:, j * half_s:(j + 1) * half_s] = bb[:, :half_s]
        bim[:, j * half_s:(j + 1) * half_s] = bb[:, half_s:]

    row0 = lax.broadcasted_iota(jnp.int32, (SUBLANES, LANES), 0) == 0
    zero = jnp.zeros((SUBLANES, LANES), F32)
    nblk = SCAN_LANE_BLOCKS
    for c0 in range(0, ns // LANES, nblk):
        sls = [pl.ds((c0 + i) * LANES, LANES) for i in range(nblk)]
        ab = [(ab_ref[0, :, sl], ab_ref[1, :, sl]) for sl in sls]

        def pass1(i, carry, sls=sls, ab=ab):
            rows = pl.ds(pl.multiple_of(i * SUBLANES, SUBLANES), SUBLANES)
            new = []
            for sl, (ar, ai), (hr, hi) in zip(sls, ab, carry):
                hr, hi = ar * hr - ai * hi + bre[rows, sl], ar * hi + ai * hr + bim[rows, sl]
                bre[rows, sl] = hr
                bim[rows, sl] = hi
                new.append((hr, hi))
            return tuple(new)

        ends = lax.fori_loop(0, lseg, pass1, tuple((zero, zero) for _ in sls), unroll=2)

        starts = []
        for sl, (er, ei) in zip(sls, ends):
            xr = jnp.where(row0, car[0:1, sl], pltpu.roll(er, 1, axis=0))
            xi = jnp.where(row0, car[1:2, sl], pltpu.roll(ei, 1, axis=0))
            for k in range(3):
                sr = pltpu.roll(xr, 1 << k, axis=0)
                si = pltpu.roll(xi, 1 << k, axis=0)
                mr, mi = sg_ref[2 * k, :, sl], sg_ref[2 * k + 1, :, sl]
                xr, xi = xr + (mr * sr - mi * si), xi + (mr * si + mi * sr)
            lr, li = sg_ref[6, :, sl], sg_ref[7, :, sl]
            nr = lr * xr - li * xi + er
            ni = lr * xi + li * xr + ei
            car[0:1, sl] = nr[SUBLANES - 1:SUBLANES, :]
            car[1:2, sl] = ni[SUBLANES - 1:SUBLANES, :]
            starts.append((xr, xi))

        def pass2(i, c, sls=sls, starts=starts):
            rows = pl.ds(pl.multiple_of(i * SUBLANES, SUBLANES), SUBLANES)
            for sl, (sr, si) in zip(sls, starts):
                pr = pw_ref[0, rows, sl]
                pi = pw_ref[1, rows, sl]
                bre[rows, sl] += pr * sr - pi * si
                bim[rows, sl] += pr * si + pi * sr
            return c

        lax.fori_loop(0, lseg, pass2, 0, unroll=2)

    hre_ref[...] = car[0:1, :]
    him_ref[...] = car[1:2, :]

    ys = []
    for j in range(2):
        hcat = jnp.concatenate([bre[:, j * half_s:(j + 1) * half_s].astype(BF16),
                                bim[:, j * half_s:(j + 1) * half_s].astype(BF16)], axis=-1)
        ys.append(jnp.dot(hcat, cb_ref[j], preferred_element_type=F32))
    yp = jnp.concatenate(ys, axis=-1)
    p0 = yp.astype(BF16)
    r1 = yp - p0.astype(F32)
    p1 = r1.astype(BF16)
    p2 = (r1 - p1.astype(F32)).astype(BF16)
    parts = jnp.dot(permt_ref[...], jnp.concatenate([p0, p1, p2], axis=-1), preferred_element_type=F32)
    y = (parts[:, :w] + parts[:, w:2 * w]) + parts[:, 2 * w:] + dskip_ref[...] * u
    y = _gelu_tanh(y)
    y = y * jax.nn.sigmoid(jnp.dot(y.astype(BF16), wglu_ref[...], preferred_element_type=F32))
    y_ref[...] = (_rms(y) * og_ref[...]).astype(y_ref.dtype)


SSM_TILE = 256


def _ssm_call(u, h0r, h0i, fb, scan, cb, dskip, wglu, ogs, nb, r):
    ttot, w = u.shape
    ab, sg, pw = scan
    ns = ab.shape[-1]
    ts = min(SSM_TILE, r)
    assert pw.shape[1] == ts
    nt = r // ts
    perm = _segment_perm(ts)
    perm, permt = jnp.asarray(perm, BF16), jnp.asarray(perm.T, BF16)
    st_spec = pl.BlockSpec((None, 1, ns), lambda b, t: (b, 0, 0))
    consts = (perm, permt, fb, ab, sg, pw, cb, dskip, wglu, ogs)
    return pl.pallas_call(
        _ssm_kernel,
        grid=(nb, nt),
        in_specs=[pl.BlockSpec((ts, w), lambda b, t: (b * nt + t, 0)), st_spec, st_spec]
        + [_const_spec(c.shape) for c in consts],
        out_specs=[pl.BlockSpec((ts, w), lambda b, t: (b * nt + t, 0)), st_spec, st_spec],
        out_shape=[jax.ShapeDtypeStruct((ttot, w), BF16),
                   jax.ShapeDtypeStruct((nb, 1, ns), F32), jax.ShapeDtypeStruct((nb, 1, ns), F32)],
        scratch_shapes=[pltpu.VMEM((ts, ns), F32), pltpu.VMEM((ts, ns), F32), pltpu.VMEM((SUBLANES, ns), F32)],
        compiler_params=_params("arbitrary", "arbitrary"),
    )(u, h0r, h0i, *consts)


def _scan_consts(a1, lseg):
    def cmul(x, y):
        return x[0] * y[0] - x[1] * y[1], x[0] * y[1] + x[1] * y[0]

    tr, ti = a1[0][None], a1[1][None]
    while tr.shape[0] < lseg:
        nr, ni = cmul((tr, ti), (tr[-1], ti[-1]))
        tr, ti = jnp.concatenate([tr, nr]), jnp.concatenate([ti, ni])
    pw = jnp.repeat(jnp.stack([tr[:lseg], ti[:lseg]]), SUBLANES, axis=1)
    ns = a1[0].shape[0]
    ab = jnp.stack([jnp.broadcast_to(a1[0], (SUBLANES, ns)), jnp.broadcast_to(a1[1], (SUBLANES, ns))])
    al = (tr[lseg - 1], ti[lseg - 1])
    al2 = cmul(al, al)
    al4 = cmul(al2, al2)
    row = jnp.arange(SUBLANES)[:, None]
    rows = []
    for k, a in enumerate((al, al2, al4)):
        keep = row >= (1 << k)
        rows += [jnp.where(keep, a[0][None, :], 0.0), jnp.where(keep, a[1][None, :], 0.0)]
    rows += [jnp.broadcast_to(al[0], (SUBLANES, ns)), jnp.broadcast_to(al[1], (SUBLANES, ns))]
    return ab, jnp.stack(rows), pw


def _ssm_consts(a_re, a_im, log_dt, b_re, b_im, c_re, c_im):
    g, p = a_re.shape
    hh = b_re.shape[-1]
    ar, ai = a_re.astype(F32), a_im.astype(F32)
    dt = jnp.exp(log_dt.astype(F32))[:, None]
    mag = jnp.exp(dt * ar)
    abar_re, abar_im = mag * jnp.cos(dt * ai), mag * jnp.sin(dt * ai)
    den = ar * ar + ai * ai
    f_re = ((abar_re - 1.0) * ar + abar_im * ai) / den
    f_im = (abar_im * ar - (abar_re - 1.0) * ai) / den
    fb_re = f_re[..., None] * b_re - f_im[..., None] * b_im
    fb_im = f_re[..., None] * b_im + f_im[..., None] * b_re
    gh = g // 2
    eye = jnp.eye(gh, dtype=F32)

    def in_blk(m):
        return jnp.einsum('gph,gk->ghkp', m, eye).reshape(gh * hh, gh * p)

    def out_blk(m):
        return jnp.einsum('ghp,gk->kpgh', m, eye).reshape(gh * p, gh * hh)

    fb = jnp.stack([jnp.concatenate([in_blk(fb_re[j * gh:(j + 1) * gh]), in_blk(fb_im[j * gh:(j + 1) * gh])], axis=1)
                    for j in range(2)]).astype(BF16)
    cb = jnp.stack([jnp.concatenate([out_blk(c_re[j * gh:(j + 1) * gh]), -out_blk(c_im[j * gh:(j + 1) * gh])], axis=0)
                    for j in range(2)]).astype(BF16)

    return fb, (abar_re.reshape(-1), abar_im.reshape(-1)), cb


def _attn_kernel(qt_ref, k_ref, vt_ref, kc_ref, vtc_ref, bias_ref, gq_ref, sink_ref, og_ref, yt_ref,
                 *, sample, pairs_per_seq):
    n_rows, tq = qt_ref.shape
    npairs = tq // PAIR
    qpg = n_rows // (N_KV_HEADS * HEAD_DIM)
    i = pl.program_id(0)
    gq = gq_ref[...]
    if sample:
        k_all, vt_all = k_ref[...], vt_ref[...]
    else:
        k_all = jnp.concatenate([kc_ref[...], k_ref[...]], axis=0)
        vt_all = jnp.concatenate([vtc_ref[...], vt_ref[...]], axis=1)
    for p in range(npairs):
        if sample:
            kb = jnp.concatenate([kc_ref[p * 2 * WINDOW:(p + 1) * 2 * WINDOW, :],
                                  k_all[p * PAIR:(p + 1) * PAIR, :]], axis=0)
            vtb = jnp.concatenate([vtc_ref[:, p * 2 * WINDOW:(p + 1) * 2 * WINDOW],
                                   vt_all[:, p * PAIR:(p + 1) * PAIR]], axis=1)
            sel = 0
        else:
            kb = k_all[p * PAIR:p * PAIR + WINDOW + PAIR, :]
            vtb = vt_all[:, p * PAIR:p * PAIR + WINDOW + PAIR]
            sel = ((i * npairs + p) % pairs_per_seq == 0).astype(jnp.int32)
        kb = kb.astype(BF16)
        vtb = vtb.astype(BF16)
        ones = jnp.ones((2 * SUBLANES, vtb.shape[1]), BF16)
        outs = []
        ssq = jnp.zeros((1, PAIR), F32)
        for g in range(N_KV_HEADS):
            qs = []
            for hh in range(qpg):
                r0 = (g * qpg + hh) * HEAD_DIM
                q = qt_ref[r0:r0 + HEAD_DIM, p * PAIR:(p + 1) * PAIR]
                ms = jnp.sum(q * q, axis=0, keepdims=True) * (1.0 / HEAD_DIM)
                qs.append((q * lax.rsqrt(ms + EPS) * gq).astype(BF16))
            qg = jnp.concatenate(qs, axis=1)
            zero = jnp.zeros_like(qg)
            qpad = jnp.concatenate([qg, zero] if g == 0 else [zero, qg], axis=0)
            s = jnp.dot(kb, qpad, preferred_element_type=F32) + bias_ref[sel, g]
            sink = sink_ref[g]
            m = jnp.maximum(jnp.max(s, axis=0, keepdims=True), sink)
            e = jnp.exp2(s - m).astype(BF16)
            va = jnp.concatenate([vtb[g * HEAD_DIM:(g + 1) * HEAD_DIM, :], ones], axis=0)
            oa = jnp.dot(va, e, preferred_element_type=F32)
            den = oa[HEAD_DIM:HEAD_DIM + 1, :] + jnp.exp2(sink - m)
            o = oa[:HEAD_DIM, :] / den
            outs.append(o)
            sq = jnp.sum(o * o, axis=0, keepdims=True)
            for hh in range(qpg):
                ssq = ssq + sq[:, hh * PAIR:(hh + 1) * PAIR]
        rn = lax.rsqrt(ssq * (1.0 / n_rows) + EPS)
        rn = jnp.concatenate([rn] * qpg, axis=1)
        for g in range(N_KV_HEADS):
            o = outs[g] * rn * og_ref[g]
            for hh in range(qpg):
                r0 = (g * qpg + hh) * HEAD_DIM
                yt_ref[r0:r0 + HEAD_DIM, p * PAIR:(p + 1) * PAIR] = \
                    o[:, hh * PAIR:(hh + 1) * PAIR].astype(yt_ref.dtype)


def _attn_call(qt, k, vt, kc, vtc, bias, gq, sink, oga, sample, seq_len):
    n_rows, ttot = qt.shape
    kvw = k.shape[1]
    tq = min(512, ttot)
    assert ttot % tq == 0 and (sample or seq_len % tq == 0)
    npairs = tq // PAIR
    if sample:
        ctx = 2 * WINDOW * npairs
        kc_spec = pl.BlockSpec((ctx, kvw), lambda i: (i, 0))
        vtc_spec = pl.BlockSpec((kvw, ctx), lambda i: (0, i))
    else:
        kc_spec = pl.BlockSpec((WINDOW, kvw), lambda i: (jnp.maximum(i * npairs - 1, 0), 0))
        vtc_spec = pl.BlockSpec((kvw, WINDOW), lambda i: (0, jnp.maximum(i * npairs - 1, 0)))
    return pl.pallas_call(
        functools.partial(_attn_kernel, sample=sample, pairs_per_seq=seq_len // PAIR),
        grid=(ttot // tq,),
        in_specs=[pl.BlockSpec((n_rows, tq), lambda i: (0, i)),
                  pl.BlockSpec((tq, kvw), lambda i: (i, 0)),
                  pl.BlockSpec((kvw, tq), lambda i: (0, i)),
                  kc_spec, vtc_spec,
                  _const_spec(bias.shape), _const_spec(gq.shape), _const_spec(sink.shape),
                  _const_spec(oga.shape)],
        out_specs=pl.BlockSpec((n_rows, tq), lambda i: (0, i)),
        out_shape=jax.ShapeDtypeStruct((n_rows, ttot), BF16),
        compiler_params=_params("arbitrary"),
    )(qt, k, vt, kc, vtc, bias, gq, sink, oga)


def _out_proj_kernel(x_ref, gate_ref, ys_ref, yt_ref, yc_ref, wo_ref, xo_ref):
    bb, tr, d = x_ref.shape
    ws, wa = ys_ref.shape[1], yt_ref.shape[0]
    o = jnp.dot(ys_ref[...], wo_ref[0:ws, :], preferred_element_type=F32)
    o += lax.dot_general(yt_ref[...], wo_ref[ws:ws + wa, :], (((0,), (0,)), ((), ())),
                         preferred_element_type=F32)
    o += jnp.dot(yc_ref[...], wo_ref[ws + wa:, :], preferred_element_type=F32)
    xo_ref[...] = x_ref[...] + gate_ref[...] * o.reshape(bb, tr, d)


def _out_proj_call(x3, mod3, ys, yt, yc, wo, tm):
    nb, r, d = x3.shape
    ws, wa, wc = ys.shape[1], yt.shape[0], yc.shape[1]
    bb, tr = _tok_blocks(nb, r, tm)
    tmm = bb * tr
    nt = r // tr
    tok3 = lambda b, t: (b, t, 0)
    return pl.pallas_call(
        _out_proj_kernel,
        grid=(nb // bb, nt),
        in_specs=[pl.BlockSpec((bb, tr, d), tok3),
                  pl.BlockSpec((bb, 1, d), lambda b, t: (b, 0, 2)),
                  pl.BlockSpec((tmm, ws), lambda b, t: (b * nt + t, 0)),
                  pl.BlockSpec((wa, tmm), lambda b, t: (0, b * nt + t)),
                  pl.BlockSpec((tmm, wc), lambda b, t: (b * nt + t, 0)),
                  _const_spec(wo.shape)],
        out_specs=pl.BlockSpec((bb, tr, d), tok3),
        out_shape=jax.ShapeDtypeStruct((nb, r, d), F32),
        compiler_params=_params("arbitrary", "arbitrary"),
    )(x3, mod3, ys, yt, yc, wo)


FFN_TILE = 512


def _ffn_kernel(x_ref, shift_ref, scale_ref, gate_ref, w1_ref, w2_ref, o_ref, h_s):
    bb, tr, d = x_ref.shape
    j = pl.program_id(2)

    @pl.when(j == 0)
    def _():
        x = x_ref[...]
        h = x * lax.rsqrt(jnp.mean(x * x, axis=-1, keepdims=True) + EPS) * (1.0 + scale_ref[...]) + shift_ref[...]
        h_s[...] = h.reshape(bb * tr, d).astype(BF16)
        o_ref[...] = jnp.zeros_like(o_ref)

    a = jnp.maximum(jnp.dot(h_s[...], w1_ref[...], preferred_element_type=F32), 0.0)
    o_ref[...] += jnp.dot((a * a).astype(BF16), w2_ref[...], preferred_element_type=F32).reshape(bb, tr, d)

    @pl.when(j == pl.num_programs(2) - 1)
    def _():
        o_ref[...] = x_ref[...] + gate_ref[...] * o_ref[...]


def _ffn_call(x3, mod3, w1, w2, tm):
    nb, r, d = x3.shape
    nj, _, tf = w1.shape
    bb, tr = _tok_blocks(nb, r, tm)
    tok3 = lambda b, t, j: (b, t, 0)
    mod_spec = lambda c: pl.BlockSpec((bb, 1, d), lambda b, t, j: (b, 0, c))
    return pl.pallas_call(
        _ffn_kernel,
        grid=(nb // bb, r // tr, nj),
        in_specs=[pl.BlockSpec((bb, tr, d), tok3), mod_spec(3), mod_spec(4), mod_spec(5),
                  pl.BlockSpec((None, d, tf), lambda b, t, j: (j, 0, 0)),
                  pl.BlockSpec((tf, d), lambda b, t, j: (j, 0))],
        out_specs=pl.BlockSpec((bb, tr, d), tok3),
        out_shape=jax.ShapeDtypeStruct((nb, r, d), F32),
        scratch_shapes=[pltpu.VMEM((bb * tr, d), BF16)],
        compiler_params=_params("arbitrary", "arbitrary", "arbitrary"),
    )(x3, mod3, mod3, mod3, w1, w2)


def _layer(x3, mod3, lw, bias, sample, state):
    nb, r, d = x3.shape
    widths = lw["widths"]
    w_conv = widths[1]
    ns = lw["abar"][0].shape[0]
    scan = _scan_consts(lw["abar"], min(SSM_TILE, r) // SUBLANES)
    if sample:
        cache_k, cache_v, h0r, h0i, conv_buf = state
        h0r = h0r.reshape(nb, 1, ns)
        h0i = h0i.reshape(nb, 1, ns)
        zinit = jnp.pad(conv_buf, ((0, 0), (SUBLANES - (CONV_K - 1), 0), (0, 0)))
    else:
        h0r = h0i = jnp.zeros((nb, 1, ns), F32)
        zinit = jnp.zeros((nb, SUBLANES, w_conv), F32)
    u, yc, ztail, k, v, qt, vt = _in_proj_call(x3, mod3, zinit, lw["wrm"], lw["wt"], lw["kg2"], lw["cw"],
                                               lw["ogc"], widths, tm=512)
    ys, hre, him = _ssm_call(u, h0r, h0i, lw["fb"], scan, lw["cb"], lw["dskip"], lw["wglu"], lw["ogs"], nb, r)

    kvw = k.shape[1]
    if sample:
        n_buf = cache_k.shape[1]
        kc = cache_k.reshape(nb * n_buf, kvw)
        vtc = cache_v.reshape(nb * n_buf, kvw).T
    else:
        kc, vtc = k, vt
    yt = _attn_call(qt, k, vt, kc, vtc, bias, lw["gq"], lw["sink"], lw["oga"], sample, r)

    x3 = _out_proj_call(x3, mod3, ys, yt, yc, lw["wo"], tm=512)
    x3 = _ffn_call(x3, mod3, lw["w1"], lw["w2"], tm=1024)

    keep = min(r, WINDOW)
    k4 = k.reshape(nb, r, kvw)[:, r - keep:].reshape(nb, keep, N_KV_HEADS, HEAD_DIM)
    v4 = v.reshape(nb, r, kvw)[:, r - keep:].reshape(nb, keep, N_KV_HEADS, HEAD_DIM)
    if sample:
        new_k = jnp.concatenate([cache_k, k4], axis=1)[:, -n_buf:]
        new_v = jnp.concatenate([cache_v, v4], axis=1)[:, -n_buf:]
    else:
        new_k, new_v = k4, v4
    g = ns // SSM_STATE
    return (x3, new_k, new_v, hre.reshape(nb, g, SSM_STATE), him.reshape(nb, g, SSM_STATE),
            ztail[:, -(CONV_K - 1):])


def _layer_weights(l, w_in, ssm_a_re, ssm_a_im, ssm_log_dt, ssm_b_re, ssm_b_im, ssm_c_re, ssm_c_im, ssm_d,
                   ssm_w_glu, q_norm_g, k_norm_g, attn_sinks, conv_w, out_norm_g, w_out, w_ff1, w_ff2):
    w_ssm = ssm_d.shape[1]
    n_heads = attn_sinks.shape[1]
    w_attn = n_heads * HEAD_DIM
    w_kv = N_KV_HEADS * HEAD_DIM
    w_conv = conv_w.shape[1]
    qpg = n_heads // N_KV_HEADS
    wi = w_in[l]
    o = np.cumsum([0, w_ssm, w_attn, w_kv, w_kv, w_conv, w_conv, w_conv])
    wu, wq, wk, wv, wgb, wgc, wxc = (wi[:, o[i]:o[i + 1]] for i in range(7))
    fb, abar, cb = _ssm_consts(ssm_a_re[l], ssm_a_im[l], ssm_log_dt[l], ssm_b_re[l], ssm_b_im[l],
                               ssm_c_re[l], ssm_c_im[l])
    og = out_norm_g[l].astype(F32)
    d_model, d_ff = w_ff1.shape[1:]
    return {
        "widths": (w_ssm, w_conv, w_kv, w_attn),
        "wrm": jnp.concatenate([wgb, wgc, wxc, wu, wk, wv], axis=1).astype(BF16),
        "wt": jnp.concatenate([wq, wv], axis=1).T.astype(BF16),
        "kg2": jnp.tile(k_norm_g[l].astype(F32), N_KV_HEADS)[None, :],
        "fb": fb, "abar": abar, "cb": cb,
        "dskip": ssm_d[l].astype(F32)[None, :],
        "wglu": ssm_w_glu[l].astype(BF16),
        "gq": jnp.broadcast_to((q_norm_g[l].astype(F32) * (HEAD_DIM ** -0.5 * LOG2E))[:, None], (HEAD_DIM, PAIR)),
        "sink": jnp.broadcast_to((attn_sinks[l].astype(F32) * LOG2E).reshape(N_KV_HEADS, 1, qpg, 1),
                                 (N_KV_HEADS, 1, qpg, PAIR)).reshape(N_KV_HEADS, 1, qpg * PAIR),
        "cw": conv_w[l].astype(F32).T,
        "ogs": og[None, :w_ssm],
        "oga": jnp.broadcast_to(og[w_ssm:w_ssm + w_attn].reshape(N_KV_HEADS, qpg, HEAD_DIM, 1).transpose(0, 2, 1, 3),
                                (N_KV_HEADS, HEAD_DIM, qpg, PAIR)).reshape(N_KV_HEADS, HEAD_DIM, qpg * PAIR),
        "ogc": og[None, w_ssm + w_attn:],
        "wo": w_out[l].astype(BF16),
        "w1": w_ff1[l].astype(BF16).reshape(d_model, d_ff // FFN_TILE, FFN_TILE).transpose(1, 0, 2),
        "w2": w_ff2[l].astype(BF16),
    }


def kernel(x_prompt, x_sample, cache_k, cache_v, state_ssm_re, state_ssm_im, state_conv, c_prompt, c_sample, rel_bias, w_ada, b_ada, w_in, ssm_a_re, ssm_a_im, ssm_log_dt, ssm_b_re, ssm_b_im, ssm_c_re, ssm_c_im, ssm_d, ssm_w_glu, q_norm_g, k_norm_g, attn_sinks, conv_w, out_norm_g, w_out, w_ff1, w_ff2):
    depth = w_in.shape[0]
    nbp, nbs = x_prompt.shape[0], x_sample.shape[0]
    assert x_sample.shape[1] == CHUNK and cache_k.shape[2] == WINDOW and nbs % 2 == 0

    nc = nbp + nbs
    ncp = -(-nc // SUBLANES) * SUBLANES
    c_all = jnp.pad(jnp.concatenate([c_prompt, c_sample], axis=0), ((0, ncp - nc), (0, 0)))
    mod = _mod_call(c_all, w_ada, b_ada)

    bias_p = _bias_call(rel_bias, sample=False)
    bias_s = _bias_call(rel_bias, sample=True)

    xp, xs = x_prompt, x_sample
    outs_p, outs_s = [], []
    for l in range(depth):
        lw = _layer_weights(l, w_in, ssm_a_re, ssm_a_im, ssm_log_dt, ssm_b_re, ssm_b_im, ssm_c_re, ssm_c_im,
                            ssm_d, ssm_w_glu, q_norm_g, k_norm_g, attn_sinks, conv_w, out_norm_g, w_out,
                            w_ff1, w_ff2)
        mod_p = mod[l, :nbp][:, None, :]
        mod_s = mod[l, nbp:nc][:, None, :]
        xp, *rest_p = _layer(xp, mod_p, lw, bias_p, False, None)
        xs, *rest_s = _layer(xs, mod_s, lw, bias_s, True,
                             (cache_k[l], cache_v[l], state_ssm_re[l], state_ssm_im[l], state_conv[l]))
        outs_p.append(rest_p)
        outs_s.append(rest_s)
    stack = lambda outs, i: jnp.stack([o[i] for o in outs])
    return (xp, xs,
            *(stack(outs_p, i) for i in range(5)),
            *(stack(outs_s, i) for i in range(5)))
```

```python
import functools
import math

import numpy as np
import jax
import jax.numpy as jnp
from jax import lax
from jax.experimental import pallas as pl
from jax.experimental.pallas import tpu as pltpu

F32 = jnp.float32
BF16 = jnp.bfloat16

CHUNK = 64
SSM_GROUP = 16
SSM_STATE = 64
HEAD_DIM = 64
N_KV_HEADS = 2
WINDOW = 128
CONV_K = 3
REL_BUCKETS = 32
REL_MAX_DIST = 64
EPS = 1e-6
NEG_INF = -1e30
LOG2E = math.log2(math.e)

LANES = 128
SUBLANES = 8
PAIR = 2 * CHUNK
VMEM_LIMIT_BYTES = 56 * 1024 * 1024


def _params(*sem):
    return pltpu.CompilerParams(dimension_semantics=sem, vmem_limit_bytes=VMEM_LIMIT_BYTES)


def _const_spec(shape):
    nd = len(shape)
    return pl.BlockSpec(shape, lambda *_: (0,) * nd, pipeline_mode=pl.Buffered(1))


def _tok_blocks(nb, r, tm):
    if r >= tm:
        assert r % tm == 0
        return 1, tm
    bb = min(tm // r, nb)
    assert nb % bb == 0
    return bb, r


def _mod_kernel(c_ref, w_ref, b_ref, o_ref):
    c = c_ref[...]
    s = (c * jax.nn.sigmoid(c)).astype(BF16)
    o_ref[...] = jnp.dot(s, w_ref[...].astype(BF16), preferred_element_type=F32) + b_ref[...]


def _mod_call(c_all, w_ada, b_ada):
    depth, d, n = w_ada.shape
    nc = c_all.shape[0]
    tn = 1024
    return pl.pallas_call(
        _mod_kernel,
        grid=(depth, n // tn),
        in_specs=[pl.BlockSpec((nc, d), lambda l, j: (0, 0)),
                  pl.BlockSpec((None, d, tn), lambda l, j: (l, 0, j)),
                  pl.BlockSpec((None, 1, tn), lambda l, j: (l, 0, j))],
        out_specs=pl.BlockSpec((None, nc, tn), lambda l, j: (l, 0, j)),
        out_shape=jax.ShapeDtypeStruct((depth, nc, n), F32),
        compiler_params=_params("arbitrary", "arbitrary"),
    )(c_all, w_ada, b_ada.reshape(depth, 1, n))


def _bucket_maps(sample):
    lane = np.arange(PAIR)[None, :]
    if sample:
        nk = 2 * WINDOW + PAIR
        row = np.arange(nk)[:, None]
        own = row >= 2 * WINDOW
        key_b = np.where(own, (row - 2 * WINDOW) // CHUNK, row // WINDOW)
        key_s = np.where(own, WINDOW + (row - 2 * WINDOW) % CHUNK, row % WINDOW)
        rel = key_s - WINDOW - lane % CHUNK
        visible = key_b == lane // CHUNK
    else:
        nk = WINDOW + PAIR
        row = np.arange(nk)[:, None]
        rel = row - WINDOW - lane
        dc = row // CHUNK - lane // CHUNK
        visible = (dc >= 0) & (dc <= WINDOW // CHUNK)
    half = REL_BUCKETS // 2
    exact = half // 2
    n = np.abs(rel)
    nf = np.maximum(n, 1).astype(np.float32)
    far = exact + (np.log(nf / np.float32(exact)) / np.float32(math.log(REL_MAX_DIST / exact))
                   * np.float32(half - exact)).astype(np.int32)
    far = np.minimum(far, half - 1)
    bucket = np.where(rel > 0, half, 0) + np.where(n < exact, n, far)
    full = np.where(visible, bucket, -1).astype(np.int32)
    first = np.where(row >= WINDOW, full, -1).astype(np.int32)
    return np.stack([full, first])


def _bias_kernel(table_ref, bucket_ref, o_ref):
    h = pl.program_id(1)
    bucket = bucket_ref[...]
    acc = jnp.full(bucket.shape, NEG_INF, F32)
    for b in range(REL_BUCKETS):
        acc = jnp.where(bucket == b, table_ref[b, h] * LOG2E, acc)
    o_ref[...] = acc


def _bias_call(rel_bias, sample):
    buckets = jnp.asarray(_bucket_maps(sample))
    _, nk, _ = buckets.shape
    n_heads = rel_bias.shape[1]
    qpg = n_heads // N_KV_HEADS
    return pl.pallas_call(
        _bias_kernel,
        grid=(2, n_heads),
        in_specs=[pl.BlockSpec(memory_space=pltpu.SMEM),
                  pl.BlockSpec((None, nk, PAIR), lambda s, h: (s, 0, 0))],
        out_specs=pl.BlockSpec((None, None, nk, PAIR), lambda s, h: (s, h // qpg, 0, h % qpg)),
        out_shape=jax.ShapeDtypeStruct((2, N_KV_HEADS, nk, qpg * PAIR), F32),
        compiler_params=_params("arbitrary", "arbitrary"),
    )(rel_bias, buckets)


def _rms(y):
    return y * lax.rsqrt(jnp.mean(y * y, axis=-1, keepdims=True) + EPS)


def _in_proj_kernel(x_ref, shift_ref, scale_ref, zinit_ref, wrm_ref, wt_ref, kg_ref, cw_ref, ogc_ref,
                    u_ref, yc_ref, zt_ref, k_ref, v_ref, qt_ref, vt_ref, zprev, *, widths):
    bb, tr, d = x_ref.shape

    @pl.when(pl.program_id(1) == 0)
    def _():
        zprev[...] = zinit_ref[...]

    x = x_ref[...]
    ms = jnp.mean(x * x, axis=-1, keepdims=True)
    h = x * lax.rsqrt(ms + EPS) * (1.0 + scale_ref[...]) + shift_ref[...]
    h = h.reshape(bb * tr, d).astype(BF16)

    w_ssm, w_conv, w_kv, w_attn = widths
    pc = jnp.dot(h, wrm_ref[:, 0:3 * w_conv], preferred_element_type=F32)
    gb = pc[:, 0:w_conv].reshape(bb, tr, w_conv)
    z = (pc[:, w_conv:2 * w_conv] * pc[:, 2 * w_conv:3 * w_conv]).reshape(bb, tr, w_conv)

    zp = jnp.concatenate([zprev[...], z], axis=1)
    ztail = z[:, tr - SUBLANES:, :]
    zprev[...] = ztail
    zt_ref[...] = ztail
    cw = cw_ref[...]

    def conv_part(c, n):
        if bb == 1:
            rc = tr // n
            zs, gs = zp[:, c * rc:c * rc + rc + SUBLANES], gb[:, c * rc:(c + 1) * rc]
        else:
            bc, rc = bb // n, tr
            zs, gs = zp[c * bc:(c + 1) * bc], gb[c * bc:(c + 1) * bc]
        conv = (zs[:, SUBLANES - 2:SUBLANES - 2 + rc] * cw[0:1] + zs[:, SUBLANES - 1:SUBLANES - 1 + rc] * cw[1:2]
                + zs[:, SUBLANES:] * cw[2:3])
        y = (_rms(gs * conv) * ogc_ref[...]).astype(yc_ref.dtype)
        rows = y.shape[0] * y.shape[1]
        yc_ref[c * rows:(c + 1) * rows, :] = y.reshape(rows, w_conv)

    n_part = 4
    assert (tr if bb == 1 else bb) % n_part == 0
    col = 3 * w_conv
    for c, width in enumerate((w_ssm // 2, w_ssm // 2, 2 * w_kv)):
        conv_part(c, n_part)
        pp = jnp.dot(h, wrm_ref[:, col:col + width], preferred_element_type=F32)
        col += width
        if c < 2:
            u_ref[:, c * width:(c + 1) * width] = pp
        else:
            k = pp[:, :w_kv]
            v_ref[...] = pp[:, w_kv:]
    conv_part(n_part - 1, n_part)

    lo = lax.broadcasted_iota(jnp.int32, k.shape, 1) < HEAD_DIM
    k2 = k * k
    s_lo = jnp.sum(jnp.where(lo, k2, 0.0), axis=-1, keepdims=True)
    s_hi = jnp.sum(jnp.where(lo, 0.0, k2), axis=-1, keepdims=True)
    ssq = jnp.where(lo, s_lo, s_hi)
    k_ref[...] = k * lax.rsqrt(ssq * (1.0 / HEAD_DIM) + EPS) * kg_ref[...]

    pt = lax.dot_general(wt_ref[...], h, (((1,), (1,)), ((), ())), preferred_element_type=F32)
    qt_ref[...] = pt[:w_attn]
    vt_ref[...] = pt[w_attn:]


TOK_TILE = 512


def _in_proj_call(x3, mod3, zinit, wrm, wt, kg2, cw, ogc, widths):
    nb, r, d = x3.shape
    w_ssm, w_conv, w_kv, w_attn = widths
    bb, tr = _tok_blocks(nb, r, TOK_TILE)
    tmm = bb * tr
    assert tmm == TOK_TILE
    nt = r // tr
    ttot = nb * r
    tok = lambda b, t: (b * nt + t, 0)
    tok_t = lambda b, t: (b * nt + t, 0, 0)
    halo_spec = pl.BlockSpec((bb, SUBLANES, w_conv), lambda b, t: (b, 0, 0))
    out_shape = [jax.ShapeDtypeStruct((ttot, w_ssm), F32), jax.ShapeDtypeStruct((ttot, w_conv), BF16),
                 jax.ShapeDtypeStruct((nb, SUBLANES, w_conv), F32),
                 jax.ShapeDtypeStruct((ttot, w_kv), F32), jax.ShapeDtypeStruct((ttot, w_kv), F32),
                 jax.ShapeDtypeStruct((ttot // tmm, w_attn, tmm), F32),
                 jax.ShapeDtypeStruct((ttot // tmm, w_kv, tmm), F32)]
    out_specs = [pl.BlockSpec((tmm, w_ssm), tok), pl.BlockSpec((tmm, w_conv), tok), halo_spec,
                 pl.BlockSpec((tmm, w_kv), tok), pl.BlockSpec((tmm, w_kv), tok),
                 pl.BlockSpec((None, w_attn, tmm), tok_t), pl.BlockSpec((None, w_kv, tmm), tok_t)]
    consts = (wrm, wt, kg2, cw, ogc)
    return pl.pallas_call(
        functools.partial(_in_proj_kernel, widths=widths),
        grid=(nb // bb, nt),
        in_specs=[pl.BlockSpec((bb, tr, d), lambda b, t: (b, t, 0)),
                  pl.BlockSpec((bb, 1, d), lambda b, t: (b, 0, 0)),
                  pl.BlockSpec((bb, 1, d), lambda b, t: (b, 0, 1)),
                  halo_spec] + [_const_spec(c.shape) for c in consts],
        out_specs=out_specs,
        out_shape=out_shape,
        scratch_shapes=[pltpu.VMEM((bb, SUBLANES, w_conv), F32)],
        compiler_params=_params("arbitrary", "arbitrary"),
    )(x3, mod3, mod3, zinit, *consts)


SCAN_LANE_BLOCKS = 8


def _gelu_tanh(x):
    return 0.5 * x * (1.0 + jnp.tanh(math.sqrt(2.0 / math.pi) * (x + 0.044715 * (x * x * x))))


def _segment_perm(ts):
    lseg = ts // SUBLANES
    p = np.zeros((ts, ts), np.float32)
    i, j = np.meshgrid(np.arange(lseg), np.arange(SUBLANES), indexing="ij")
    p[(i * SUBLANES + j).ravel(), (j * lseg + i).ravel()] = 1.0
    return p


def _ssm_kernel(u_ref, h0r_ref, h0i_ref, perm_ref, permt_ref, fb_ref, ab_ref, sg_ref, pw_ref, cb_ref, dskip_ref,
                wglu_ref, og_ref, y_ref, hre_ref, him_ref, bre, bim, car):
    ts, w = u_ref.shape
    ns = bre.shape[1]
    lseg = ts // SUBLANES
    half_u = w // 2
    half_s = ns // 2
    t = pl.program_id(1)

    @pl.when(t == 0)
    def _():
        car[0:1, :] = h0r_ref[...]
        car[1:2, :] = h0i_ref[...]

    u = u_ref[...]
    up = jnp.dot(perm_ref[...], u.astype(BF16), preferred_element_type=F32).astype(BF16)
    for j in range(2):
        bb = jnp.dot(up[:, j * half_u:(j + 1) * half_u], fb_ref[j], preferred_element_type=F32)
        bre[:, j * half_s:(j + 1) * half_s] = bb[:, :half_s]
        bim[:, j * half_s:(j + 1) * half_s] = bb[:, half_s:]

    row0 = lax.broadcasted_iota(jnp.int32, (SUBLANES, LANES), 0) == 0
    zero = jnp.zeros((SUBLANES, LANES), F32)
    nblk = SCAN_LANE_BLOCKS
    for c0 in range(0, ns // LANES, nblk):
        sls = [pl.ds((c0 + i) * LANES, LANES) for i in range(nblk)]
        ab = [(ab_ref[0, :, sl], ab_ref[1, :, sl]) for sl in sls]

        def pass1(i, carry, sls=sls, ab=ab):
            rows = pl.ds(pl.multiple_of(i * SUBLANES, SUBLANES), SUBLANES)
            new = []
            for sl, (ar, ai), (hr, hi) in zip(sls, ab, carry):
                hr, hi = ar * hr - ai * hi + bre[rows, sl], ar * hi + ai * hr + bim[rows, sl]
                bre[rows, sl] = hr
                bim[rows, sl] = hi
                new.append((hr, hi))
            return tuple(new)

        ends = lax.fori_loop(0, lseg, pass1, tuple((zero, zero) for _ in sls), unroll=2)

        starts = []
        for sl, (er, ei) in zip(sls, ends):
            xr = jnp.where(row0, car[0:1, sl], pltpu.roll(er, 1, axis=0))
            xi = jnp.where(row0, car[1:2, sl], pltpu.roll(ei, 1, axis=0))
            for k in range(3):
                sr = pltpu.roll(xr, 1 << k, axis=0)
                si = pltpu.roll(xi, 1 << k, axis=0)
                mr, mi = sg_ref[2 * k, :, sl], sg_ref[2 * k + 1, :, sl]
                xr, xi = xr + (mr * sr - mi * si), xi + (mr * si + mi * sr)
            lr, li = sg_ref[6, :, sl], sg_ref[7, :, sl]
            nr = lr * xr - li * xi + er
            ni = lr * xi + li * xr + ei
            car[0:1, sl] = nr[SUBLANES - 1:SUBLANES, :]
            car[1:2, sl] = ni[SUBLANES - 1:SUBLANES, :]
            starts.append((xr, xi))

        def pass2(i, c, sls=sls, starts=starts):
            rows = pl.ds(pl.multiple_of(i * SUBLANES, SUBLANES), SUBLANES)
            for sl, (sr, si) in zip(sls, starts):
                pr = pw_ref[0, rows, sl]
                pi = pw_ref[1, rows, sl]
                bre[rows, sl] += pr * sr - pi * si
                bim[rows, sl] += pr * si + pi * sr
            return c

        lax.fori_loop(0, lseg, pass2, 0, unroll=2)

    hre_ref[...] = car[0:1, :]
    him_ref[...] = car[1:2, :]

    ys = []
    for j in range(2):
        hcat = jnp.concatenate([bre[:, j * half_s:(j + 1) * half_s].astype(BF16),
                                bim[:, j * half_s:(j + 1) * half_s].astype(BF16)], axis=-1)
        ys.append(jnp.dot(hcat, cb_ref[j], preferred_element_type=F32))
    yp = jnp.concatenate(ys, axis=-1)
    p0 = yp.astype(BF16)
    r1 = yp - p0.astype(F32)
    p1 = r1.astype(BF16)
    p2 = (r1 - p1.astype(F32)).astype(BF16)
    parts = jnp.dot(permt_ref[...], jnp.concatenate([p0, p1, p2], axis=-1), preferred_element_type=F32)
    y = (parts[:, :w] + parts[:, w:2 * w]) + parts[:, 2 * w:] + dskip_ref[...] * u
    y = _gelu_tanh(y)
    y = y * jax.nn.sigmoid(jnp.dot(y.astype(BF16), wglu_ref[...], preferred_element_type=F32))
    y_ref[...] = (_rms(y) * og_ref[...]).astype(y_ref.dtype)


SSM_TILE = 256


def _ssm_call(u, h0r, h0i, fb, scan, cb, dskip, wglu, ogs, nb, r):
    ttot, w = u.shape
    ab, sg, pw = scan
    ns = ab.shape[-1]
    ts = min(SSM_TILE, r)
    assert pw.shape[1] == ts
    nt = r // ts
    perm = _segment_perm(ts)
    perm, permt = jnp.asarray(perm, BF16), jnp.asarray(perm.T, BF16)
    st_spec = pl.BlockSpec((None, 1, ns), lambda b, t: (b, 0, 0))
    consts = (perm, permt, fb, ab, sg, pw, cb, dskip, wglu, ogs)
    return pl.pallas_call(
        _ssm_kernel,
        grid=(nb, nt),
        in_specs=[pl.BlockSpec((ts, w), lambda b, t: (b * nt + t, 0)), st_spec, st_spec]
        + [_const_spec(c.shape) for c in consts],
        out_specs=[pl.BlockSpec((ts, w), lambda b, t: (b * nt + t, 0)), st_spec, st_spec],
        out_shape=[jax.ShapeDtypeStruct((ttot, w), BF16),
                   jax.ShapeDtypeStruct((nb, 1, ns), F32), jax.ShapeDtypeStruct((nb, 1, ns), F32)],
        scratch_shapes=[pltpu.VMEM((ts, ns), F32), pltpu.VMEM((ts, ns), F32), pltpu.VMEM((SUBLANES, ns), F32)],
        compiler_params=_params("arbitrary", "arbitrary"),
    )(u, h0r, h0i, *consts)


def _scan_consts(a1, lseg):
    def cmul(x, y):
        return x[0] * y[0] - x[1] * y[1], x[0] * y[1] + x[1] * y[0]

    tr, ti = a1[0][None], a1[1][None]
    while tr.shape[0] < lseg:
        nr, ni = cmul((tr, ti), (tr[-1], ti[-1]))
        tr, ti = jnp.concatenate([tr, nr]), jnp.concatenate([ti, ni])
    pw = jnp.repeat(jnp.stack([tr[:lseg], ti[:lseg]]), SUBLANES, axis=1)
    ns = a1[0].shape[0]
    ab = jnp.stack([jnp.broadcast_to(a1[0], (SUBLANES, ns)), jnp.broadcast_to(a1[1], (SUBLANES, ns))])
    al = (tr[lseg - 1], ti[lseg - 1])
    al2 = cmul(al, al)
    al4 = cmul(al2, al2)
    row = jnp.arange(SUBLANES)[:, None]
    rows = []
    for k, a in enumerate((al, al2, al4)):
        keep = row >= (1 << k)
        rows += [jnp.where(keep, a[0][None, :], 0.0), jnp.where(keep, a[1][None, :], 0.0)]
    rows += [jnp.broadcast_to(al[0], (SUBLANES, ns)), jnp.broadcast_to(al[1], (SUBLANES, ns))]
    return ab, jnp.stack(rows), pw


def _ssm_consts(a_re, a_im, log_dt, b_re, b_im, c_re, c_im):
    g, p = a_re.shape
    hh = b_re.shape[-1]
    ar, ai = a_re.astype(F32), a_im.astype(F32)
    dt = jnp.exp(log_dt.astype(F32))[:, None]
    mag = jnp.exp(dt * ar)
    abar_re, abar_im = mag * jnp.cos(dt * ai), mag * jnp.sin(dt * ai)
    den = ar * ar + ai * ai
    f_re = ((abar_re - 1.0) * ar + abar_im * ai) / den
    f_im = (abar_im * ar - (abar_re - 1.0) * ai) / den
    fb_re = f_re[..., None] * b_re - f_im[..., None] * b_im
    fb_im = f_re[..., None] * b_im + f_im[..., None] * b_re
    gh = g // 2
    eye = jnp.eye(gh, dtype=F32)

    def in_blk(m):
        return jnp.einsum('gph,gk->ghkp', m, eye).reshape(gh * hh, gh * p)

    def out_blk(m):
        return jnp.einsum('ghp,gk->kpgh', m, eye).reshape(gh * p, gh * hh)

    fb = jnp.stack([jnp.concatenate([in_blk(fb_re[j * gh:(j + 1) * gh]), in_blk(fb_im[j * gh:(j + 1) * gh])], axis=1)
                    for j in range(2)]).astype(BF16)
    cb = jnp.stack([jnp.concatenate([out_blk(c_re[j * gh:(j + 1) * gh]), -out_blk(c_im[j * gh:(j + 1) * gh])], axis=0)
                    for j in range(2)]).astype(BF16)

    return fb, (abar_re.reshape(-1), abar_im.reshape(-1)), cb


def _attn_kernel(qt_ref, k_ref, vt_ref, kc_ref, vtc_ref, bias_ref, gq_ref, sink_ref, og_ref, yt_ref,
                 *, sample, pairs_per_seq):
    n_rows, tq = qt_ref.shape
    npairs = tq // PAIR
    qpg = n_rows // (N_KV_HEADS * HEAD_DIM)
    i = pl.program_id(0)
    gq = gq_ref[...]
    if sample:
        k_all, vt_all = k_ref[...], vt_ref[...]
    else:
        k_all = jnp.concatenate([kc_ref[...], k_ref[...]], axis=0)
        vt_all = jnp.concatenate([vtc_ref[...], vt_ref[...]], axis=1)
    for p in range(npairs):
        if sample:
            kb = jnp.concatenate([kc_ref[p * 2 * WINDOW:(p + 1) * 2 * WINDOW, :],
                                  k_all[p * PAIR:(p + 1) * PAIR, :]], axis=0)
            vtb = jnp.concatenate([vtc_ref[:, p * 2 * WINDOW:(p + 1) * 2 * WINDOW],
                                   vt_all[:, p * PAIR:(p + 1) * PAIR]], axis=1)
            sel = 0
        else:
            kb = k_all[p * PAIR:p * PAIR + WINDOW + PAIR, :]
            vtb = vt_all[:, p * PAIR:p * PAIR + WINDOW + PAIR]
            sel = ((i * npairs + p) % pairs_per_seq == 0).astype(jnp.int32)
        kb = kb.astype(BF16)
        vtb = vtb.astype(BF16)
        ones = jnp.ones((2 * SUBLANES, vtb.shape[1]), BF16)
        outs = []
        ssq = jnp.zeros((1, PAIR), F32)
        for g in range(N_KV_HEADS):
            qs = []
            for hh in range(qpg):
                r0 = (g * qpg + hh) * HEAD_DIM
                q = qt_ref[r0:r0 + HEAD_DIM, p * PAIR:(p + 1) * PAIR]
                ms = jnp.sum(q * q, axis=0, keepdims=True) * (1.0 / HEAD_DIM)
                qs.append((q * lax.rsqrt(ms + EPS) * gq).astype(BF16))
            qg = jnp.concatenate(qs, axis=1)
            zero = jnp.zeros_like(qg)
            qpad = jnp.concatenate([qg, zero] if g == 0 else [zero, qg], axis=0)
            s = jnp.dot(kb, qpad, preferred_element_type=F32) + bias_ref[sel, g]
            sink = sink_ref[g]
            m = jnp.maximum(jnp.max(s, axis=0, keepdims=True), sink)
            e = jnp.exp2(s - m).astype(BF16)
            va = jnp.concatenate([vtb[g * HEAD_DIM:(g + 1) * HEAD_DIM, :], ones], axis=0)
            oa = jnp.dot(va, e, preferred_element_type=F32)
            den = oa[HEAD_DIM:HEAD_DIM + 1, :] + jnp.exp2(sink - m)
            o = oa[:HEAD_DIM, :] / den
            outs.append(o)
            sq = jnp.sum(o * o, axis=0, keepdims=True)
            for hh in range(qpg):
                ssq = ssq + sq[:, hh * PAIR:(hh + 1) * PAIR]
        rn = lax.rsqrt(ssq * (1.0 / n_rows) + EPS)
        rn = jnp.concatenate([rn] * qpg, axis=1)
        for g in range(N_KV_HEADS):
            o = outs[g] * rn * og_ref[g]
            for hh in range(qpg):
                r0 = (g * qpg + hh) * HEAD_DIM
                yt_ref[r0:r0 + HEAD_DIM, p * PAIR:(p + 1) * PAIR] = \
                    o[:, hh * PAIR:(hh + 1) * PAIR].astype(yt_ref.dtype)


def _attn_call(qt, k, vt, kc, vtc, bias, gq, sink, oga, sample, seq_len):
    ntile, n_rows, tq = qt.shape
    kvw = k.shape[1]
    assert sample or seq_len % tq == 0
    npairs = tq // PAIR
    tile3 = lambda i: (i, 0, 0)
    if sample:
        ctx = 2 * WINDOW * npairs
        kc_spec = pl.BlockSpec((ctx, kvw), lambda i: (i, 0))
        vtc_spec = pl.BlockSpec((kvw, ctx), lambda i: (0, i))
    else:
        kc_spec = pl.BlockSpec((WINDOW, kvw), lambda i: (jnp.maximum(i * npairs - 1, 0), 0))
        vtc_spec = pl.BlockSpec((None, kvw, WINDOW), lambda i: (jnp.maximum(i - 1, 0), 0, tq // WINDOW - 1))
    return pl.pallas_call(
        functools.partial(_attn_kernel, sample=sample, pairs_per_seq=seq_len // PAIR),
        grid=(ntile,),
        in_specs=[pl.BlockSpec((None, n_rows, tq), tile3),
                  pl.BlockSpec((tq, kvw), lambda i: (i, 0)),
                  pl.BlockSpec((None, kvw, tq), tile3),
                  kc_spec, vtc_spec,
                  _const_spec(bias.shape), _const_spec(gq.shape), _const_spec(sink.shape),
                  _const_spec(oga.shape)],
        out_specs=pl.BlockSpec((None, n_rows, tq), tile3),
        out_shape=jax.ShapeDtypeStruct((ntile, n_rows, tq), BF16),
        compiler_params=_params("arbitrary"),
    )(qt, k, vt, kc, vtc, bias, gq, sink, oga)


def _out_proj_kernel(x_ref, gate_ref, ys_ref, yt_ref, yc_ref, wo_ref, xo_ref):
    bb, tr, d = x_ref.shape
    ws, wa = ys_ref.shape[1], yt_ref.shape[0]
    o = jnp.dot(ys_ref[...], wo_ref[0:ws, :], preferred_element_type=F32)
    o += lax.dot_general(yt_ref[...], wo_ref[ws:ws + wa, :], (((0,), (0,)), ((), ())),
                         preferred_element_type=F32)
    o += jnp.dot(yc_ref[...], wo_ref[ws + wa:, :], preferred_element_type=F32)
    xo_ref[...] = x_ref[...] + gate_ref[...] * o.reshape(bb, tr, d)


def _out_proj_call(x3, mod3, ys, yt, yc, wo):
    nb, r, d = x3.shape
    ws, wa, wc = ys.shape[1], yt.shape[1], yc.shape[1]
    bb, tr = _tok_blocks(nb, r, TOK_TILE)
    tmm = bb * tr
    assert tmm == yt.shape[2]
    nt = r // tr
    tok3 = lambda b, t: (b, t, 0)
    return pl.pallas_call(
        _out_proj_kernel,
        grid=(nb // bb, nt),
        in_specs=[pl.BlockSpec((bb, tr, d), tok3),
                  pl.BlockSpec((bb, 1, d), lambda b, t: (b, 0, 2)),
                  pl.BlockSpec((tmm, ws), lambda b, t: (b * nt + t, 0)),
                  pl.BlockSpec((None, wa, tmm), lambda b, t: (b * nt + t, 0, 0)),
                  pl.BlockSpec((tmm, wc), lambda b, t: (b * nt + t, 0)),
                  _const_spec(wo.shape)],
        out_specs=pl.BlockSpec((bb, tr, d), tok3),
        out_shape=jax.ShapeDtypeStruct((nb, r, d), F32),
        compiler_params=_params("arbitrary", "arbitrary"),
    )(x3, mod3, ys, yt, yc, wo)


FFN_TILE = 512


def _ffn_kernel(x_ref, shift_ref, scale_ref, gate_ref, w1_ref, w2_ref, o_ref, h_s):
    bb, tr, d = x_ref.shape
    j = pl.program_id(2)

    @pl.when(j == 0)
    def _():
        x = x_ref[...]
        h = x * lax.rsqrt(jnp.mean(x * x, axis=-1, keepdims=True) + EPS) * (1.0 + scale_ref[...]) + shift_ref[...]
        h_s[...] = h.reshape(bb * tr, d).astype(BF16)
        o_ref[...] = jnp.zeros_like(o_ref)

    a = jnp.maximum(jnp.dot(h_s[...], w1_ref[...], preferred_element_type=F32), 0.0)
    o_ref[...] += jnp.dot((a * a).astype(BF16), w2_ref[...], preferred_element_type=F32).reshape(bb, tr, d)

    @pl.when(j == pl.num_programs(2) - 1)
    def _():
        o_ref[...] = x_ref[...] + gate_ref[...] * o_ref[...]


def _ffn_call(x3, mod3, w1, w2, tm):
    nb, r, d = x3.shape
    tf = FFN_TILE
    nj = w1.shape[1] // tf
    bb, tr = _tok_blocks(nb, r, tm)
    tok3 = lambda b, t, j: (b, t, 0)
    mod_spec = lambda c: pl.BlockSpec((bb, 1, d), lambda b, t, j: (b, 0, c))
    return pl.pallas_call(
        _ffn_kernel,
        grid=(nb // bb, r // tr, nj),
        in_specs=[pl.BlockSpec((bb, tr, d), tok3), mod_spec(3), mod_spec(4), mod_spec(5),
                  pl.BlockSpec((d, tf), lambda b, t, j: (0, j)),
                  pl.BlockSpec((tf, d), lambda b, t, j: (j, 0))],
        out_specs=pl.BlockSpec((bb, tr, d), tok3),
        out_shape=jax.ShapeDtypeStruct((nb, r, d), F32),
        scratch_shapes=[pltpu.VMEM((bb * tr, d), BF16)],
        compiler_params=_params("arbitrary", "arbitrary", "arbitrary"),
    )(x3, mod3, mod3, mod3, w1, w2)


def _layer(x3, mod3, lw, bias, sample, state):
    nb, r, d = x3.shape
    widths = lw["widths"]
    w_conv = widths[1]
    ns = lw["abar"][0].shape[0]
    scan = _scan_consts(lw["abar"], min(SSM_TILE, r) // SUBLANES)
    if sample:
        cache_k, cache_v, h0r, h0i, conv_buf = state
        h0r = h0r.reshape(nb, 1, ns)
        h0i = h0i.reshape(nb, 1, ns)
        zinit = jnp.pad(conv_buf, ((0, 0), (SUBLANES - (CONV_K - 1), 0), (0, 0)))
    else:
        h0r = h0i = jnp.zeros((nb, 1, ns), F32)
        zinit = jnp.zeros((nb, SUBLANES, w_conv), F32)
    u, yc, ztail, k, v, qt, vt = _in_proj_call(x3, mod3, zinit, lw["wrm"], lw["wt"], lw["kg2"], lw["cw"],
                                               lw["ogc"], widths)
    ys, hre, him = _ssm_call(u, h0r, h0i, lw["fb"], scan, lw["cb"], lw["dskip"], lw["wglu"], lw["ogs"], nb, r)

    kvw = k.shape[1]
    if sample:
        n_buf = cache_k.shape[1]
        kc = cache_k.reshape(nb * n_buf, kvw)
        vtc = cache_v.reshape(nb * n_buf, kvw).T
    else:
        kc, vtc = k, vt
    yt = _attn_call(qt, k, vt, kc, vtc, bias, lw["gq"], lw["sink"], lw["oga"], sample, r)

    x3 = _out_proj_call(x3, mod3, ys, yt, yc, lw["wo"])
    x3 = _ffn_call(x3, mod3, lw["w1"], lw["w2"], tm=1024)

    keep = min(r, WINDOW)
    k4 = k.reshape(nb, r, kvw)[:, r - keep:].reshape(nb, keep, N_KV_HEADS, HEAD_DIM)
    v4 = v.reshape(nb, r, kvw)[:, r - keep:].reshape(nb, keep, N_KV_HEADS, HEAD_DIM)
    if sample:
        new_k = jnp.concatenate([cache_k, k4], axis=1)[:, -n_buf:]
        new_v = jnp.concatenate([cache_v, v4], axis=1)[:, -n_buf:]
    else:
        new_k, new_v = k4, v4
    g = ns // SSM_STATE
    return (x3, new_k, new_v, hre.reshape(nb, g, SSM_STATE), him.reshape(nb, g, SSM_STATE),
            ztail[:, -(CONV_K - 1):])


def _layer_weights(l, w_in, ssm_a_re, ssm_a_im, ssm_log_dt, ssm_b_re, ssm_b_im, ssm_c_re, ssm_c_im, ssm_d,
                   ssm_w_glu, q_norm_g, k_norm_g, attn_sinks, conv_w, out_norm_g, w_out, w_ff1, w_ff2):
    w_ssm = ssm_d.shape[1]
    n_heads = attn_sinks.shape[1]
    w_attn = n_heads * HEAD_DIM
    w_kv = N_KV_HEADS * HEAD_DIM
    w_conv = conv_w.shape[1]
    qpg = n_heads // N_KV_HEADS
    wi = w_in[l]
    o = np.cumsum([0, w_ssm, w_attn, w_kv, w_kv, w_conv, w_conv, w_conv])
    wu, wq, wk, wv, wgb, wgc, wxc = (wi[:, o[i]:o[i + 1]] for i in range(7))
    fb, abar, cb = _ssm_consts(ssm_a_re[l], ssm_a_im[l], ssm_log_dt[l], ssm_b_re[l], ssm_b_im[l],
                               ssm_c_re[l], ssm_c_im[l])
    og = out_norm_g[l].astype(F32)
    d_model, d_ff = w_ff1.shape[1:]
    return {
        "widths": (w_ssm, w_conv, w_kv, w_attn),
        "wrm": jnp.concatenate([wgb, wgc, wxc, wu, wk, wv], axis=1).astype(BF16),
        "wt": jnp.concatenate([wq, wv], axis=1).T.astype(BF16),
        "kg2": jnp.tile(k_norm_g[l].astype(F32), N_KV_HEADS)[None, :],
        "fb": fb, "abar": abar, "cb": cb,
        "dskip": ssm_d[l].astype(F32)[None, :],
        "wglu": ssm_w_glu[l].astype(BF16),
        "gq": jnp.broadcast_to((q_norm_g[l].astype(F32) * (HEAD_DIM ** -0.5 * LOG2E))[:, None], (HEAD_DIM, PAIR)),
        "sink": jnp.broadcast_to((attn_sinks[l].astype(F32) * LOG2E).reshape(N_KV_HEADS, 1, qpg, 1),
                                 (N_KV_HEADS, 1, qpg, PAIR)).reshape(N_KV_HEADS, 1, qpg * PAIR),
        "cw": conv_w[l].astype(F32).T,
        "ogs": og[None, :w_ssm],
        "oga": jnp.broadcast_to(og[w_ssm:w_ssm + w_attn].reshape(N_KV_HEADS, qpg, HEAD_DIM, 1).transpose(0, 2, 1, 3),
                                (N_KV_HEADS, HEAD_DIM, qpg, PAIR)).reshape(N_KV_HEADS, HEAD_DIM, qpg * PAIR),
        "ogc": og[None, w_ssm + w_attn:],
        "wo": w_out[l].astype(BF16),
        "w1": w_ff1[l].astype(BF16),
        "w2": w_ff2[l].astype(BF16),
    }


def kernel(x_prompt, x_sample, cache_k, cache_v, state_ssm_re, state_ssm_im, state_conv, c_prompt, c_sample, rel_bias, w_ada, b_ada, w_in, ssm_a_re, ssm_a_im, ssm_log_dt, ssm_b_re, ssm_b_im, ssm_c_re, ssm_c_im, ssm_d, ssm_w_glu, q_norm_g, k_norm_g, attn_sinks, conv_w, out_norm_g, w_out, w_ff1, w_ff2):
    depth = w_in.shape[0]
    nbp, nbs = x_prompt.shape[0], x_sample.shape[0]
    assert x_sample.shape[1] == CHUNK and cache_k.shape[2] == WINDOW and nbs % 2 == 0

    nc = nbp + nbs
    ncp = -(-nc // SUBLANES) * SUBLANES
    c_all = jnp.pad(jnp.concatenate([c_prompt, c_sample], axis=0), ((0, ncp - nc), (0, 0)))
    mod = _mod_call(c_all, w_ada, b_ada)

    bias_p = _bias_call(rel_bias, sample=False)
    bias_s = _bias_call(rel_bias, sample=True)

    xp, xs = x_prompt, x_sample
    outs_p, outs_s = [], []
    for l in range(depth):
        lw = _layer_weights(l, w_in, ssm_a_re, ssm_a_im, ssm_log_dt, ssm_b_re, ssm_b_im, ssm_c_re, ssm_c_im,
                            ssm_d, ssm_w_glu, q_norm_g, k_norm_g, attn_sinks, conv_w, out_norm_g, w_out,
                            w_ff1, w_ff2)
        mod_p = mod[l, :nbp][:, None, :]
        mod_s = mod[l, nbp:nc][:, None, :]
        xp, *rest_p = _layer(xp, mod_p, lw, bias_p, False, None)
        xs, *rest_s = _layer(xs, mod_s, lw, bias_s, True,
                             (cache_k[l], cache_v[l], state_ssm_re[l], state_ssm_im[l], state_conv[l]))
        outs_p.append(rest_p)
        outs_s.append(rest_s)
    stack = lambda outs, i: jnp.stack([o[i] for o in outs])
    return (xp, xs,
            *(stack(outs_p, i) for i in range(5)),
            *(stack(outs_s, i) for i in range(5)))
```

```python
import functools
import math

import numpy as np
import jax
import jax.numpy as jnp
from jax import lax
from jax.experimental import pallas as pl
from jax.experimental.pallas import tpu as pltpu

F32 = jnp.float32
BF16 = jnp.bfloat16

CHUNK = 64
SSM_GROUP = 16
SSM_STATE = 64
HEAD_DIM = 64
N_KV_HEADS = 2
WINDOW = 128
CONV_K = 3
REL_BUCKETS = 32
REL_MAX_DIST = 64
EPS = 1e-6
NEG_INF = -1e30
LOG2E = math.log2(math.e)

LANES = 128
SUBLANES = 8
PAIR = 2 * CHUNK
VMEM_LIMIT_BYTES = 56 * 1024 * 1024


def _params(*sem):
    return pltpu.CompilerParams(dimension_semantics=sem, vmem_limit_bytes=VMEM_LIMIT_BYTES)


def _const_spec(shape):
    nd = len(shape)
    return pl.BlockSpec(shape, lambda *_: (0,) * nd, pipeline_mode=pl.Buffered(1))


def _tok_blocks(nb, r, tm):
    if r >= tm:
        assert r % tm == 0
        return 1, tm
    bb = min(tm // r, nb)
    assert nb % bb == 0
    return bb, r


def _mod_kernel(c_ref, w_ref, b_ref, o_ref):
    c = c_ref[...]
    s = (c * jax.nn.sigmoid(c)).astype(BF16)
    o_ref[...] = jnp.dot(s, w_ref[...].astype(BF16), preferred_element_type=F32) + b_ref[...]


def _mod_call(c_all, w_ada, b_ada):
    depth, d, n = w_ada.shape
    nc = c_all.shape[0]
    tn = 1024
    return pl.pallas_call(
        _mod_kernel,
        grid=(depth, n // tn),
        in_specs=[pl.BlockSpec((nc, d), lambda l, j: (0, 0)),
                  pl.BlockSpec((None, d, tn), lambda l, j: (l, 0, j)),
                  pl.BlockSpec((None, 1, tn), lambda l, j: (l, 0, j))],
        out_specs=pl.BlockSpec((None, nc, tn), lambda l, j: (l, 0, j)),
        out_shape=jax.ShapeDtypeStruct((depth, nc, n), F32),
        compiler_params=_params("arbitrary", "arbitrary"),
    )(c_all, w_ada, b_ada.reshape(depth, 1, n))


def _bucket_maps(sample):
    lane = np.arange(PAIR)[None, :]
    if sample:
        nk = 2 * WINDOW + PAIR
        row = np.arange(nk)[:, None]
        own = row >= 2 * WINDOW
        key_b = np.where(own, (row - 2 * WINDOW) // CHUNK, row // WINDOW)
        key_s = np.where(own, WINDOW + (row - 2 * WINDOW) % CHUNK, row % WINDOW)
        rel = key_s - WINDOW - lane % CHUNK
        visible = key_b == lane // CHUNK
    else:
        nk = WINDOW + PAIR
        row = np.arange(nk)[:, None]
        rel = row - WINDOW - lane
        dc = row // CHUNK - lane // CHUNK
        visible = (dc >= 0) & (dc <= WINDOW // CHUNK)
    half = REL_BUCKETS // 2
    exact = half // 2
    n = np.abs(rel)
    nf = np.maximum(n, 1).astype(np.float32)
    far = exact + (np.log(nf / np.float32(exact)) / np.float32(math.log(REL_MAX_DIST / exact))
                   * np.float32(half - exact)).astype(np.int32)
    far = np.minimum(far, half - 1)
    bucket = np.where(rel > 0, half, 0) + np.where(n < exact, n, far)
    full = np.where(visible, bucket, -1).astype(np.int32)
    first = np.where(row >= WINDOW, full, -1).astype(np.int32)
    return np.stack([full, first])


def _bias_kernel(table_ref, bucket_ref, o_ref):
    h = pl.program_id(1)
    bucket = bucket_ref[...]
    acc = jnp.full(bucket.shape, NEG_INF, F32)
    for b in range(REL_BUCKETS):
        acc = jnp.where(bucket == b, table_ref[b, h] * LOG2E, acc)
    o_ref[...] = acc


def _bias_call(rel_bias, sample):
    buckets = jnp.asarray(_bucket_maps(sample))
    _, nk, _ = buckets.shape
    n_heads = rel_bias.shape[1]
    qpg = n_heads // N_KV_HEADS
    return pl.pallas_call(
        _bias_kernel,
        grid=(2, n_heads),
        in_specs=[pl.BlockSpec(memory_space=pltpu.SMEM),
                  pl.BlockSpec((None, nk, PAIR), lambda s, h: (s, 0, 0))],
        out_specs=pl.BlockSpec((None, None, nk, PAIR), lambda s, h: (s, h // qpg, 0, h % qpg)),
        out_shape=jax.ShapeDtypeStruct((2, N_KV_HEADS, nk, qpg * PAIR), F32),
        compiler_params=_params("arbitrary", "arbitrary"),
    )(rel_bias, buckets)


def _rms(y):
    return y * lax.rsqrt(jnp.mean(y * y, axis=-1, keepdims=True) + EPS)


def _in_proj_kernel(x_ref, shift_ref, scale_ref, zinit_ref, wrm_ref, wt_ref, kg_ref, cw_ref, ogc_ref,
                    u_ref, yc_ref, zt_ref, k_ref, v_ref, qt_ref, vt_ref, zprev, *, widths):
    bb, tr, d = x_ref.shape

    @pl.when(pl.program_id(1) == 0)
    def _():
        zprev[...] = zinit_ref[...]

    x = x_ref[...]
    ms = jnp.mean(x * x, axis=-1, keepdims=True)
    h = x * lax.rsqrt(ms + EPS) * (1.0 + scale_ref[...]) + shift_ref[...]
    h = h.reshape(bb * tr, d).astype(BF16)

    w_ssm, w_conv, w_kv, w_attn = widths
    pc = jnp.dot(h, wrm_ref[:, 0:3 * w_conv], preferred_element_type=F32)
    gb = pc[:, 0:w_conv].reshape(bb, tr, w_conv)
    z = (pc[:, w_conv:2 * w_conv] * pc[:, 2 * w_conv:3 * w_conv]).reshape(bb, tr, w_conv)

    zp = jnp.concatenate([zprev[...], z], axis=1)
    ztail = z[:, tr - SUBLANES:, :]
    zprev[...] = ztail
    zt_ref[...] = ztail
    cw = cw_ref[...]

    def conv_part(c, n):
        if bb == 1:
            rc = tr // n
            zs, gs = zp[:, c * rc:c * rc + rc + SUBLANES], gb[:, c * rc:(c + 1) * rc]
        else:
            bc, rc = bb // n, tr
            zs, gs = zp[c * bc:(c + 1) * bc], gb[c * bc:(c + 1) * bc]
        conv = (zs[:, SUBLANES - 2:SUBLANES - 2 + rc] * cw[0:1] + zs[:, SUBLANES - 1:SUBLANES - 1 + rc] * cw[1:2]
                + zs[:, SUBLANES:] * cw[2:3])
        y = (_rms(gs * conv) * ogc_ref[...]).astype(yc_ref.dtype)
        rows = y.shape[0] * y.shape[1]
        yc_ref[c * rows:(c + 1) * rows, :] = y.reshape(rows, w_conv)

    n_part = 4
    assert (tr if bb == 1 else bb) % n_part == 0
    col = 3 * w_conv
    for c, width in enumerate((w_ssm // 2, w_ssm // 2, 2 * w_kv)):
        conv_part(c, n_part)
        pp = jnp.dot(h, wrm_ref[:, col:col + width], preferred_element_type=F32)
        col += width
        if c < 2:
            u_ref[:, c * width:(c + 1) * width] = pp
        else:
            k = pp[:, :w_kv]
            v_ref[...] = pp[:, w_kv:]
    conv_part(n_part - 1, n_part)

    lo = lax.broadcasted_iota(jnp.int32, k.shape, 1) < HEAD_DIM
    k2 = k * k
    s_lo = jnp.sum(jnp.where(lo, k2, 0.0), axis=-1, keepdims=True)
    s_hi = jnp.sum(jnp.where(lo, 0.0, k2), axis=-1, keepdims=True)
    ssq = jnp.where(lo, s_lo, s_hi)
    k_ref[...] = k * lax.rsqrt(ssq * (1.0 / HEAD_DIM) + EPS) * kg_ref[...]

    pt = lax.dot_general(wt_ref[...], h, (((1,), (1,)), ((), ())), preferred_element_type=F32)
    qt_ref[...] = pt[:w_attn]
    vt_ref[...] = pt[w_attn:]


TOK_TILE = 512


def _in_proj_call(x3, mod3, zinit, wrm, wt, kg2, cw, ogc, widths):
    nb, r, d = x3.shape
    w_ssm, w_conv, w_kv, w_attn = widths
    bb, tr = _tok_blocks(nb, r, TOK_TILE)
    tmm = bb * tr
    assert tmm == TOK_TILE
    nt = r // tr
    ttot = nb * r
    tok = lambda b, t: (b * nt + t, 0)
    tok_t = lambda b, t: (b * nt + t, 0, 0)
    halo_spec = pl.BlockSpec((bb, SUBLANES, w_conv), lambda b, t: (b, 0, 0))
    out_shape = [jax.ShapeDtypeStruct((ttot, w_ssm), F32), jax.ShapeDtypeStruct((ttot, w_conv), BF16),
                 jax.ShapeDtypeStruct((nb, SUBLANES, w_conv), F32),
                 jax.ShapeDtypeStruct((ttot, w_kv), F32), jax.ShapeDtypeStruct((ttot, w_kv), F32),
                 jax.ShapeDtypeStruct((ttot // tmm, w_attn, tmm), F32),
                 jax.ShapeDtypeStruct((ttot // tmm, w_kv, tmm), F32)]
    out_specs = [pl.BlockSpec((tmm, w_ssm), tok), pl.BlockSpec((tmm, w_conv), tok), halo_spec,
                 pl.BlockSpec((tmm, w_kv), tok), pl.BlockSpec((tmm, w_kv), tok),
                 pl.BlockSpec((None, w_attn, tmm), tok_t), pl.BlockSpec((None, w_kv, tmm), tok_t)]
    consts = (wrm, wt, kg2, cw, ogc)
    return pl.pallas_call(
        functools.partial(_in_proj_kernel, widths=widths),
        grid=(nb // bb, nt),
        in_specs=[pl.BlockSpec((bb, tr, d), lambda b, t: (b, t, 0)),
                  pl.BlockSpec((bb, 1, d), lambda b, t: (b, 0, 0)),
                  pl.BlockSpec((bb, 1, d), lambda b, t: (b, 0, 1)),
                  halo_spec] + [_const_spec(c.shape) for c in consts],
        out_specs=out_specs,
        out_shape=out_shape,
        scratch_shapes=[pltpu.VMEM((bb, SUBLANES, w_conv), F32)],
        compiler_params=_params("arbitrary", "arbitrary"),
    )(x3, mod3, mod3, zinit, *consts)


SCAN_LANE_BLOCKS = 8


def _gelu_tanh(x):
    return 0.5 * x * (1.0 + jnp.tanh(math.sqrt(2.0 / math.pi) * (x + 0.044715 * (x * x * x))))


def _segment_perm(ts):
    lseg = ts // SUBLANES
    p = np.zeros((ts, ts), np.float32)
    i, j = np.meshgrid(np.arange(lseg), np.arange(SUBLANES), indexing="ij")
    p[(i * SUBLANES + j).ravel(), (j * lseg + i).ravel()] = 1.0
    return p


def _ssm_kernel(u_ref, h0r_ref, h0i_ref, perm_ref, permt_ref, fb_ref, ab_ref, sg_ref, pw_ref, cb_ref, dskip_ref,
                wglu_ref, og_ref, y_ref, hre_ref, him_ref, bre, bim, car, *, segs_per_seq):
    ts, w = u_ref.shape
    ns = bre.shape[1]
    lseg = ts // SUBLANES
    half_u = w // 2
    half_s = ns // 2
    t = pl.program_id(1)

    @pl.when(t == 0)
    def _():
        car[0:SUBLANES, :] = h0r_ref[...]
        car[SUBLANES:, :] = h0i_ref[...]

    u = u_ref[...]
    up = jnp.dot(perm_ref[...], u.astype(BF16), preferred_element_type=F32).astype(BF16)
    for j in range(2):
        bb = jnp.dot(up[:, j * half_u:(j + 1) * half_u], fb_ref[j], preferred_element_type=F32)
        bre[:, j * half_s:(j + 1) * half_s] = bb[:, :half_s]
        bim[:, j * half_s:(j + 1) * half_s] = bb[:, half_s:]

    seq_start = lax.broadcasted_iota(jnp.int32, (SUBLANES, LANES), 0) % segs_per_seq == 0
    zero = jnp.zeros((SUBLANES, LANES), F32)
    nblk = SCAN_LANE_BLOCKS
    for c0 in range(0, ns // LANES, nblk):
        sls = [pl.ds((c0 + i) * LANES, LANES) for i in range(nblk)]
        ab = [(ab_ref[0, :, sl], ab_ref[1, :, sl]) for sl in sls]

        def pass1(i, carry, sls=sls, ab=ab):
            rows = pl.ds(pl.multiple_of(i * SUBLANES, SUBLANES), SUBLANES)
            new = []
            for sl, (ar, ai), (hr, hi) in zip(sls, ab, carry):
                hr, hi = ar * hr - ai * hi + bre[rows, sl], ar * hi + ai * hr + bim[rows, sl]
                bre[rows, sl] = hr
                bim[rows, sl] = hi
                new.append((hr, hi))
            return tuple(new)

        ends = lax.fori_loop(0, lseg, pass1, tuple((zero, zero) for _ in sls), unroll=2)

        starts = []
        for sl, (er, ei) in zip(sls, ends):
            xr = jnp.where(seq_start, car[0:SUBLANES, sl], pltpu.roll(er, 1, axis=0))
            xi = jnp.where(seq_start, car[SUBLANES:, sl], pltpu.roll(ei, 1, axis=0))
            for k in range(segs_per_seq.bit_length() - 1):
                sr = pltpu.roll(xr, 1 << k, axis=0)
                si = pltpu.roll(xi, 1 << k, axis=0)
                mr, mi = sg_ref[2 * k, :, sl], sg_ref[2 * k + 1, :, sl]
                xr, xi = xr + (mr * sr - mi * si), xi + (mr * si + mi * sr)
            lr, li = sg_ref[6, :, sl], sg_ref[7, :, sl]
            nr = lr * xr - li * xi + er
            ni = lr * xi + li * xr + ei
            hre_ref[:, sl] = nr
            him_ref[:, sl] = ni
            car[0:SUBLANES, sl] = jnp.broadcast_to(nr[SUBLANES - 1:SUBLANES, :], (SUBLANES, LANES))
            car[SUBLANES:, sl] = jnp.broadcast_to(ni[SUBLANES - 1:SUBLANES, :], (SUBLANES, LANES))
            starts.append((xr, xi))

        def pass2(i, c, sls=sls, starts=starts):
            rows = pl.ds(pl.multiple_of(i * SUBLANES, SUBLANES), SUBLANES)
            for sl, (sr, si) in zip(sls, starts):
                pr = pw_ref[0, rows, sl]
                pi = pw_ref[1, rows, sl]
                bre[rows, sl] += pr * sr - pi * si
                bim[rows, sl] += pr * si + pi * sr
            return c

        lax.fori_loop(0, lseg, pass2, 0, unroll=2)

    ys = []
    for j in range(2):
        hcat = jnp.concatenate([bre[:, j * half_s:(j + 1) * half_s].astype(BF16),
                                bim[:, j * half_s:(j + 1) * half_s].astype(BF16)], axis=-1)
        ys.append(jnp.dot(hcat, cb_ref[j], preferred_element_type=F32))
    yp = jnp.concatenate(ys, axis=-1)
    p0 = yp.astype(BF16)
    r1 = yp - p0.astype(F32)
    p1 = r1.astype(BF16)
    p2 = (r1 - p1.astype(F32)).astype(BF16)
    parts = jnp.dot(permt_ref[...], jnp.concatenate([p0, p1, p2], axis=-1), preferred_element_type=F32)
    y = (parts[:, :w] + parts[:, w:2 * w]) + parts[:, 2 * w:] + dskip_ref[...] * u
    y = _gelu_tanh(y)
    y = y * jax.nn.sigmoid(jnp.dot(y.astype(BF16), wglu_ref[...], preferred_element_type=F32))
    y_ref[...] = (_rms(y) * og_ref[...]).astype(y_ref.dtype)


SSM_TILE = 256


def _ssm_call(u, h0r, h0i, fb, scan, cb, dskip, wglu, ogs, r):
    ttot, w = u.shape
    ab, sg, pw = scan
    ns = ab.shape[-1]
    ts = SSM_TILE
    lseg = ts // SUBLANES
    assert pw.shape[1] == ts
    nseq = ttot // r
    if r >= ts:
        assert r % ts == 0
        seq_per_tile, nt = 1, r // ts
    else:
        assert r % lseg == 0 and ts % r == 0 and nseq % (ts // r) == 0
        seq_per_tile, nt = ts // r, 1
    segs_per_seq = SUBLANES // seq_per_tile
    ngrp = nseq // seq_per_tile
    init = lambda h: jnp.repeat(h.reshape(ngrp, seq_per_tile, ns), segs_per_seq, axis=1)
    perm = _segment_perm(ts)
    perm, permt = jnp.asarray(perm, BF16), jnp.asarray(perm.T, BF16)
    st_spec = pl.BlockSpec((None, SUBLANES, ns), lambda b, t: (b, 0, 0))
    consts = (perm, permt, fb, ab, sg, pw, cb, dskip, wglu, ogs)
    y, hre, him = pl.pallas_call(
        functools.partial(_ssm_kernel, segs_per_seq=segs_per_seq),
        grid=(ngrp, nt),
        in_specs=[pl.BlockSpec((ts, w), lambda b, t: (b * nt + t, 0)), st_spec, st_spec]
        + [_const_spec(c.shape) for c in consts],
        out_specs=[pl.BlockSpec((ts, w), lambda b, t: (b * nt + t, 0)), st_spec, st_spec],
        out_shape=[jax.ShapeDtypeStruct((ttot, w), BF16),
                   jax.ShapeDtypeStruct((ngrp, SUBLANES, ns), F32), jax.ShapeDtypeStruct((ngrp, SUBLANES, ns), F32)],
        scratch_shapes=[pltpu.VMEM((ts, ns), F32), pltpu.VMEM((ts, ns), F32), pltpu.VMEM((2 * SUBLANES, ns), F32)],
        compiler_params=_params("arbitrary", "arbitrary"),
    )(u, init(h0r), init(h0i), *consts)
    last = lambda h: h[:, segs_per_seq - 1::segs_per_seq].reshape(nseq, ns)
    return y, last(hre), last(him)


def _scan_consts(a1, lseg, segs_per_seq):
    def cmul(x, y):
        return x[0] * y[0] - x[1] * y[1], x[0] * y[1] + x[1] * y[0]

    tr, ti = a1[0][None], a1[1][None]
    while tr.shape[0] < lseg:
        nr, ni = cmul((tr, ti), (tr[-1], ti[-1]))
        tr, ti = jnp.concatenate([tr, nr]), jnp.concatenate([ti, ni])
    pw = jnp.repeat(jnp.stack([tr[:lseg], ti[:lseg]]), SUBLANES, axis=1)
    ns = a1[0].shape[0]
    ab = jnp.stack([jnp.broadcast_to(a1[0], (SUBLANES, ns)), jnp.broadcast_to(a1[1], (SUBLANES, ns))])
    al = (tr[lseg - 1], ti[lseg - 1])
    al2 = cmul(al, al)
    al4 = cmul(al2, al2)
    row = jnp.arange(SUBLANES)[:, None]
    rows = []
    for k, a in enumerate((al, al2, al4)):
        keep = row % segs_per_seq >= (1 << k)
        rows += [jnp.where(keep, a[0][None, :], 0.0), jnp.where(keep, a[1][None, :], 0.0)]
    rows += [jnp.broadcast_to(al[0], (SUBLANES, ns)), jnp.broadcast_to(al[1], (SUBLANES, ns))]
    return ab, jnp.stack(rows), pw


def _ssm_consts(a_re, a_im, log_dt, b_re, b_im, c_re, c_im):
    g, p = a_re.shape
    hh = b_re.shape[-1]
    ar, ai = a_re.astype(F32), a_im.astype(F32)
    dt = jnp.exp(log_dt.astype(F32))[:, None]
    mag = jnp.exp(dt * ar)
    abar_re, abar_im = mag * jnp.cos(dt * ai), mag * jnp.sin(dt * ai)
    den = ar * ar + ai * ai
    f_re = ((abar_re - 1.0) * ar + abar_im * ai) / den
    f_im = (abar_im * ar - (abar_re - 1.0) * ai) / den
    fb_re = f_re[..., None] * b_re - f_im[..., None] * b_im
    fb_im = f_re[..., None] * b_im + f_im[..., None] * b_re
    gh = g // 2
    eye = jnp.eye(gh, dtype=F32)

    def in_blk(m):
        return jnp.einsum('gph,gk->ghkp', m, eye).reshape(gh * hh, gh * p)

    def out_blk(m):
        return jnp.einsum('ghp,gk->kpgh', m, eye).reshape(gh * p, gh * hh)

    fb = jnp.stack([jnp.concatenate([in_blk(fb_re[j * gh:(j + 1) * gh]), in_blk(fb_im[j * gh:(j + 1) * gh])], axis=1)
                    for j in range(2)]).astype(BF16)
    cb = jnp.stack([jnp.concatenate([out_blk(c_re[j * gh:(j + 1) * gh]), -out_blk(c_im[j * gh:(j + 1) * gh])], axis=0)
                    for j in range(2)]).astype(BF16)

    return fb, (abar_re.reshape(-1), abar_im.reshape(-1)), cb


def _attn_kernel(qt_ref, k_ref, vt_ref, kc_ref, vtc_ref, bias_ref, gq_ref, sink_ref, og_ref, yt_ref,
                 *, sample, pairs_per_seq):
    n_rows, tq = qt_ref.shape
    npairs = tq // PAIR
    qpg = n_rows // (N_KV_HEADS * HEAD_DIM)
    i = pl.program_id(0)
    gq = gq_ref[...]
    if sample:
        k_all, vt_all = k_ref[...], vt_ref[...]
    else:
        k_all = jnp.concatenate([kc_ref[...], k_ref[...]], axis=0)
        vt_all = jnp.concatenate([vtc_ref[...], vt_ref[...]], axis=1)
    for p in range(npairs):
        if sample:
            kb = jnp.concatenate([kc_ref[p * 2 * WINDOW:(p + 1) * 2 * WINDOW, :],
                                  k_all[p * PAIR:(p + 1) * PAIR, :]], axis=0)
            vtb = jnp.concatenate([vtc_ref[:, p * 2 * WINDOW:(p + 1) * 2 * WINDOW],
                                   vt_all[:, p * PAIR:(p + 1) * PAIR]], axis=1)
            sel = 0
        else:
            kb = k_all[p * PAIR:p * PAIR + WINDOW + PAIR, :]
            vtb = vt_all[:, p * PAIR:p * PAIR + WINDOW + PAIR]
            sel = ((i * npairs + p) % pairs_per_seq == 0).astype(jnp.int32)
        kb = kb.astype(BF16)
        vtb = vtb.astype(BF16)
        ones = jnp.ones((2 * SUBLANES, vtb.shape[1]), BF16)
        outs = []
        ssq = jnp.zeros((1, PAIR), F32)
        for g in range(N_KV_HEADS):
            qs = []
            for hh in range(qpg):
                r0 = (g * qpg + hh) * HEAD_DIM
                q = qt_ref[r0:r0 + HEAD_DIM, p * PAIR:(p + 1) * PAIR]
                ms = jnp.sum(q * q, axis=0, keepdims=True) * (1.0 / HEAD_DIM)
                qs.append((q * lax.rsqrt(ms + EPS) * gq).astype(BF16))
            qg = jnp.concatenate(qs, axis=1)
            zero = jnp.zeros_like(qg)
            qpad = jnp.concatenate([qg, zero] if g == 0 else [zero, qg], axis=0)
            s = jnp.dot(kb, qpad, preferred_element_type=F32) + bias_ref[sel, g]
            sink = sink_ref[g]
            m = jnp.maximum(jnp.max(s, axis=0, keepdims=True), sink)
            e = jnp.exp2(s - m).astype(BF16)
            va = jnp.concatenate([vtb[g * HEAD_DIM:(g + 1) * HEAD_DIM, :], ones], axis=0)
            oa = jnp.dot(va, e, preferred_element_type=F32)
            den = oa[HEAD_DIM:HEAD_DIM + 1, :] + jnp.exp2(sink - m)
            o = oa[:HEAD_DIM, :] / den
            outs.append(o)
            sq = jnp.sum(o * o, axis=0, keepdims=True)
            for hh in range(qpg):
                ssq = ssq + sq[:, hh * PAIR:(hh + 1) * PAIR]
        rn = lax.rsqrt(ssq * (1.0 / n_rows) + EPS)
        rn = jnp.concatenate([rn] * qpg, axis=1)
        for g in range(N_KV_HEADS):
            o = outs[g] * rn * og_ref[g]
            for hh in range(qpg):
                r0 = (g * qpg + hh) * HEAD_DIM
                yt_ref[r0:r0 + HEAD_DIM, p * PAIR:(p + 1) * PAIR] = \
                    o[:, hh * PAIR:(hh + 1) * PAIR].astype(yt_ref.dtype)


def _attn_call(qt, k, vt, kc, vtc, bias, gq, sink, oga, sample, seq_len):
    ntile, n_rows, tq = qt.shape
    kvw = k.shape[1]
    assert sample or seq_len % tq == 0
    npairs = tq // PAIR
    tile3 = lambda i: (i, 0, 0)
    if sample:
        ctx = 2 * WINDOW * npairs
        kc_spec = pl.BlockSpec((ctx, kvw), lambda i: (i, 0))
        vtc_spec = pl.BlockSpec((kvw, ctx), lambda i: (0, i))
    else:
        kc_spec = pl.BlockSpec((WINDOW, kvw), lambda i: (jnp.maximum(i * npairs - 1, 0), 0))
        vtc_spec = pl.BlockSpec((None, kvw, WINDOW), lambda i: (jnp.maximum(i - 1, 0), 0, tq // WINDOW - 1))
    return pl.pallas_call(
        functools.partial(_attn_kernel, sample=sample, pairs_per_seq=seq_len // PAIR),
        grid=(ntile,),
        in_specs=[pl.BlockSpec((None, n_rows, tq), tile3),
                  pl.BlockSpec((tq, kvw), lambda i: (i, 0)),
                  pl.BlockSpec((None, kvw, tq), tile3),
                  kc_spec, vtc_spec,
                  _const_spec(bias.shape), _const_spec(gq.shape), _const_spec(sink.shape),
                  _const_spec(oga.shape)],
        out_specs=pl.BlockSpec((None, n_rows, tq), tile3),
        out_shape=jax.ShapeDtypeStruct((ntile, n_rows, tq), BF16),
        compiler_params=_params("arbitrary"),
    )(qt, k, vt, kc, vtc, bias, gq, sink, oga)


def _out_proj_kernel(x_ref, gate_ref, ys_ref, yt_ref, yc_ref, wo_ref, xo_ref):
    bb, tr, d = x_ref.shape
    ws, wa = ys_ref.shape[1], yt_ref.shape[0]
    o = jnp.dot(ys_ref[...], wo_ref[0:ws, :], preferred_element_type=F32)
    o += lax.dot_general(yt_ref[...], wo_ref[ws:ws + wa, :], (((0,), (0,)), ((), ())),
                         preferred_element_type=F32)
    o += jnp.dot(yc_ref[...], wo_ref[ws + wa:, :], preferred_element_type=F32)
    xo_ref[...] = x_ref[...] + gate_ref[...] * o.reshape(bb, tr, d)


def _out_proj_call(x3, mod3, ys, yt, yc, wo):
    nb, r, d = x3.shape
    ws, wa, wc = ys.shape[1], yt.shape[1], yc.shape[1]
    bb, tr = _tok_blocks(nb, r, TOK_TILE)
    tmm = bb * tr
    assert tmm == yt.shape[2]
    nt = r // tr
    tok3 = lambda b, t: (b, t, 0)
    return pl.pallas_call(
        _out_proj_kernel,
        grid=(nb // bb, nt),
        in_specs=[pl.BlockSpec((bb, tr, d), tok3),
                  pl.BlockSpec((bb, 1, d), lambda b, t: (b, 0, 2)),
                  pl.BlockSpec((tmm, ws), lambda b, t: (b * nt + t, 0)),
                  pl.BlockSpec((None, wa, tmm), lambda b, t: (b * nt + t, 0, 0)),
                  pl.BlockSpec((tmm, wc), lambda b, t: (b * nt + t, 0)),
                  _const_spec(wo.shape)],
        out_specs=pl.BlockSpec((bb, tr, d), tok3),
        out_shape=jax.ShapeDtypeStruct((nb, r, d), F32),
        compiler_params=_params("arbitrary", "arbitrary"),
    )(x3, mod3, ys, yt, yc, wo)


FFN_TILE = 512


NORM_ROWS = 64


def _ffn_kernel(x_ref, shift_ref, scale_ref, gate_ref, w1_ref, w2_ref, o_ref, h_s, r_s):
    bb, tr, d = x_ref.shape
    j = pl.program_id(2)

    @pl.when(j == 0)
    def _():
        per = tr // NORM_ROWS

        def chunk(c):
            return c // per, pl.ds(pl.multiple_of((c % per) * NORM_ROWS, NORM_ROWS), NORM_ROWS)

        def scales(c, carry):
            b, rows = chunk(c)
            x = x_ref[b, rows, :]
            r = lax.rsqrt(jnp.mean(x * x, axis=-1, keepdims=True) + EPS)
            r_s[pl.ds(pl.multiple_of(c * NORM_ROWS, NORM_ROWS), NORM_ROWS), :] = jnp.broadcast_to(r, (NORM_ROWS, LANES))
            return carry

        def rows_out(c, carry):
            b, rows = chunk(c)
            flat = pl.ds(pl.multiple_of(c * NORM_ROWS, NORM_ROWS), NORM_ROWS)
            r = jnp.concatenate([r_s[flat, :]] * (d // LANES), axis=1)
            h = x_ref[b, rows, :] * r * (1.0 + scale_ref[b]) + shift_ref[b]
            h_s[flat, :] = h.astype(BF16)
            return carry

        n = bb * per
        lax.fori_loop(0, n, scales, 0, unroll=4)
        lax.fori_loop(0, n, rows_out, 0)

    def partial_out():
        a = jnp.maximum(jnp.dot(h_s[...], w1_ref[...], preferred_element_type=F32), 0.0)
        return jnp.dot((a * a).astype(BF16), w2_ref[...], preferred_element_type=F32).reshape(bb, tr, d)

    @pl.when(j == 0)
    def _():
        o_ref[...] = partial_out()

    @pl.when(j > 0)
    def _():
        o_ref[...] += partial_out()

    @pl.when(j == pl.num_programs(2) - 1)
    def _():
        o_ref[...] = x_ref[...] + gate_ref[...] * o_ref[...]


def _ffn_call(x3, mod3, w1, w2, tm):
    nb, r, d = x3.shape
    tf = FFN_TILE
    nj = w1.shape[1] // tf
    bb, tr = _tok_blocks(nb, r, tm)
    tok3 = lambda b, t, j: (b, t, 0)
    mod_spec = lambda c: pl.BlockSpec((bb, 1, d), lambda b, t, j: (b, 0, c))
    return pl.pallas_call(
        _ffn_kernel,
        grid=(nb // bb, r // tr, nj),
        in_specs=[pl.BlockSpec((bb, tr, d), tok3), mod_spec(3), mod_spec(4), mod_spec(5),
                  pl.BlockSpec((d, tf), lambda b, t, j: (0, j)),
                  pl.BlockSpec((tf, d), lambda b, t, j: (j, 0))],
        out_specs=pl.BlockSpec((bb, tr, d), tok3),
        out_shape=jax.ShapeDtypeStruct((nb, r, d), F32),
        scratch_shapes=[pltpu.VMEM((bb * tr, d), BF16), pltpu.VMEM((bb * tr, LANES), F32)],
        compiler_params=_params("arbitrary", "arbitrary", "arbitrary"),
    )(x3, mod3, mod3, mod3, w1, w2)


def _layer(x3, mod3, lw, bias, sample, state):
    nb, r, d = x3.shape
    widths = lw["widths"]
    w_conv = widths[1]
    ns = lw["abar"][0].shape[0]
    scan = _scan_consts(lw["abar"], SSM_TILE // SUBLANES, SUBLANES // max(SSM_TILE // r, 1))
    if sample:
        cache_k, cache_v, h0r, h0i, conv_buf = state
        h0r = h0r.reshape(nb, ns)
        h0i = h0i.reshape(nb, ns)
        zinit = jnp.pad(conv_buf, ((0, 0), (SUBLANES - (CONV_K - 1), 0), (0, 0)))
    else:
        h0r = h0i = jnp.zeros((nb, ns), F32)
        zinit = jnp.zeros((nb, SUBLANES, w_conv), F32)
    u, yc, ztail, k, v, qt, vt = _in_proj_call(x3, mod3, zinit, lw["wrm"], lw["wt"], lw["kg2"], lw["cw"],
                                               lw["ogc"], widths)
    ys, hre, him = _ssm_call(u, h0r, h0i, lw["fb"], scan, lw["cb"], lw["dskip"], lw["wglu"], lw["ogs"], r)

    kvw = k.shape[1]
    if sample:
        n_buf = cache_k.shape[1]
        kc = cache_k.reshape(nb * n_buf, kvw)
        vtc = cache_v.reshape(nb * n_buf, kvw).T
    else:
        kc, vtc = k, vt
    yt = _attn_call(qt, k, vt, kc, vtc, bias, lw["gq"], lw["sink"], lw["oga"], sample, r)

    x3 = _out_proj_call(x3, mod3, ys, yt, yc, lw["wo"])
    x3 = _ffn_call(x3, mod3, lw["w1"], lw["w2"], tm=1024)

    keep = min(r, WINDOW)
    k4 = k.reshape(nb, r, kvw)[:, r - keep:].reshape(nb, keep, N_KV_HEADS, HEAD_DIM)
    v4 = v.reshape(nb, r, kvw)[:, r - keep:].reshape(nb, keep, N_KV_HEADS, HEAD_DIM)
    if sample:
        new_k = jnp.concatenate([cache_k, k4], axis=1)[:, -n_buf:]
        new_v = jnp.concatenate([cache_v, v4], axis=1)[:, -n_buf:]
    else:
        new_k, new_v = k4, v4
    g = ns // SSM_STATE
    return (x3, new_k, new_v, hre.reshape(nb, g, SSM_STATE), him.reshape(nb, g, SSM_STATE),
            ztail[:, -(CONV_K - 1):])


def _layer_weights(l, w_in, ssm_a_re, ssm_a_im, ssm_log_dt, ssm_b_re, ssm_b_im, ssm_c_re, ssm_c_im, ssm_d,
                   ssm_w_glu, q_norm_g, k_norm_g, attn_sinks, conv_w, out_norm_g, w_out, w_ff1, w_ff2):
    w_ssm = ssm_d.shape[1]
    n_heads = attn_sinks.shape[1]
    w_attn = n_heads * HEAD_DIM
    w_kv = N_KV_HEADS * HEAD_DIM
    w_conv = conv_w.shape[1]
    qpg = n_heads // N_KV_HEADS
    wi = w_in[l]
    o = np.cumsum([0, w_ssm, w_attn, w_kv, w_kv, w_conv, w_conv, w_conv])
    wu, wq, wk, wv, wgb, wgc, wxc = (wi[:, o[i]:o[i + 1]] for i in range(7))
    fb, abar, cb = _ssm_consts(ssm_a_re[l], ssm_a_im[l], ssm_log_dt[l], ssm_b_re[l], ssm_b_im[l],
                               ssm_c_re[l], ssm_c_im[l])
    og = out_norm_g[l].astype(F32)
    d_model, d_ff = w_ff1.shape[1:]
    return {
        "widths": (w_ssm, w_conv, w_kv, w_attn),
        "wrm": jnp.concatenate([wgb, wgc, wxc, wu, wk, wv], axis=1).astype(BF16),
        "wt": jnp.concatenate([wq, wv], axis=1).T.astype(BF16),
        "kg2": jnp.tile(k_norm_g[l].astype(F32), N_KV_HEADS)[None, :],
        "fb": fb, "abar": abar, "cb": cb,
        "dskip": ssm_d[l].astype(F32)[None, :],
        "wglu": ssm_w_glu[l].astype(BF16),
        "gq": jnp.broadcast_to((q_norm_g[l].astype(F32) * (HEAD_DIM ** -0.5 * LOG2E))[:, None], (HEAD_DIM, PAIR)),
        "sink": jnp.broadcast_to((attn_sinks[l].astype(F32) * LOG2E).reshape(N_KV_HEADS, 1, qpg, 1),
                                 (N_KV_HEADS, 1, qpg, PAIR)).reshape(N_KV_HEADS, 1, qpg * PAIR),
        "cw": conv_w[l].astype(F32).T,
        "ogs": og[None, :w_ssm],
        "oga": jnp.broadcast_to(og[w_ssm:w_ssm + w_attn].reshape(N_KV_HEADS, qpg, HEAD_DIM, 1).transpose(0, 2, 1, 3),
                                (N_KV_HEADS, HEAD_DIM, qpg, PAIR)).reshape(N_KV_HEADS, HEAD_DIM, qpg * PAIR),
        "ogc": og[None, w_ssm + w_attn:],
        "wo": w_out[l].astype(BF16),
        "w1": w_ff1[l].astype(BF16),
        "w2": w_ff2[l].astype(BF16),
    }


def kernel(x_prompt, x_sample, cache_k, cache_v, state_ssm_re, state_ssm_im, state_conv, c_prompt, c_sample, rel_bias, w_ada, b_ada, w_in, ssm_a_re, ssm_a_im, ssm_log_dt, ssm_b_re, ssm_b_im, ssm_c_re, ssm_c_im, ssm_d, ssm_w_glu, q_norm_g, k_norm_g, attn_sinks, conv_w, out_norm_g, w_out, w_ff1, w_ff2):
    depth = w_in.shape[0]
    nbp, nbs = x_prompt.shape[0], x_sample.shape[0]
    assert x_sample.shape[1] == CHUNK and cache_k.shape[2] == WINDOW and nbs % 2 == 0

    nc = nbp + nbs
    ncp = -(-nc // SUBLANES) * SUBLANES
    c_all = jnp.pad(jnp.concatenate([c_prompt, c_sample], axis=0), ((0, ncp - nc), (0, 0)))
    mod = _mod_call(c_all, w_ada, b_ada)

    bias_p = _bias_call(rel_bias, sample=False)
    bias_s = _bias_call(rel_bias, sample=True)

    xp, xs = x_prompt, x_sample
    outs_p, outs_s = [], []
    for l in range(depth):
        lw = _layer_weights(l, w_in, ssm_a_re, ssm_a_im, ssm_log_dt, ssm_b_re, ssm_b_im, ssm_c_re, ssm_c_im,
                            ssm_d, ssm_w_glu, q_norm_g, k_norm_g, attn_sinks, conv_w, out_norm_g, w_out,
                            w_ff1, w_ff2)
        mod_p = mod[l, :nbp][:, None, :]
        mod_s = mod[l, nbp:nc][:, None, :]
        xp, *rest_p = _layer(xp, mod_p, lw, bias_p, False, None)
        xs, *rest_s = _layer(xs, mod_s, lw, bias_s, True,
                             (cache_k[l], cache_v[l], state_ssm_re[l], state_ssm_im[l], state_conv[l]))
        outs_p.append(rest_p)
        outs_s.append(rest_s)
    stack = lambda outs, i: jnp.stack([o[i] for o in outs])
    return (xp, xs,
            *(stack(outs_p, i) for i in range(5)),
            *(stack(outs_s, i) for i in range(5)))
```

```python
import functools
import math

import numpy as np
import jax
import jax.numpy as jnp
from jax import lax
from jax.experimental import pallas as pl
from jax.experimental.pallas import tpu as pltpu

F32 = jnp.float32
BF16 = jnp.bfloat16

CHUNK = 64
SSM_GROUP = 16
SSM_STATE = 64
HEAD_DIM = 64
N_KV_HEADS = 2
WINDOW = 128
CONV_K = 3
REL_BUCKETS = 32
REL_MAX_DIST = 64
EPS = 1e-6
NEG_INF = -1e30
LOG2E = math.log2(math.e)

LANES = 128
SUBLANES = 8
PAIR = 2 * CHUNK
VMEM_LIMIT_BYTES = 56 * 1024 * 1024


def _params(*sem):
    return pltpu.CompilerParams(dimension_semantics=sem, vmem_limit_bytes=VMEM_LIMIT_BYTES)


def _const_spec(shape):
    nd = len(shape)
    return pl.BlockSpec(shape, lambda *_: (0,) * nd, pipeline_mode=pl.Buffered(1))


def _tok_blocks(nb, r, tm):
    if r >= tm:
        assert r % tm == 0
        return 1, tm
    bb = min(tm // r, nb)
    assert nb % bb == 0
    return bb, r


def _mod_kernel(c_ref, w_ref, b_ref, o_ref):
    c = c_ref[...]
    s = (c * jax.nn.sigmoid(c)).astype(BF16)
    o_ref[...] = jnp.dot(s, w_ref[...].astype(BF16), preferred_element_type=F32) + b_ref[...]


def _mod_call(c_all, w_ada, b_ada):
    depth, d, n = w_ada.shape
    nc = c_all.shape[0]
    tn = 1024
    return pl.pallas_call(
        _mod_kernel,
        grid=(depth, n // tn),
        in_specs=[pl.BlockSpec((nc, d), lambda l, j: (0, 0)),
                  pl.BlockSpec((None, d, tn), lambda l, j: (l, 0, j)),
                  pl.BlockSpec((None, 1, tn), lambda l, j: (l, 0, j))],
        out_specs=pl.BlockSpec((None, nc, tn), lambda l, j: (l, 0, j)),
        out_shape=jax.ShapeDtypeStruct((depth, nc, n), F32),
        compiler_params=_params("arbitrary", "arbitrary"),
    )(c_all, w_ada, b_ada.reshape(depth, 1, n))


def _bucket_maps(sample):
    lane = np.arange(PAIR)[None, :]
    if sample:
        nk = 2 * WINDOW + PAIR
        row = np.arange(nk)[:, None]
        own = row >= 2 * WINDOW
        key_b = np.where(own, (row - 2 * WINDOW) // CHUNK, row // WINDOW)
        key_s = np.where(own, WINDOW + (row - 2 * WINDOW) % CHUNK, row % WINDOW)
        rel = key_s - WINDOW - lane % CHUNK
        visible = key_b == lane // CHUNK
    else:
        nk = WINDOW + PAIR
        row = np.arange(nk)[:, None]
        rel = row - WINDOW - lane
        dc = row // CHUNK - lane // CHUNK
        visible = (dc >= 0) & (dc <= WINDOW // CHUNK)
    half = REL_BUCKETS // 2
    exact = half // 2
    n = np.abs(rel)
    nf = np.maximum(n, 1).astype(np.float32)
    far = exact + (np.log(nf / np.float32(exact)) / np.float32(math.log(REL_MAX_DIST / exact))
                   * np.float32(half - exact)).astype(np.int32)
    far = np.minimum(far, half - 1)
    bucket = np.where(rel > 0, half, 0) + np.where(n < exact, n, far)
    full = np.where(visible, bucket, -1).astype(np.int32)
    first = np.where(row >= WINDOW, full, -1).astype(np.int32)
    return np.stack([full, first])


def _bias_kernel(table_ref, bucket_ref, o_ref):
    h = pl.program_id(1)
    bucket = bucket_ref[...]
    acc = jnp.full(bucket.shape, NEG_INF, F32)
    for b in range(REL_BUCKETS):
        acc = jnp.where(bucket == b, table_ref[b, h] * LOG2E, acc)
    o_ref[...] = acc


def _bias_call(rel_bias, sample):
    buckets = jnp.asarray(_bucket_maps(sample))
    _, nk, _ = buckets.shape
    n_heads = rel_bias.shape[1]
    qpg = n_heads // N_KV_HEADS
    return pl.pallas_call(
        _bias_kernel,
        grid=(2, n_heads),
        in_specs=[pl.BlockSpec(memory_space=pltpu.SMEM),
                  pl.BlockSpec((None, nk, PAIR), lambda s, h: (s, 0, 0))],
        out_specs=pl.BlockSpec((None, None, nk, PAIR), lambda s, h: (s, h // qpg, 0, h % qpg)),
        out_shape=jax.ShapeDtypeStruct((2, N_KV_HEADS, nk, qpg * PAIR), F32),
        compiler_params=_params("arbitrary", "arbitrary"),
    )(rel_bias, buckets)


def _rms(y):
    return y * lax.rsqrt(jnp.mean(y * y, axis=-1, keepdims=True) + EPS)


def _in_proj_kernel(x_ref, shift_ref, scale_ref, zinit_ref, wrm_ref, wt_ref, kg_ref, cw_ref, ogc_ref, gq_ref,
                    u_ref, yc_ref, zt_ref, k_ref, v_ref, qt_ref, vt_ref, zprev, *, widths):
    bb, tr, d = x_ref.shape

    @pl.when(pl.program_id(1) == 0)
    def _():
        zprev[...] = zinit_ref[...]

    x = x_ref[...]
    ms = jnp.mean(x * x, axis=-1, keepdims=True)
    h = x * lax.rsqrt(ms + EPS) * (1.0 + scale_ref[...]) + shift_ref[...]
    h = h.reshape(bb * tr, d).astype(BF16)

    w_ssm, w_conv, w_kv, w_attn = widths

    pt = lax.dot_general(wt_ref[...], h, (((1,), (1,)), ((), ())), preferred_element_type=F32)
    gq = gq_ref[...]
    for hd in range(w_attn // HEAD_DIM):
        q = pt[hd * HEAD_DIM:(hd + 1) * HEAD_DIM]
        ms = jnp.sum(q * q, axis=0, keepdims=True) * (1.0 / HEAD_DIM)
        qt_ref[hd * HEAD_DIM:(hd + 1) * HEAD_DIM, :] = (q * lax.rsqrt(ms + EPS) * gq).astype(qt_ref.dtype)
    vt_ref[...] = pt[w_attn:].astype(vt_ref.dtype)

    pc = jnp.dot(h, wrm_ref[:, 0:3 * w_conv], preferred_element_type=F32)
    gb = pc[:, 0:w_conv].reshape(bb, tr, w_conv)
    z = (pc[:, w_conv:2 * w_conv] * pc[:, 2 * w_conv:3 * w_conv]).reshape(bb, tr, w_conv)

    zp = jnp.concatenate([zprev[...], z], axis=1)
    ztail = z[:, tr - SUBLANES:, :]
    zprev[...] = ztail
    zt_ref[...] = ztail
    cw = cw_ref[...]

    def conv_part(c, n):
        if bb == 1:
            rc = tr // n
            zs, gs = zp[:, c * rc:c * rc + rc + SUBLANES], gb[:, c * rc:(c + 1) * rc]
        else:
            bc, rc = bb // n, tr
            zs, gs = zp[c * bc:(c + 1) * bc], gb[c * bc:(c + 1) * bc]
        conv = (zs[:, SUBLANES - 2:SUBLANES - 2 + rc] * cw[0:1] + zs[:, SUBLANES - 1:SUBLANES - 1 + rc] * cw[1:2]
                + zs[:, SUBLANES:] * cw[2:3])
        y = (_rms(gs * conv) * ogc_ref[...]).astype(yc_ref.dtype)
        rows = y.shape[0] * y.shape[1]
        yc_ref[c * rows:(c + 1) * rows, :] = y.reshape(rows, w_conv)

    n_part = 4
    assert (tr if bb == 1 else bb) % n_part == 0
    col = 3 * w_conv
    for c, width in enumerate((w_ssm // 2, w_ssm // 2, 2 * w_kv)):
        conv_part(c, n_part)
        pp = jnp.dot(h, wrm_ref[:, col:col + width], preferred_element_type=F32)
        col += width
        if c < 2:
            u_ref[:, c * width:(c + 1) * width] = pp
        else:
            k = pp[:, :w_kv]
            v_ref[...] = pp[:, w_kv:]
    conv_part(n_part - 1, n_part)

    lo = lax.broadcasted_iota(jnp.int32, k.shape, 1) < HEAD_DIM
    k2 = k * k
    s_lo = jnp.sum(jnp.where(lo, k2, 0.0), axis=-1, keepdims=True)
    s_hi = jnp.sum(jnp.where(lo, 0.0, k2), axis=-1, keepdims=True)
    ssq = jnp.where(lo, s_lo, s_hi)
    k_ref[...] = k * lax.rsqrt(ssq * (1.0 / HEAD_DIM) + EPS) * kg_ref[...]


TOK_TILE = 512


def _in_proj_call(x3, mod3, zinit, wrm, wt, kg2, cw, ogc, gq, widths):
    nb, r, d = x3.shape
    w_ssm, w_conv, w_kv, w_attn = widths
    bb, tr = _tok_blocks(nb, r, TOK_TILE)
    tmm = bb * tr
    assert tmm == TOK_TILE
    nt = r // tr
    ttot = nb * r
    tok = lambda b, t: (b * nt + t, 0)
    tok_t = lambda b, t: (b * nt + t, 0, 0)
    halo_spec = pl.BlockSpec((bb, SUBLANES, w_conv), lambda b, t: (b, 0, 0))
    out_shape = [jax.ShapeDtypeStruct((ttot, w_ssm), F32), jax.ShapeDtypeStruct((ttot, w_conv), BF16),
                 jax.ShapeDtypeStruct((nb, SUBLANES, w_conv), F32),
                 jax.ShapeDtypeStruct((ttot, w_kv), F32), jax.ShapeDtypeStruct((ttot, w_kv), F32),
                 jax.ShapeDtypeStruct((ttot // tmm, w_attn, tmm), BF16),
                 jax.ShapeDtypeStruct((ttot // tmm, w_kv, tmm), BF16)]
    out_specs = [pl.BlockSpec((tmm, w_ssm), tok), pl.BlockSpec((tmm, w_conv), tok), halo_spec,
                 pl.BlockSpec((tmm, w_kv), tok), pl.BlockSpec((tmm, w_kv), tok),
                 pl.BlockSpec((None, w_attn, tmm), tok_t), pl.BlockSpec((None, w_kv, tmm), tok_t)]
    consts = (wrm, wt, kg2, cw, ogc, gq)
    return pl.pallas_call(
        functools.partial(_in_proj_kernel, widths=widths),
        grid=(nb // bb, nt),
        in_specs=[pl.BlockSpec((bb, tr, d), lambda b, t: (b, t, 0)),
                  pl.BlockSpec((bb, 1, d), lambda b, t: (b, 0, 0)),
                  pl.BlockSpec((bb, 1, d), lambda b, t: (b, 0, 1)),
                  halo_spec] + [_const_spec(c.shape) for c in consts],
        out_specs=out_specs,
        out_shape=out_shape,
        scratch_shapes=[pltpu.VMEM((bb, SUBLANES, w_conv), F32)],
        compiler_params=_params("arbitrary", "arbitrary"),
    )(x3, mod3, mod3, zinit, *consts)


SCAN_LANE_BLOCKS = 8


def _gelu_tanh(x):
    return 0.5 * x * (1.0 + jnp.tanh(math.sqrt(2.0 / math.pi) * (x + 0.044715 * (x * x * x))))


def _segment_perm(ts):
    lseg = ts // SUBLANES
    p = np.zeros((ts, ts), np.float32)
    i, j = np.meshgrid(np.arange(lseg), np.arange(SUBLANES), indexing="ij")
    p[(i * SUBLANES + j).ravel(), (j * lseg + i).ravel()] = 1.0
    return p


def _ssm_kernel(u_ref, h0r_ref, h0i_ref, perm_ref, permt_ref, fb_ref, ab_ref, sg_ref, pw_ref, cb_ref, dskip_ref,
                wglu_ref, og_ref, y_ref, hre_ref, him_ref, bre, bim, car, *, segs_per_seq):
    ts, w = u_ref.shape
    ns = bre.shape[1]
    lseg = ts // SUBLANES
    half_u = w // 2
    half_s = ns // 2
    t = pl.program_id(1)

    @pl.when(t == 0)
    def _():
        car[0:SUBLANES, :] = h0r_ref[...]
        car[SUBLANES:, :] = h0i_ref[...]

    u = u_ref[...]
    up = jnp.dot(perm_ref[...], u.astype(BF16), preferred_element_type=F32).astype(BF16)
    for j in range(2):
        bb = jnp.dot(up[:, j * half_u:(j + 1) * half_u], fb_ref[j], preferred_element_type=F32)
        bre[:, j * half_s:(j + 1) * half_s] = bb[:, :half_s]
        bim[:, j * half_s:(j + 1) * half_s] = bb[:, half_s:]

    seq_start = lax.broadcasted_iota(jnp.int32, (SUBLANES, LANES), 0) % segs_per_seq == 0
    zero = jnp.zeros((SUBLANES, LANES), F32)
    nblk = SCAN_LANE_BLOCKS
    for c0 in range(0, ns // LANES, nblk):
        sls = [pl.ds((c0 + i) * LANES, LANES) for i in range(nblk)]
        ab = [(ab_ref[0, :, sl], ab_ref[1, :, sl]) for sl in sls]

        def pass1(i, carry, sls=sls, ab=ab):
            rows = pl.ds(pl.multiple_of(i * SUBLANES, SUBLANES), SUBLANES)
            new = []
            for sl, (ar, ai), (hr, hi) in zip(sls, ab, carry):
                hr, hi = ar * hr - ai * hi + bre[rows, sl], ar * hi + ai * hr + bim[rows, sl]
                bre[rows, sl] = hr
                bim[rows, sl] = hi
                new.append((hr, hi))
            return tuple(new)

        ends = lax.fori_loop(0, lseg, pass1, tuple((zero, zero) for _ in sls), unroll=True)

        starts = []
        for sl, (er, ei) in zip(sls, ends):
            xr = jnp.where(seq_start, car[0:SUBLANES, sl], pltpu.roll(er, 1, axis=0))
            xi = jnp.where(seq_start, car[SUBLANES:, sl], pltpu.roll(ei, 1, axis=0))
            for k in range(segs_per_seq.bit_length() - 1):
                sr = pltpu.roll(xr, 1 << k, axis=0)
                si = pltpu.roll(xi, 1 << k, axis=0)
                mr, mi = sg_ref[2 * k, :, sl], sg_ref[2 * k + 1, :, sl]
                xr, xi = xr + (mr * sr - mi * si), xi + (mr * si + mi * sr)
            lr, li = sg_ref[6, :, sl], sg_ref[7, :, sl]
            nr = lr * xr - li * xi + er
            ni = lr * xi + li * xr + ei
            hre_ref[:, sl] = nr
            him_ref[:, sl] = ni
            car[0:SUBLANES, sl] = jnp.broadcast_to(nr[SUBLANES - 1:SUBLANES, :], (SUBLANES, LANES))
            car[SUBLANES:, sl] = jnp.broadcast_to(ni[SUBLANES - 1:SUBLANES, :], (SUBLANES, LANES))
            starts.append((xr, xi))

        def pass2(i, c, sls=sls, starts=starts):
            rows = pl.ds(pl.multiple_of(i * SUBLANES, SUBLANES), SUBLANES)
            for sl, (sr, si) in zip(sls, starts):
                pr = pw_ref[0, rows, sl]
                pi = pw_ref[1, rows, sl]
                bre[rows, sl] += pr * sr - pi * si
                bim[rows, sl] += pr * si + pi * sr
            return c

        lax.fori_loop(0, lseg, pass2, 0, unroll=True)

    ys = []
    for j in range(2):
        hcat = jnp.concatenate([bre[:, j * half_s:(j + 1) * half_s].astype(BF16),
                                bim[:, j * half_s:(j + 1) * half_s].astype(BF16)], axis=-1)
        ys.append(jnp.dot(hcat, cb_ref[j], preferred_element_type=F32))
    yp = jnp.concatenate(ys, axis=-1)
    p0 = yp.astype(BF16)
    r1 = yp - p0.astype(F32)
    p1 = r1.astype(BF16)
    p2 = (r1 - p1.astype(F32)).astype(BF16)
    parts = jnp.dot(permt_ref[...], jnp.concatenate([p0, p1, p2], axis=-1), preferred_element_type=F32)
    y = (parts[:, :w] + parts[:, w:2 * w]) + parts[:, 2 * w:] + dskip_ref[...] * u
    y = _gelu_tanh(y)
    y = y * jax.nn.sigmoid(jnp.dot(y.astype(BF16), wglu_ref[...], preferred_element_type=F32))
    y_ref[...] = (_rms(y) * og_ref[...]).astype(y_ref.dtype)


SSM_TILE = 256


def _ssm_call(u, h0r, h0i, fb, scan, cb, dskip, wglu, ogs, r):
    ttot, w = u.shape
    ab, sg, pw = scan
    ns = ab.shape[-1]
    ts = SSM_TILE
    lseg = ts // SUBLANES
    assert pw.shape[1] == ts
    nseq = ttot // r
    if r >= ts:
        assert r % ts == 0
        seq_per_tile, nt = 1, r // ts
    else:
        assert r % lseg == 0 and ts % r == 0 and nseq % (ts // r) == 0
        seq_per_tile, nt = ts // r, 1
    segs_per_seq = SUBLANES // seq_per_tile
    ngrp = nseq // seq_per_tile
    init = lambda h: jnp.repeat(h.reshape(ngrp, seq_per_tile, ns), segs_per_seq, axis=1)
    perm = _segment_perm(ts)
    perm, permt = jnp.asarray(perm, BF16), jnp.asarray(perm.T, BF16)
    st_spec = pl.BlockSpec((None, SUBLANES, ns), lambda b, t: (b, 0, 0))
    consts = (perm, permt, fb, ab, sg, pw, cb, dskip, wglu, ogs)
    y, hre, him = pl.pallas_call(
        functools.partial(_ssm_kernel, segs_per_seq=segs_per_seq),
        grid=(ngrp, nt),
        in_specs=[pl.BlockSpec((ts, w), lambda b, t: (b * nt + t, 0)), st_spec, st_spec]
        + [_const_spec(c.shape) for c in consts],
        out_specs=[pl.BlockSpec((ts, w), lambda b, t: (b * nt + t, 0)), st_spec, st_spec],
        out_shape=[jax.ShapeDtypeStruct((ttot, w), BF16),
                   jax.ShapeDtypeStruct((ngrp, SUBLANES, ns), F32), jax.ShapeDtypeStruct((ngrp, SUBLANES, ns), F32)],
        scratch_shapes=[pltpu.VMEM((ts, ns), F32), pltpu.VMEM((ts, ns), F32), pltpu.VMEM((2 * SUBLANES, ns), F32)],
        compiler_params=_params("arbitrary", "arbitrary"),
    )(u, init(h0r), init(h0i), *consts)
    last = lambda h: h[:, segs_per_seq - 1::segs_per_seq].reshape(nseq, ns)
    return y, last(hre), last(him)


def _scan_consts(a1, lseg, segs_per_seq):
    def cmul(x, y):
        return x[0] * y[0] - x[1] * y[1], x[0] * y[1] + x[1] * y[0]

    tr, ti = a1[0][None], a1[1][None]
    while tr.shape[0] < lseg:
        nr, ni = cmul((tr, ti), (tr[-1], ti[-1]))
        tr, ti = jnp.concatenate([tr, nr]), jnp.concatenate([ti, ni])
    pw = jnp.repeat(jnp.stack([tr[:lseg], ti[:lseg]]), SUBLANES, axis=1)
    ns = a1[0].shape[0]
    ab = jnp.stack([jnp.broadcast_to(a1[0], (SUBLANES, ns)), jnp.broadcast_to(a1[1], (SUBLANES, ns))])
    al = (tr[lseg - 1], ti[lseg - 1])
    al2 = cmul(al, al)
    al4 = cmul(al2, al2)
    row = jnp.arange(SUBLANES)[:, None]
    rows = []
    for k, a in enumerate((al, al2, al4)):
        keep = row % segs_per_seq >= (1 << k)
        rows += [jnp.where(keep, a[0][None, :], 0.0), jnp.where(keep, a[1][None, :], 0.0)]
    rows += [jnp.broadcast_to(al[0], (SUBLANES, ns)), jnp.broadcast_to(al[1], (SUBLANES, ns))]
    return ab, jnp.stack(rows), pw


def _ssm_consts(a_re, a_im, log_dt, b_re, b_im, c_re, c_im):
    g, p = a_re.shape
    hh = b_re.shape[-1]
    ar, ai = a_re.astype(F32), a_im.astype(F32)
    dt = jnp.exp(log_dt.astype(F32))[:, None]
    mag = jnp.exp(dt * ar)
    abar_re, abar_im = mag * jnp.cos(dt * ai), mag * jnp.sin(dt * ai)
    den = ar * ar + ai * ai
    f_re = ((abar_re - 1.0) * ar + abar_im * ai) / den
    f_im = (abar_im * ar - (abar_re - 1.0) * ai) / den
    fb_re = f_re[..., None] * b_re - f_im[..., None] * b_im
    fb_im = f_re[..., None] * b_im + f_im[..., None] * b_re
    gh = g // 2
    eye = jnp.eye(gh, dtype=F32)

    def in_blk(m):
        return jnp.einsum('gph,gk->ghkp', m, eye).reshape(gh * hh, gh * p)

    def out_blk(m):
        return jnp.einsum('ghp,gk->kpgh', m, eye).reshape(gh * p, gh * hh)

    fb = jnp.stack([jnp.concatenate([in_blk(fb_re[j * gh:(j + 1) * gh]), in_blk(fb_im[j * gh:(j + 1) * gh])], axis=1)
                    for j in range(2)]).astype(BF16)
    cb = jnp.stack([jnp.concatenate([out_blk(c_re[j * gh:(j + 1) * gh]), -out_blk(c_im[j * gh:(j + 1) * gh])], axis=0)
                    for j in range(2)]).astype(BF16)

    return fb, (abar_re.reshape(-1), abar_im.reshape(-1)), cb


def _attn_kernel(qt_ref, k_ref, vt_ref, kc_ref, vtc_ref, bias_ref, sink_ref, og_ref, yt_ref,
                 *, sample, pairs_per_seq):
    n_rows, tq = qt_ref.shape
    npairs = tq // PAIR
    qpg = n_rows // (N_KV_HEADS * HEAD_DIM)
    i = pl.program_id(0)
    if sample:
        k_all, vt_all = k_ref[...], vt_ref[...]
    else:
        k_all = jnp.concatenate([kc_ref[...], k_ref[...]], axis=0)
        vt_all = jnp.concatenate([vtc_ref[...], vt_ref[...]], axis=1)
    for p in range(npairs):
        if sample:
            kb = jnp.concatenate([kc_ref[p * 2 * WINDOW:(p + 1) * 2 * WINDOW, :],
                                  k_all[p * PAIR:(p + 1) * PAIR, :]], axis=0)
            vtb = jnp.concatenate([vtc_ref[:, p * 2 * WINDOW:(p + 1) * 2 * WINDOW],
                                   vt_all[:, p * PAIR:(p + 1) * PAIR]], axis=1)
            sel = 0
        else:
            kb = k_all[p * PAIR:p * PAIR + WINDOW + PAIR, :]
            vtb = vt_all[:, p * PAIR:p * PAIR + WINDOW + PAIR]
            sel = ((i * npairs + p) % pairs_per_seq == 0).astype(jnp.int32)
        kb = kb.astype(BF16)
        vtb = vtb.astype(BF16)
        ones = jnp.ones((2 * SUBLANES, vtb.shape[1]), BF16)
        outs = []
        ssq = jnp.zeros((1, PAIR), F32)
        for g in range(N_KV_HEADS):
            qg = jnp.concatenate([qt_ref[(g * qpg + hh) * HEAD_DIM:(g * qpg + hh + 1) * HEAD_DIM,
                                         p * PAIR:(p + 1) * PAIR] for hh in range(qpg)], axis=1)
            zero = jnp.zeros_like(qg)
            qpad = jnp.concatenate([qg, zero] if g == 0 else [zero, qg], axis=0)
            s = jnp.dot(kb, qpad, preferred_element_type=F32) + bias_ref[sel, g]
            sink = sink_ref[g]
            m = jnp.maximum(jnp.max(s, axis=0, keepdims=True), sink)
            e = jnp.exp2(s - m).astype(BF16)
            va = jnp.concatenate([vtb[g * HEAD_DIM:(g + 1) * HEAD_DIM, :], ones], axis=0)
            oa = jnp.dot(va, e, preferred_element_type=F32)
            den = oa[HEAD_DIM:HEAD_DIM + 1, :] + jnp.exp2(sink - m)
            o = oa[:HEAD_DIM, :] / den
            outs.append(o)
            sq = jnp.sum(o * o, axis=0, keepdims=True)
            for hh in range(qpg):
                ssq = ssq + sq[:, hh * PAIR:(hh + 1) * PAIR]
        rn = lax.rsqrt(ssq * (1.0 / n_rows) + EPS)
        rn = jnp.concatenate([rn] * qpg, axis=1)
        for g in range(N_KV_HEADS):
            o = outs[g] * rn * og_ref[g]
            for hh in range(qpg):
                r0 = (g * qpg + hh) * HEAD_DIM
                yt_ref[r0:r0 + HEAD_DIM, p * PAIR:(p + 1) * PAIR] = \
                    o[:, hh * PAIR:(hh + 1) * PAIR].astype(yt_ref.dtype)


def _attn_call(qt, k, vt, kc, vtc, bias, sink, oga, sample, seq_len):
    ntile, n_rows, tq = qt.shape
    kvw = k.shape[1]
    assert sample or seq_len % tq == 0
    npairs = tq // PAIR
    tile3 = lambda i: (i, 0, 0)
    if sample:
        ctx = 2 * WINDOW * npairs
        kc_spec = pl.BlockSpec((ctx, kvw), lambda i: (i, 0))
        vtc_spec = pl.BlockSpec((kvw, ctx), lambda i: (0, i))
    else:
        kc_spec = pl.BlockSpec((WINDOW, kvw), lambda i: (jnp.maximum(i * npairs - 1, 0), 0))
        vtc_spec = pl.BlockSpec((None, kvw, WINDOW), lambda i: (jnp.maximum(i - 1, 0), 0, tq // WINDOW - 1))
    return pl.pallas_call(
        functools.partial(_attn_kernel, sample=sample, pairs_per_seq=seq_len // PAIR),
        grid=(ntile,),
        in_specs=[pl.BlockSpec((None, n_rows, tq), tile3),
                  pl.BlockSpec((tq, kvw), lambda i: (i, 0)),
                  pl.BlockSpec((None, kvw, tq), tile3),
                  kc_spec, vtc_spec,
                  _const_spec(bias.shape), _const_spec(sink.shape),
                  _const_spec(oga.shape)],
        out_specs=pl.BlockSpec((None, n_rows, tq), tile3),
        out_shape=jax.ShapeDtypeStruct((ntile, n_rows, tq), BF16),
        compiler_params=_params("arbitrary"),
    )(qt, k, vt, kc, vtc, bias, sink, oga)


def _out_proj_kernel(x_ref, gate_ref, ys_ref, yt_ref, yc_ref, wo_ref, xo_ref):
    bb, tr, d = x_ref.shape
    ws, wa = ys_ref.shape[1], yt_ref.shape[0]
    o = jnp.dot(ys_ref[...], wo_ref[0:ws, :], preferred_element_type=F32)
    o += lax.dot_general(yt_ref[...], wo_ref[ws:ws + wa, :], (((0,), (0,)), ((), ())),
                         preferred_element_type=F32)
    o += jnp.dot(yc_ref[...], wo_ref[ws + wa:, :], preferred_element_type=F32)
    xo_ref[...] = x_ref[...] + gate_ref[...] * o.reshape(bb, tr, d)


def _out_proj_call(x3, mod3, ys, yt, yc, wo):
    nb, r, d = x3.shape
    ws, wa, wc = ys.shape[1], yt.shape[1], yc.shape[1]
    bb, tr = _tok_blocks(nb, r, TOK_TILE)
    tmm = bb * tr
    assert tmm == yt.shape[2]
    nt = r // tr
    tok3 = lambda b, t: (b, t, 0)
    return pl.pallas_call(
        _out_proj_kernel,
        grid=(nb // bb, nt),
        in_specs=[pl.BlockSpec((bb, tr, d), tok3),
                  pl.BlockSpec((bb, 1, d), lambda b, t: (b, 0, 2)),
                  pl.BlockSpec((tmm, ws), lambda b, t: (b * nt + t, 0)),
                  pl.BlockSpec((None, wa, tmm), lambda b, t: (b * nt + t, 0, 0)),
                  pl.BlockSpec((tmm, wc), lambda b, t: (b * nt + t, 0)),
                  _const_spec(wo.shape)],
        out_specs=pl.BlockSpec((bb, tr, d), tok3),
        out_shape=jax.ShapeDtypeStruct((nb, r, d), F32),
        compiler_params=_params("arbitrary", "arbitrary"),
    )(x3, mod3, ys, yt, yc, wo)


FFN_TILE = 512


NORM_ROWS = 64


def _ffn_kernel(x_ref, shift_ref, scale_ref, gate_ref, w1_ref, w2_ref, o_ref, h_s, r_s):
    bb, tr, d = x_ref.shape
    j = pl.program_id(2)

    @pl.when(j == 0)
    def _():
        per = tr // NORM_ROWS

        def chunk(c):
            return c // per, pl.ds(pl.multiple_of((c % per) * NORM_ROWS, NORM_ROWS), NORM_ROWS)

        def scales(c, carry):
            b, rows = chunk(c)
            x = x_ref[b, rows, :]
            r = lax.rsqrt(jnp.mean(x * x, axis=-1, keepdims=True) + EPS)
            r_s[pl.ds(pl.multiple_of(c * NORM_ROWS, NORM_ROWS), NORM_ROWS), :] = jnp.broadcast_to(r, (NORM_ROWS, LANES))
            return carry

        def rows_out(c, carry):
            b, rows = chunk(c)
            flat = pl.ds(pl.multiple_of(c * NORM_ROWS, NORM_ROWS), NORM_ROWS)
            r = jnp.concatenate([r_s[flat, :]] * (d // LANES), axis=1)
            h = x_ref[b, rows, :] * r * (1.0 + scale_ref[b]) + shift_ref[b]
            h_s[flat, :] = h.astype(BF16)
            return carry

        n = bb * per
        lax.fori_loop(0, n, scales, 0, unroll=4)
        lax.fori_loop(0, n, rows_out, 0)

    def partial_out():
        a = jnp.maximum(jnp.dot(h_s[...], w1_ref[...], preferred_element_type=F32), 0.0)
        return jnp.dot((a * a).astype(BF16), w2_ref[...], preferred_element_type=F32).reshape(bb, tr, d)

    @pl.when(j == 0)
    def _():
        o_ref[...] = partial_out()

    @pl.when(j > 0)
    def _():
        o_ref[...] += partial_out()

    @pl.when(j == pl.num_programs(2) - 1)
    def _():
        o_ref[...] = x_ref[...] + gate_ref[...] * o_ref[...]


def _ffn_call(x3, mod3, w1, w2, tm):
    nb, r, d = x3.shape
    tf = FFN_TILE
    nj = w1.shape[1] // tf
    bb, tr = _tok_blocks(nb, r, tm)
    tok3 = lambda b, t, j: (b, t, 0)
    mod_spec = lambda c: pl.BlockSpec((bb, 1, d), lambda b, t, j: (b, 0, c))
    return pl.pallas_call(
        _ffn_kernel,
        grid=(nb // bb, r // tr, nj),
        in_specs=[pl.BlockSpec((bb, tr, d), tok3), mod_spec(3), mod_spec(4), mod_spec(5),
                  pl.BlockSpec((d, tf), lambda b, t, j: (0, j)),
                  pl.BlockSpec((tf, d), lambda b, t, j: (j, 0))],
        out_specs=pl.BlockSpec((bb, tr, d), tok3),
        out_shape=jax.ShapeDtypeStruct((nb, r, d), F32),
        scratch_shapes=[pltpu.VMEM((bb * tr, d), BF16), pltpu.VMEM((bb * tr, LANES), F32)],
        compiler_params=_params("arbitrary", "arbitrary", "arbitrary"),
    )(x3, mod3, mod3, mod3, w1, w2)


def _layer(x3, mod3, lw, bias, sample, state):
    nb, r, d = x3.shape
    widths = lw["widths"]
    w_conv = widths[1]
    ns = lw["abar"][0].shape[0]
    scan = _scan_consts(lw["abar"], SSM_TILE // SUBLANES, SUBLANES // max(SSM_TILE // r, 1))
    if sample:
        cache_k, cache_v, h0r, h0i, conv_buf = state
        h0r = h0r.reshape(nb, ns)
        h0i = h0i.reshape(nb, ns)
        zinit = jnp.pad(conv_buf, ((0, 0), (SUBLANES - (CONV_K - 1), 0), (0, 0)))
    else:
        h0r = h0i = jnp.zeros((nb, ns), F32)
        zinit = jnp.zeros((nb, SUBLANES, w_conv), F32)
    u, yc, ztail, k, v, qt, vt = _in_proj_call(x3, mod3, zinit, lw["wrm"], lw["wt"], lw["kg2"], lw["cw"],
                                               lw["ogc"], lw["gq"], widths)
    ys, hre, him = _ssm_call(u, h0r, h0i, lw["fb"], scan, lw["cb"], lw["dskip"], lw["wglu"], lw["ogs"], r)

    kvw = k.shape[1]
    if sample:
        n_buf = cache_k.shape[1]
        kc = cache_k.reshape(nb * n_buf, kvw)
        vtc = cache_v.reshape(nb * n_buf, kvw).T.astype(vt.dtype)
    else:
        kc, vtc = k, vt
    yt = _attn_call(qt, k, vt, kc, vtc, bias, lw["sink"], lw["oga"], sample, r)

    x3 = _out_proj_call(x3, mod3, ys, yt, yc, lw["wo"])
    x3 = _ffn_call(x3, mod3, lw["w1"], lw["w2"], tm=1024)

    keep = min(r, WINDOW)
    k4 = k.reshape(nb, r, kvw)[:, r - keep:].reshape(nb, keep, N_KV_HEADS, HEAD_DIM)
    v4 = v.reshape(nb, r, kvw)[:, r - keep:].reshape(nb, keep, N_KV_HEADS, HEAD_DIM)
    if sample:
        new_k = jnp.concatenate([cache_k, k4], axis=1)[:, -n_buf:]
        new_v = jnp.concatenate([cache_v, v4], axis=1)[:, -n_buf:]
    else:
        new_k, new_v = k4, v4
    g = ns // SSM_STATE
    return (x3, new_k, new_v, hre.reshape(nb, g, SSM_STATE), him.reshape(nb, g, SSM_STATE),
            ztail[:, -(CONV_K - 1):])


def _layer_weights(l, w_in, ssm_a_re, ssm_a_im, ssm_log_dt, ssm_b_re, ssm_b_im, ssm_c_re, ssm_c_im, ssm_d,
                   ssm_w_glu, q_norm_g, k_norm_g, attn_sinks, conv_w, out_norm_g, w_out, w_ff1, w_ff2):
    w_ssm = ssm_d.shape[1]
    n_heads = attn_sinks.shape[1]
    w_attn = n_heads * HEAD_DIM
    w_kv = N_KV_HEADS * HEAD_DIM
    w_conv = conv_w.shape[1]
    qpg = n_heads // N_KV_HEADS
    wi = w_in[l]
    o = np.cumsum([0, w_ssm, w_attn, w_kv, w_kv, w_conv, w_conv, w_conv])
    wu, wq, wk, wv, wgb, wgc, wxc = (wi[:, o[i]:o[i + 1]] for i in range(7))
    fb, abar, cb = _ssm_consts(ssm_a_re[l], ssm_a_im[l], ssm_log_dt[l], ssm_b_re[l], ssm_b_im[l],
                               ssm_c_re[l], ssm_c_im[l])
    og = out_norm_g[l].astype(F32)
    d_model, d_ff = w_ff1.shape[1:]
    return {
        "widths": (w_ssm, w_conv, w_kv, w_attn),
        "wrm": jnp.concatenate([wgb, wgc, wxc, wu, wk, wv], axis=1).astype(BF16),
        "wt": jnp.concatenate([wq, wv], axis=1).T.astype(BF16),
        "kg2": jnp.tile(k_norm_g[l].astype(F32), N_KV_HEADS)[None, :],
        "fb": fb, "abar": abar, "cb": cb,
        "dskip": ssm_d[l].astype(F32)[None, :],
        "wglu": ssm_w_glu[l].astype(BF16),
        "gq": jnp.broadcast_to((q_norm_g[l].astype(F32) * (HEAD_DIM ** -0.5 * LOG2E))[:, None], (HEAD_DIM, TOK_TILE)),
        "sink": jnp.broadcast_to((attn_sinks[l].astype(F32) * LOG2E).reshape(N_KV_HEADS, 1, qpg, 1),
                                 (N_KV_HEADS, 1, qpg, PAIR)).reshape(N_KV_HEADS, 1, qpg * PAIR),
        "cw": conv_w[l].astype(F32).T,
        "ogs": og[None, :w_ssm],
        "oga": jnp.broadcast_to(og[w_ssm:w_ssm + w_attn].reshape(N_KV_HEADS, qpg, HEAD_DIM, 1).transpose(0, 2, 1, 3),
                                (N_KV_HEADS, HEAD_DIM, qpg, PAIR)).reshape(N_KV_HEADS, HEAD_DIM, qpg * PAIR),
        "ogc": og[None, w_ssm + w_attn:],
        "wo": w_out[l].astype(BF16),
        "w1": w_ff1[l].astype(BF16),
        "w2": w_ff2[l].astype(BF16),
    }


def kernel(x_prompt, x_sample, cache_k, cache_v, state_ssm_re, state_ssm_im, state_conv, c_prompt, c_sample, rel_bias, w_ada, b_ada, w_in, ssm_a_re, ssm_a_im, ssm_log_dt, ssm_b_re, ssm_b_im, ssm_c_re, ssm_c_im, ssm_d, ssm_w_glu, q_norm_g, k_norm_g, attn_sinks, conv_w, out_norm_g, w_out, w_ff1, w_ff2):
    depth = w_in.shape[0]
    nbp, nbs = x_prompt.shape[0], x_sample.shape[0]
    assert x_sample.shape[1] == CHUNK and cache_k.shape[2] == WINDOW and nbs % 2 == 0

    nc = nbp + nbs
    ncp = -(-nc // SUBLANES) * SUBLANES
    c_all = jnp.pad(jnp.concatenate([c_prompt, c_sample], axis=0), ((0, ncp - nc), (0, 0)))
    mod = _mod_call(c_all, w_ada, b_ada)

    bias_p = _bias_call(rel_bias, sample=False)
    bias_s = _bias_call(rel_bias, sample=True)

    xp, xs = x_prompt, x_sample
    outs_p, outs_s = [], []
    for l in range(depth):
        lw = _layer_weights(l, w_in, ssm_a_re, ssm_a_im, ssm_log_dt, ssm_b_re, ssm_b_im, ssm_c_re, ssm_c_im,
                            ssm_d, ssm_w_glu, q_norm_g, k_norm_g, attn_sinks, conv_w, out_norm_g, w_out,
                            w_ff1, w_ff2)
        mod_p = mod[l, :nbp][:, None, :]
        mod_s = mod[l, nbp:nc][:, None, :]
        xp, *rest_p = _layer(xp, mod_p, lw, bias_p, False, None)
        xs, *rest_s = _layer(xs, mod_s, lw, bias_s, True,
                             (cache_k[l], cache_v[l], state_ssm_re[l], state_ssm_im[l], state_conv[l]))
        outs_p.append(rest_p)
        outs_s.append(rest_s)
    stack = lambda outs, i: jnp.stack([o[i] for o in outs])
    return (xp, xs,
            *(stack(outs_p, i) for i in range(5)),
            *(stack(outs_s, i) for i in range(5)))
```

```python
import functools
import math

import numpy as np
import jax
import jax.numpy as jnp
from jax import lax
from jax.experimental import pallas as pl
from jax.experimental.pallas import tpu as pltpu

F32 = jnp.float32
BF16 = jnp.bfloat16

CHUNK = 64
SSM_GROUP = 16
SSM_STATE = 64
HEAD_DIM = 64
N_KV_HEADS = 2
WINDOW = 128
CONV_K = 3
REL_BUCKETS = 32
REL_MAX_DIST = 64
EPS = 1e-6
NEG_INF = -1e30
LOG2E = math.log2(math.e)

LANES = 128
SUBLANES = 8
PAIR = 2 * CHUNK
VMEM_LIMIT_BYTES = 60 * 1024 * 1024


def _params(*sem):
    return pltpu.CompilerParams(dimension_semantics=sem, vmem_limit_bytes=VMEM_LIMIT_BYTES)


def _const_spec(shape):
    nd = len(shape)
    return pl.BlockSpec(shape, lambda *_: (0,) * nd, pipeline_mode=pl.Buffered(1))


def _tok_blocks(nb, r, tm):
    if r >= tm:
        assert r % tm == 0
        return 1, tm
    bb = min(tm // r, nb)
    assert nb % bb == 0
    return bb, r


def _mod_kernel(c_ref, w_ref, b_ref, o_ref):
    c = c_ref[...]
    s = (c * jax.nn.sigmoid(c)).astype(BF16)
    o_ref[...] = jnp.dot(s, w_ref[...].astype(BF16), preferred_element_type=F32) + b_ref[...]


def _mod_call(c_all, w_ada, b_ada):
    depth, d, n = w_ada.shape
    nc = c_all.shape[0]
    tn = 1024
    return pl.pallas_call(
        _mod_kernel,
        grid=(depth, n // tn),
        in_specs=[pl.BlockSpec((nc, d), lambda l, j: (0, 0)),
                  pl.BlockSpec((None, d, tn), lambda l, j: (l, 0, j)),
                  pl.BlockSpec((None, 1, tn), lambda l, j: (l, 0, j))],
        out_specs=pl.BlockSpec((None, nc, tn), lambda l, j: (l, 0, j)),
        out_shape=jax.ShapeDtypeStruct((depth, nc, n), F32),
        compiler_params=_params("arbitrary", "arbitrary"),
    )(c_all, w_ada, b_ada.reshape(depth, 1, n))


def _bucket_maps(sample):
    lane = np.arange(PAIR)[None, :]
    if sample:
        nk = 2 * WINDOW + PAIR
        row = np.arange(nk)[:, None]
        own = row >= 2 * WINDOW
        key_b = np.where(own, (row - 2 * WINDOW) // CHUNK, row // WINDOW)
        key_s = np.where(own, WINDOW + (row - 2 * WINDOW) % CHUNK, row % WINDOW)
        rel = key_s - WINDOW - lane % CHUNK
        visible = key_b == lane // CHUNK
    else:
        nk = WINDOW + PAIR
        row = np.arange(nk)[:, None]
        rel = row - WINDOW - lane
        dc = row // CHUNK - lane // CHUNK
        visible = (dc >= 0) & (dc <= WINDOW // CHUNK)
    half = REL_BUCKETS // 2
    exact = half // 2
    n = np.abs(rel)
    nf = np.maximum(n, 1).astype(np.float32)
    far = exact + (np.log(nf / np.float32(exact)) / np.float32(math.log(REL_MAX_DIST / exact))
                   * np.float32(half - exact)).astype(np.int32)
    far = np.minimum(far, half - 1)
    bucket = np.where(rel > 0, half, 0) + np.where(n < exact, n, far)
    full = np.where(visible, bucket, -1).astype(np.int32)
    first = np.where(row >= WINDOW, full, -1).astype(np.int32)
    return np.stack([full, first])


def _bias_kernel(table_ref, bucket_ref, o_ref):
    h = pl.program_id(1)
    bucket = bucket_ref[...]
    acc = jnp.full(bucket.shape, NEG_INF, F32)
    for b in range(REL_BUCKETS):
        acc = jnp.where(bucket == b, table_ref[b, h] * LOG2E, acc)
    o_ref[...] = acc


def _bias_call(rel_bias, sample):
    buckets = jnp.asarray(_bucket_maps(sample))
    _, nk, _ = buckets.shape
    n_heads = rel_bias.shape[1]
    qpg = n_heads // N_KV_HEADS
    return pl.pallas_call(
        _bias_kernel,
        grid=(2, n_heads),
        in_specs=[pl.BlockSpec(memory_space=pltpu.SMEM),
                  pl.BlockSpec((None, nk, PAIR), lambda s, h: (s, 0, 0))],
        out_specs=pl.BlockSpec((None, None, nk, PAIR), lambda s, h: (s, h // qpg, 0, h % qpg)),
        out_shape=jax.ShapeDtypeStruct((2, N_KV_HEADS, nk, qpg * PAIR), F32),
        compiler_params=_params("arbitrary", "arbitrary"),
    )(rel_bias, buckets)


def _rms(y):
    return y * lax.rsqrt(jnp.mean(y * y, axis=-1, keepdims=True) + EPS)


def _in_proj_kernel(x_ref, shift_ref, scale_ref, zinit_ref, wrm_ref, wt_ref, kg_ref, cw_ref, ogc_ref, gq_ref,
                    u_ref, yc_ref, zt_ref, k_ref, v_ref, qt_ref, vt_ref, zprev, *, widths):
    bb, tr, d = x_ref.shape

    @pl.when(pl.program_id(1) == 0)
    def _():
        zprev[...] = zinit_ref[...]

    x = x_ref[...]
    ms = jnp.mean(x * x, axis=-1, keepdims=True)
    h = x * lax.rsqrt(ms + EPS) * (1.0 + scale_ref[...]) + shift_ref[...]
    h = h.reshape(bb * tr, d).astype(BF16)

    w_ssm, w_conv, w_kv, w_attn = widths

    pt = lax.dot_general(wt_ref[...], h, (((1,), (1,)), ((), ())), preferred_element_type=F32)
    gq = gq_ref[...]
    for hd in range(w_attn // HEAD_DIM):
        q = pt[hd * HEAD_DIM:(hd + 1) * HEAD_DIM]
        ms = jnp.sum(q * q, axis=0, keepdims=True) * (1.0 / HEAD_DIM)
        qt_ref[hd * HEAD_DIM:(hd + 1) * HEAD_DIM, :] = (q * lax.rsqrt(ms + EPS) * gq).astype(qt_ref.dtype)
    vt_ref[...] = pt[w_attn:].astype(vt_ref.dtype)

    pc = jnp.dot(h, wrm_ref[:, 0:3 * w_conv], preferred_element_type=F32)
    gb = pc[:, 0:w_conv].reshape(bb, tr, w_conv)
    z = (pc[:, w_conv:2 * w_conv] * pc[:, 2 * w_conv:3 * w_conv]).reshape(bb, tr, w_conv)

    zp = jnp.concatenate([zprev[...], z], axis=1)
    ztail = z[:, tr - SUBLANES:, :]
    zprev[...] = ztail
    zt_ref[...] = ztail
    cw = cw_ref[...]

    def conv_part(c, n):
        if bb == 1:
            rc = tr // n
            zs, gs = zp[:, c * rc:c * rc + rc + SUBLANES], gb[:, c * rc:(c + 1) * rc]
        else:
            bc, rc = bb // n, tr
            zs, gs = zp[c * bc:(c + 1) * bc], gb[c * bc:(c + 1) * bc]
        conv = (zs[:, SUBLANES - 2:SUBLANES - 2 + rc] * cw[0:1] + zs[:, SUBLANES - 1:SUBLANES - 1 + rc] * cw[1:2]
                + zs[:, SUBLANES:] * cw[2:3])
        y = (_rms(gs * conv) * ogc_ref[...]).astype(yc_ref.dtype)
        rows = y.shape[0] * y.shape[1]
        yc_ref[c * rows:(c + 1) * rows, :] = y.reshape(rows, w_conv)

    n_part = 4
    assert (tr if bb == 1 else bb) % n_part == 0
    col = 3 * w_conv
    for c, width in enumerate((w_ssm // 2, w_ssm // 2, 2 * w_kv)):
        conv_part(c, n_part)
        pp = jnp.dot(h, wrm_ref[:, col:col + width], preferred_element_type=F32)
        col += width
        if c < 2:
            u_ref[:, c * width:(c + 1) * width] = pp
        else:
            k = pp[:, :w_kv]
            v_ref[...] = pp[:, w_kv:]
    conv_part(n_part - 1, n_part)

    lo = lax.broadcasted_iota(jnp.int32, k.shape, 1) < HEAD_DIM
    k2 = k * k
    s_lo = jnp.sum(jnp.where(lo, k2, 0.0), axis=-1, keepdims=True)
    s_hi = jnp.sum(jnp.where(lo, 0.0, k2), axis=-1, keepdims=True)
    ssq = jnp.where(lo, s_lo, s_hi)
    k_ref[...] = k * lax.rsqrt(ssq * (1.0 / HEAD_DIM) + EPS) * kg_ref[...]


TOK_TILE = 512


def _in_proj_call(x3, mod3, zinit, wrm, wt, kg2, cw, ogc, gq, widths):
    nb, r, d = x3.shape
    w_ssm, w_conv, w_kv, w_attn = widths
    bb, tr = _tok_blocks(nb, r, TOK_TILE)
    tmm = bb * tr
    assert tmm == TOK_TILE
    nt = r // tr
    ttot = nb * r
    tok = lambda b, t: (b * nt + t, 0)
    tok_t = lambda b, t: (b * nt + t, 0, 0)
    halo_spec = pl.BlockSpec((bb, SUBLANES, w_conv), lambda b, t: (b, 0, 0))
    out_shape = [jax.ShapeDtypeStruct((ttot, w_ssm), F32), jax.ShapeDtypeStruct((ttot, w_conv), BF16),
                 jax.ShapeDtypeStruct((nb, SUBLANES, w_conv), F32),
                 jax.ShapeDtypeStruct((ttot, w_kv), F32), jax.ShapeDtypeStruct((ttot, w_kv), F32),
                 jax.ShapeDtypeStruct((ttot // tmm, w_attn, tmm), BF16),
                 jax.ShapeDtypeStruct((ttot // tmm, w_kv, tmm), BF16)]
    out_specs = [pl.BlockSpec((tmm, w_ssm), tok), pl.BlockSpec((tmm, w_conv), tok), halo_spec,
                 pl.BlockSpec((tmm, w_kv), tok), pl.BlockSpec((tmm, w_kv), tok),
                 pl.BlockSpec((None, w_attn, tmm), tok_t), pl.BlockSpec((None, w_kv, tmm), tok_t)]
    consts = (wrm, wt, kg2, cw, ogc, gq)
    return pl.pallas_call(
        functools.partial(_in_proj_kernel, widths=widths),
        grid=(nb // bb, nt),
        in_specs=[pl.BlockSpec((bb, tr, d), lambda b, t: (b, t, 0)),
                  pl.BlockSpec((bb, 1, d), lambda b, t: (b, 0, 0)),
                  pl.BlockSpec((bb, 1, d), lambda b, t: (b, 0, 1)),
                  halo_spec] + [_const_spec(c.shape) for c in consts],
        out_specs=out_specs,
        out_shape=out_shape,
        scratch_shapes=[pltpu.VMEM((bb, SUBLANES, w_conv), F32)],
        compiler_params=_params("arbitrary", "arbitrary"),
    )(x3, mod3, mod3, zinit, *consts)


SCAN_LANE_BLOCKS = 8


def _gelu_tanh(x):
    return 0.5 * x * (1.0 + jnp.tanh(math.sqrt(2.0 / math.pi) * (x + 0.044715 * (x * x * x))))


def _segment_perm(ts):
    lseg = ts // SUBLANES
    p = np.zeros((ts, ts), np.float32)
    i, j = np.meshgrid(np.arange(lseg), np.arange(SUBLANES), indexing="ij")
    p[(i * SUBLANES + j).ravel(), (j * lseg + i).ravel()] = 1.0
    return p


def _ssm_kernel(u_ref, h0r_ref, h0i_ref, perm_ref, permt_ref, fb_ref, ab_ref, sg_ref, pw_ref, cb_ref, dskip_ref,
                wglu_ref, og_ref, y_ref, hre_ref, him_ref, bre, bim, car, *, segs_per_seq):
    ts, w = u_ref.shape
    ns = bre.shape[1]
    lseg = ts // SUBLANES
    half_u = w // 2
    half_s = ns // 2
    t = pl.program_id(1)

    @pl.when(t == 0)
    def _():
        car[0:SUBLANES, :] = h0r_ref[...]
        car[SUBLANES:, :] = h0i_ref[...]

    u = u_ref[...]
    up = jnp.dot(perm_ref[...], u.astype(BF16), preferred_element_type=F32).astype(BF16)
    for j in range(2):
        bb = jnp.dot(up[:, j * half_u:(j + 1) * half_u], fb_ref[j], preferred_element_type=F32)
        bre[:, j * half_s:(j + 1) * half_s] = bb[:, :half_s]
        bim[:, j * half_s:(j + 1) * half_s] = bb[:, half_s:]

    seq_start = lax.broadcasted_iota(jnp.int32, (SUBLANES, LANES), 0) % segs_per_seq == 0
    zero = jnp.zeros((SUBLANES, LANES), F32)
    nblk = SCAN_LANE_BLOCKS
    for c0 in range(0, ns // LANES, nblk):
        sls = [pl.ds((c0 + i) * LANES, LANES) for i in range(nblk)]
        ab = [(ab_ref[0, :, sl], ab_ref[1, :, sl]) for sl in sls]

        def pass1(i, carry, sls=sls, ab=ab):
            rows = pl.ds(pl.multiple_of(i * SUBLANES, SUBLANES), SUBLANES)
            new = []
            for sl, (ar, ai), (hr, hi) in zip(sls, ab, carry):
                hr, hi = ar * hr - ai * hi + bre[rows, sl], ar * hi + ai * hr + bim[rows, sl]
                bre[rows, sl] = hr
                bim[rows, sl] = hi
                new.append((hr, hi))
            return tuple(new)

        ends = lax.fori_loop(0, lseg, pass1, tuple((zero, zero) for _ in sls), unroll=True)

        starts = []
        for sl, (er, ei) in zip(sls, ends):
            xr = jnp.where(seq_start, car[0:SUBLANES, sl], pltpu.roll(er, 1, axis=0))
            xi = jnp.where(seq_start, car[SUBLANES:, sl], pltpu.roll(ei, 1, axis=0))
            for k in range(segs_per_seq.bit_length() - 1):
                sr = pltpu.roll(xr, 1 << k, axis=0)
                si = pltpu.roll(xi, 1 << k, axis=0)
                mr, mi = sg_ref[2 * k, :, sl], sg_ref[2 * k + 1, :, sl]
                xr, xi = xr + (mr * sr - mi * si), xi + (mr * si + mi * sr)
            lr, li = sg_ref[6, :, sl], sg_ref[7, :, sl]
            nr = lr * xr - li * xi + er
            ni = lr * xi + li * xr + ei
            hre_ref[:, sl] = nr
            him_ref[:, sl] = ni
            car[0:SUBLANES, sl] = jnp.broadcast_to(nr[SUBLANES - 1:SUBLANES, :], (SUBLANES, LANES))
            car[SUBLANES:, sl] = jnp.broadcast_to(ni[SUBLANES - 1:SUBLANES, :], (SUBLANES, LANES))
            starts.append((xr, xi))

        def pass2(i, c, sls=sls, starts=starts):
            rows = pl.ds(pl.multiple_of(i * SUBLANES, SUBLANES), SUBLANES)
            for sl, (sr, si) in zip(sls, starts):
                pr = pw_ref[0, rows, sl]
                pi = pw_ref[1, rows, sl]
                bre[rows, sl] += pr * sr - pi * si
                bim[rows, sl] += pr * si + pi * sr
            return c

        lax.fori_loop(0, lseg, pass2, 0, unroll=True)

    ys = []
    for j in range(2):
        hcat = jnp.concatenate([bre[:, j * half_s:(j + 1) * half_s].astype(BF16),
                                bim[:, j * half_s:(j + 1) * half_s].astype(BF16)], axis=-1)
        ys.append(jnp.dot(hcat, cb_ref[j], preferred_element_type=F32))
    yp = jnp.concatenate(ys, axis=-1)
    p0 = yp.astype(BF16)
    r1 = yp - p0.astype(F32)
    p1 = r1.astype(BF16)
    p2 = (r1 - p1.astype(F32)).astype(BF16)
    parts = jnp.dot(permt_ref[...], jnp.concatenate([p0, p1, p2], axis=-1), preferred_element_type=F32)
    y = (parts[:, :w] + parts[:, w:2 * w]) + parts[:, 2 * w:] + dskip_ref[...] * u
    y = _gelu_tanh(y)
    y = y * jax.nn.sigmoid(jnp.dot(y.astype(BF16), wglu_ref[...], preferred_element_type=F32))
    y_ref[...] = (_rms(y) * og_ref[...]).astype(y_ref.dtype)


SSM_TILE = 512


def _ssm_call(u, h0r, h0i, fb, scan, cb, dskip, wglu, ogs, r):
    ttot, w = u.shape
    ab, sg, pw = scan
    ns = ab.shape[-1]
    ts = SSM_TILE
    lseg = ts // SUBLANES
    assert pw.shape[1] == ts
    nseq = ttot // r
    if r >= ts:
        assert r % ts == 0
        seq_per_tile, nt = 1, r // ts
    else:
        assert r % lseg == 0 and ts % r == 0 and nseq % (ts // r) == 0
        seq_per_tile, nt = ts // r, 1
    segs_per_seq = SUBLANES // seq_per_tile
    ngrp = nseq // seq_per_tile
    init = lambda h: jnp.repeat(h.reshape(ngrp, seq_per_tile, ns), segs_per_seq, axis=1)
    perm = _segment_perm(ts)
    perm, permt = jnp.asarray(perm, BF16), jnp.asarray(perm.T, BF16)
    st_spec = pl.BlockSpec((None, SUBLANES, ns), lambda b, t: (b, 0, 0))
    consts = (perm, permt, fb, ab, sg, pw, cb, dskip, wglu, ogs)
    y, hre, him = pl.pallas_call(
        functools.partial(_ssm_kernel, segs_per_seq=segs_per_seq),
        grid=(ngrp, nt),
        in_specs=[pl.BlockSpec((ts, w), lambda b, t: (b * nt + t, 0)), st_spec, st_spec]
        + [_const_spec(c.shape) for c in consts],
        out_specs=[pl.BlockSpec((ts, w), lambda b, t: (b * nt + t, 0)), st_spec, st_spec],
        out_shape=[jax.ShapeDtypeStruct((ttot, w), BF16),
                   jax.ShapeDtypeStruct((ngrp, SUBLANES, ns), F32), jax.ShapeDtypeStruct((ngrp, SUBLANES, ns), F32)],
        scratch_shapes=[pltpu.VMEM((ts, ns), F32), pltpu.VMEM((ts, ns), F32), pltpu.VMEM((2 * SUBLANES, ns), F32)],
        compiler_params=_params("arbitrary", "arbitrary"),
    )(u, init(h0r), init(h0i), *consts)
    last = lambda h: h[:, segs_per_seq - 1::segs_per_seq].reshape(nseq, ns)
    return y, last(hre), last(him)


def _scan_consts(a1, lseg, segs_per_seq):
    def cmul(x, y):
        return x[0] * y[0] - x[1] * y[1], x[0] * y[1] + x[1] * y[0]

    tr, ti = a1[0][None], a1[1][None]
    while tr.shape[0] < lseg:
        nr, ni = cmul((tr, ti), (tr[-1], ti[-1]))
        tr, ti = jnp.concatenate([tr, nr]), jnp.concatenate([ti, ni])
    pw = jnp.repeat(jnp.stack([tr[:lseg], ti[:lseg]]), SUBLANES, axis=1)
    ns = a1[0].shape[0]
    ab = jnp.stack([jnp.broadcast_to(a1[0], (SUBLANES, ns)), jnp.broadcast_to(a1[1], (SUBLANES, ns))])
    al = (tr[lseg - 1], ti[lseg - 1])
    al2 = cmul(al, al)
    al4 = cmul(al2, al2)
    row = jnp.arange(SUBLANES)[:, None]
    rows = []
    for k, a in enumerate((al, al2, al4)):
        keep = row % segs_per_seq >= (1 << k)
        rows += [jnp.where(keep, a[0][None, :], 0.0), jnp.where(keep, a[1][None, :], 0.0)]
    rows += [jnp.broadcast_to(al[0], (SUBLANES, ns)), jnp.broadcast_to(al[1], (SUBLANES, ns))]
    return ab, jnp.stack(rows), pw


def _ssm_consts(a_re, a_im, log_dt, b_re, b_im, c_re, c_im):
    g, p = a_re.shape
    hh = b_re.shape[-1]
    ar, ai = a_re.astype(F32), a_im.astype(F32)
    dt = jnp.exp(log_dt.astype(F32))[:, None]
    mag = jnp.exp(dt * ar)
    abar_re, abar_im = mag * jnp.cos(dt * ai), mag * jnp.sin(dt * ai)
    den = ar * ar + ai * ai
    f_re = ((abar_re - 1.0) * ar + abar_im * ai) / den
    f_im = (abar_im * ar - (abar_re - 1.0) * ai) / den
    fb_re = f_re[..., None] * b_re - f_im[..., None] * b_im
    fb_im = f_re[..., None] * b_im + f_im[..., None] * b_re
    gh = g // 2
    eye = jnp.eye(gh, dtype=F32)

    def in_blk(m):
        return jnp.einsum('gph,gk->ghkp', m, eye).reshape(gh * hh, gh * p)

    def out_blk(m):
        return jnp.einsum('ghp,gk->kpgh', m, eye).reshape(gh * p, gh * hh)

    fb = jnp.stack([jnp.concatenate([in_blk(fb_re[j * gh:(j + 1) * gh]), in_blk(fb_im[j * gh:(j + 1) * gh])], axis=1)
                    for j in range(2)]).astype(BF16)
    cb = jnp.stack([jnp.concatenate([out_blk(c_re[j * gh:(j + 1) * gh]), -out_blk(c_im[j * gh:(j + 1) * gh])], axis=0)
                    for j in range(2)]).astype(BF16)

    return fb, (abar_re.reshape(-1), abar_im.reshape(-1)), cb


def _attn_kernel(qt_ref, k_ref, vt_ref, kc_ref, vtc_ref, bias_ref, sink_ref, og_ref, yt_ref,
                 *, sample, pairs_per_seq):
    n_rows, tq = qt_ref.shape
    npairs = tq // PAIR
    qpg = n_rows // (N_KV_HEADS * HEAD_DIM)
    i = pl.program_id(0)
    if sample:
        k_all, vt_all = k_ref[...], vt_ref[...]
    else:
        k_all = jnp.concatenate([kc_ref[...], k_ref[...]], axis=0)
        vt_all = jnp.concatenate([vtc_ref[...], vt_ref[...]], axis=1)
    for p in range(npairs):
        if sample:
            kb = jnp.concatenate([kc_ref[p * 2 * WINDOW:(p + 1) * 2 * WINDOW, :],
                                  k_all[p * PAIR:(p + 1) * PAIR, :]], axis=0)
            vtb = jnp.concatenate([vtc_ref[:, p * 2 * WINDOW:(p + 1) * 2 * WINDOW],
                                   vt_all[:, p * PAIR:(p + 1) * PAIR]], axis=1)
            sel = 0
        else:
            kb = k_all[p * PAIR:p * PAIR + WINDOW + PAIR, :]
            vtb = vt_all[:, p * PAIR:p * PAIR + WINDOW + PAIR]
            sel = ((i * npairs + p) % pairs_per_seq == 0).astype(jnp.int32)
        kb = kb.astype(BF16)
        vtb = vtb.astype(BF16)
        ones = jnp.ones((2 * SUBLANES, vtb.shape[1]), BF16)
        outs = []
        ssq = jnp.zeros((1, PAIR), F32)
        for g in range(N_KV_HEADS):
            qg = jnp.concatenate([qt_ref[(g * qpg + hh) * HEAD_DIM:(g * qpg + hh + 1) * HEAD_DIM,
                                         p * PAIR:(p + 1) * PAIR] for hh in range(qpg)], axis=1)
            zero = jnp.zeros_like(qg)
            qpad = jnp.concatenate([qg, zero] if g == 0 else [zero, qg], axis=0)
            s = jnp.dot(kb, qpad, preferred_element_type=F32) + bias_ref[sel, g]
            sink = sink_ref[g]
            m = jnp.maximum(jnp.max(s, axis=0, keepdims=True), sink)
            e = jnp.exp2(s - m).astype(BF16)
            va = jnp.concatenate([vtb[g * HEAD_DIM:(g + 1) * HEAD_DIM, :], ones], axis=0)
            oa = jnp.dot(va, e, preferred_element_type=F32)
            den = oa[HEAD_DIM:HEAD_DIM + 1, :] + jnp.exp2(sink - m)
            o = oa[:HEAD_DIM, :] / den
            outs.append(o)
            sq = jnp.sum(o * o, axis=0, keepdims=True)
            for hh in range(qpg):
                ssq = ssq + sq[:, hh * PAIR:(hh + 1) * PAIR]
        rn = lax.rsqrt(ssq * (1.0 / n_rows) + EPS)
        rn = jnp.concatenate([rn] * qpg, axis=1)
        for g in range(N_KV_HEADS):
            o = outs[g] * rn * og_ref[g]
            for hh in range(qpg):
                r0 = (g * qpg + hh) * HEAD_DIM
                yt_ref[r0:r0 + HEAD_DIM, p * PAIR:(p + 1) * PAIR] = \
                    o[:, hh * PAIR:(hh + 1) * PAIR].astype(yt_ref.dtype)


def _attn_call(qt, k, vt, kc, vtc, bias, sink, oga, sample, seq_len):
    ntile, n_rows, tq = qt.shape
    kvw = k.shape[1]
    assert sample or seq_len % tq == 0
    npairs = tq // PAIR
    tile3 = lambda i: (i, 0, 0)
    if sample:
        ctx = 2 * WINDOW * npairs
        kc_spec = pl.BlockSpec((ctx, kvw), lambda i: (i, 0))
        vtc_spec = pl.BlockSpec((kvw, ctx), lambda i: (0, i))
    else:
        kc_spec = pl.BlockSpec((WINDOW, kvw), lambda i: (jnp.maximum(i * npairs - 1, 0), 0))
        vtc_spec = pl.BlockSpec((None, kvw, WINDOW), lambda i: (jnp.maximum(i - 1, 0), 0, tq // WINDOW - 1))
    return pl.pallas_call(
        functools.partial(_attn_kernel, sample=sample, pairs_per_seq=seq_len // PAIR),
        grid=(ntile,),
        in_specs=[pl.BlockSpec((None, n_rows, tq), tile3),
                  pl.BlockSpec((tq, kvw), lambda i: (i, 0)),
                  pl.BlockSpec((None, kvw, tq), tile3),
                  kc_spec, vtc_spec,
                  _const_spec(bias.shape), _const_spec(sink.shape),
                  _const_spec(oga.shape)],
        out_specs=pl.BlockSpec((None, n_rows, tq), tile3),
        out_shape=jax.ShapeDtypeStruct((ntile, n_rows, tq), BF16),
        compiler_params=_params("arbitrary"),
    )(qt, k, vt, kc, vtc, bias, sink, oga)


def _out_proj_kernel(x_ref, gate_ref, ys_ref, yt_ref, yc_ref, wo_ref, xo_ref):
    bb, tr, d = x_ref.shape
    ws, wa = ys_ref.shape[1], yt_ref.shape[0]
    o = jnp.dot(ys_ref[...], wo_ref[0:ws, :], preferred_element_type=F32)
    o += lax.dot_general(yt_ref[...], wo_ref[ws:ws + wa, :], (((0,), (0,)), ((), ())),
                         preferred_element_type=F32)
    o += jnp.dot(yc_ref[...], wo_ref[ws + wa:, :], preferred_element_type=F32)
    xo_ref[...] = x_ref[...] + gate_ref[...] * o.reshape(bb, tr, d)


def _out_proj_call(x3, mod3, ys, yt, yc, wo):
    nb, r, d = x3.shape
    ws, wa, wc = ys.shape[1], yt.shape[1], yc.shape[1]
    bb, tr = _tok_blocks(nb, r, TOK_TILE)
    tmm = bb * tr
    assert tmm == yt.shape[2]
    nt = r // tr
    tok3 = lambda b, t: (b, t, 0)
    return pl.pallas_call(
        _out_proj_kernel,
        grid=(nb // bb, nt),
        in_specs=[pl.BlockSpec((bb, tr, d), tok3),
                  pl.BlockSpec((bb, 1, d), lambda b, t: (b, 0, 2)),
                  pl.BlockSpec((tmm, ws), lambda b, t: (b * nt + t, 0)),
                  pl.BlockSpec((None, wa, tmm), lambda b, t: (b * nt + t, 0, 0)),
                  pl.BlockSpec((tmm, wc), lambda b, t: (b * nt + t, 0)),
                  _const_spec(wo.shape)],
        out_specs=pl.BlockSpec((bb, tr, d), tok3),
        out_shape=jax.ShapeDtypeStruct((nb, r, d), F32),
        compiler_params=_params("arbitrary", "arbitrary"),
    )(x3, mod3, ys, yt, yc, wo)


FFN_TILE = 1024
NORM_ROWS = 64


def _ffn_kernel(x_ref, shift_ref, scale_ref, gate_ref, w1_ref, w2_ref, o_ref, h_s, r_s):
    bb, tr, d = x_ref.shape
    j = pl.program_id(2)

    @pl.when(j == 0)
    def _():
        per = tr // NORM_ROWS

        def chunk(c):
            return c // per, pl.ds(pl.multiple_of((c % per) * NORM_ROWS, NORM_ROWS), NORM_ROWS)

        def scales(c, carry):
            b, rows = chunk(c)
            x = x_ref[b, rows, :]
            r = lax.rsqrt(jnp.mean(x * x, axis=-1, keepdims=True) + EPS)
            r_s[pl.ds(pl.multiple_of(c * NORM_ROWS, NORM_ROWS), NORM_ROWS), :] = jnp.broadcast_to(r, (NORM_ROWS, LANES))
            return carry

        def rows_out(c, carry):
            b, rows = chunk(c)
            flat = pl.ds(pl.multiple_of(c * NORM_ROWS, NORM_ROWS), NORM_ROWS)
            r = jnp.concatenate([r_s[flat, :]] * (d // LANES), axis=1)
            h = x_ref[b, rows, :] * r * (1.0 + scale_ref[b]) + shift_ref[b]
            h_s[flat, :] = h.astype(BF16)
            return carry

        n = bb * per
        lax.fori_loop(0, n, scales, 0, unroll=4)
        lax.fori_loop(0, n, rows_out, 0)

    def partial_out():
        a = jnp.maximum(jnp.dot(h_s[...], w1_ref[...], preferred_element_type=F32), 0.0)
        return jnp.dot((a * a).astype(BF16), w2_ref[...], preferred_element_type=F32).reshape(bb, tr, d)

    @pl.when(j == 0)
    def _():
        o_ref[...] = partial_out()

    @pl.when(j > 0)
    def _():
        o_ref[...] += partial_out()

    @pl.when(j == pl.num_programs(2) - 1)
    def _():
        o_ref[...] = x_ref[...] + gate_ref[...] * o_ref[...]


def _ffn_call(x3, mod3, w1, w2, tm):
    nb, r, d = x3.shape
    tf = FFN_TILE
    nj = w1.shape[1] // tf
    bb, tr = _tok_blocks(nb, r, tm)
    tok3 = lambda b, t, j: (b, t, 0)
    mod_spec = lambda c: pl.BlockSpec((bb, 1, d), lambda b, t, j: (b, 0, c))
    return pl.pallas_call(
        _ffn_kernel,
        grid=(nb // bb, r // tr, nj),
        in_specs=[pl.BlockSpec((bb, tr, d), tok3), mod_spec(3), mod_spec(4), mod_spec(5),
                  pl.BlockSpec((d, tf), lambda b, t, j: (0, j)),
                  pl.BlockSpec((tf, d), lambda b, t, j: (j, 0))],
        out_specs=pl.BlockSpec((bb, tr, d), tok3),
        out_shape=jax.ShapeDtypeStruct((nb, r, d), F32),
        scratch_shapes=[pltpu.VMEM((bb * tr, d), BF16), pltpu.VMEM((bb * tr, LANES), F32)],
        compiler_params=_params("arbitrary", "arbitrary", "arbitrary"),
    )(x3, mod3, mod3, mod3, w1, w2)


def _layer(x3, mod3, lw, bias, sample, state):
    nb, r, d = x3.shape
    widths = lw["widths"]
    w_conv = widths[1]
    ns = lw["abar"][0].shape[0]
    scan = _scan_consts(lw["abar"], SSM_TILE // SUBLANES, SUBLANES // max(SSM_TILE // r, 1))
    if sample:
        cache_k, cache_v, h0r, h0i, conv_buf = state
        h0r = h0r.reshape(nb, ns)
        h0i = h0i.reshape(nb, ns)
        zinit = jnp.pad(conv_buf, ((0, 0), (SUBLANES - (CONV_K - 1), 0), (0, 0)))
    else:
        h0r = h0i = jnp.zeros((nb, ns), F32)
        zinit = jnp.zeros((nb, SUBLANES, w_conv), F32)
    u, yc, ztail, k, v, qt, vt = _in_proj_call(x3, mod3, zinit, lw["wrm"], lw["wt"], lw["kg2"], lw["cw"],
                                               lw["ogc"], lw["gq"], widths)
    ys, hre, him = _ssm_call(u, h0r, h0i, lw["fb"], scan, lw["cb"], lw["dskip"], lw["wglu"], lw["ogs"], r)

    kvw = k.shape[1]
    if sample:
        n_buf = cache_k.shape[1]
        kc = cache_k.reshape(nb * n_buf, kvw)
        vtc = cache_v.reshape(nb * n_buf, kvw).T.astype(vt.dtype)
    else:
        kc, vtc = k, vt
    yt = _attn_call(qt, k, vt, kc, vtc, bias, lw["sink"], lw["oga"], sample, r)

    x3 = _out_proj_call(x3, mod3, ys, yt, yc, lw["wo"])
    x3 = _ffn_call(x3, mod3, lw["w1"], lw["w2"], tm=1024)

    keep = min(r, WINDOW)
    k4 = k.reshape(nb, r, kvw)[:, r - keep:].reshape(nb, keep, N_KV_HEADS, HEAD_DIM)
    v4 = v.reshape(nb, r, kvw)[:, r - keep:].reshape(nb, keep, N_KV_HEADS, HEAD_DIM)
    if sample:
        new_k = jnp.concatenate([cache_k, k4], axis=1)[:, -n_buf:]
        new_v = jnp.concatenate([cache_v, v4], axis=1)[:, -n_buf:]
    else:
        new_k, new_v = k4, v4
    g = ns // SSM_STATE
    return (x3, new_k, new_v, hre.reshape(nb, g, SSM_STATE), him.reshape(nb, g, SSM_STATE),
            ztail[:, -(CONV_K - 1):])


def _layer_weights(l, w_in, ssm_a_re, ssm_a_im, ssm_log_dt, ssm_b_re, ssm_b_im, ssm_c_re, ssm_c_im, ssm_d,
                   ssm_w_glu, q_norm_g, k_norm_g, attn_sinks, conv_w, out_norm_g, w_out, w_ff1, w_ff2):
    w_ssm = ssm_d.shape[1]
    n_heads = attn_sinks.shape[1]
    w_attn = n_heads * HEAD_DIM
    w_kv = N_KV_HEADS * HEAD_DIM
    w_conv = conv_w.shape[1]
    qpg = n_heads // N_KV_HEADS
    wi = w_in[l]
    o = np.cumsum([0, w_ssm, w_attn, w_kv, w_kv, w_conv, w_conv, w_conv])
    wu, wq, wk, wv, wgb, wgc, wxc = (wi[:, o[i]:o[i + 1]] for i in range(7))
    fb, abar, cb = _ssm_consts(ssm_a_re[l], ssm_a_im[l], ssm_log_dt[l], ssm_b_re[l], ssm_b_im[l],
                               ssm_c_re[l], ssm_c_im[l])
    og = out_norm_g[l].astype(F32)
    return {
        "widths": (w_ssm, w_conv, w_kv, w_attn),
        "wrm": jnp.concatenate([wgb, wgc, wxc, wu, wk, wv], axis=1).astype(BF16),
        "wt": jnp.concatenate([wq, wv], axis=1).T.astype(BF16),
        "kg2": jnp.tile(k_norm_g[l].astype(F32), N_KV_HEADS)[None, :],
        "fb": fb, "abar": abar, "cb": cb,
        "dskip": ssm_d[l].astype(F32)[None, :],
        "wglu": ssm_w_glu[l].astype(BF16),
        "gq": jnp.broadcast_to((q_norm_g[l].astype(F32) * (HEAD_DIM ** -0.5 * LOG2E))[:, None], (HEAD_DIM, TOK_TILE)),
        "sink": jnp.broadcast_to((attn_sinks[l].astype(F32) * LOG2E).reshape(N_KV_HEADS, 1, qpg, 1),
                                 (N_KV_HEADS, 1, qpg, PAIR)).reshape(N_KV_HEADS, 1, qpg * PAIR),
        "cw": conv_w[l].astype(F32).T,
        "ogs": og[None, :w_ssm],
        "oga": jnp.broadcast_to(og[w_ssm:w_ssm + w_attn].reshape(N_KV_HEADS, qpg, HEAD_DIM, 1).transpose(0, 2, 1, 3),
                                (N_KV_HEADS, HEAD_DIM, qpg, PAIR)).reshape(N_KV_HEADS, HEAD_DIM, qpg * PAIR),
        "ogc": og[None, w_ssm + w_attn:],
        "wo": w_out[l].astype(BF16),
        "w1": w_ff1[l].astype(BF16),
        "w2": w_ff2[l].astype(BF16),
    }


def kernel(x_prompt, x_sample, cache_k, cache_v, state_ssm_re, state_ssm_im, state_conv, c_prompt, c_sample, rel_bias, w_ada, b_ada, w_in, ssm_a_re, ssm_a_im, ssm_log_dt, ssm_b_re, ssm_b_im, ssm_c_re, ssm_c_im, ssm_d, ssm_w_glu, q_norm_g, k_norm_g, attn_sinks, conv_w, out_norm_g, w_out, w_ff1, w_ff2):
    depth = w_in.shape[0]
    nbp, nbs = x_prompt.shape[0], x_sample.shape[0]
    assert x_sample.shape[1] == CHUNK and cache_k.shape[2] == WINDOW and nbs % 2 == 0

    nc = nbp + nbs
    ncp = -(-nc // SUBLANES) * SUBLANES
    c_all = jnp.pad(jnp.concatenate([c_prompt, c_sample], axis=0), ((0, ncp - nc), (0, 0)))
    mod = _mod_call(c_all, w_ada, b_ada)

    bias_p = _bias_call(rel_bias, sample=False)
    bias_s = _bias_call(rel_bias, sample=True)

    xp, xs = x_prompt, x_sample
    outs_p, outs_s = [], []
    for l in range(depth):
        lw = _layer_weights(l, w_in, ssm_a_re, ssm_a_im, ssm_log_dt, ssm_b_re, ssm_b_im, ssm_c_re, ssm_c_im,
                            ssm_d, ssm_w_glu, q_norm_g, k_norm_g, attn_sinks, conv_w, out_norm_g, w_out,
                            w_ff1, w_ff2)
        mod_p = mod[l, :nbp][:, None, :]
        mod_s = mod[l, nbp:nc][:, None, :]
        xp, *rest_p = _layer(xp, mod_p, lw, bias_p, False, None)
        xs, *rest_s = _layer(xs, mod_s, lw, bias_s, True,
                             (cache_k[l], cache_v[l], state_ssm_re[l], state_ssm_im[l], state_conv[l]))
        outs_p.append(rest_p)
        outs_s.append(rest_s)
    stack = lambda outs, i: jnp.stack([o[i] for o in outs])
    return (xp, xs,
            *(stack(outs_p, i) for i in range(5)),
            *(stack(outs_s, i) for i in range(5)))
```

```python
import functools
import math

import numpy as np
import jax
import jax.numpy as jnp
from jax import lax
from jax.experimental import pallas as pl
from jax.experimental.pallas import tpu as pltpu

F32 = jnp.float32
BF16 = jnp.bfloat16

CHUNK = 64
SSM_GROUP = 16
SSM_STATE = 64
HEAD_DIM = 64
N_KV_HEADS = 2
WINDOW = 128
CONV_K = 3
REL_BUCKETS = 32
REL_MAX_DIST = 64
EPS = 1e-6
NEG_INF = -1e30
LOG2E = math.log2(math.e)

LANES = 128
SUBLANES = 8
PAIR = 2 * CHUNK
VMEM_LIMIT_BYTES = 60 * 1024 * 1024


def _params(*sem):
    return pltpu.CompilerParams(dimension_semantics=sem, vmem_limit_bytes=VMEM_LIMIT_BYTES)


def _const_spec(shape):
    nd = len(shape)
    return pl.BlockSpec(shape, lambda *_: (0,) * nd, pipeline_mode=pl.Buffered(1))


def _tok_blocks(nb, r, tm):
    if r >= tm:
        assert r % tm == 0
        return 1, tm
    bb = min(tm // r, nb)
    assert nb % bb == 0
    return bb, r


def _mod_kernel(c_ref, w_ref, b_ref, o_ref):
    c = c_ref[...]
    s = (c * jax.nn.sigmoid(c)).astype(BF16)
    o_ref[...] = jnp.dot(s, w_ref[...].astype(BF16), preferred_element_type=F32) + b_ref[...]


def _mod_call(c_all, w_ada, b_ada):
    depth, d, n = w_ada.shape
    nc = c_all.shape[0]
    tn = 1024
    return pl.pallas_call(
        _mod_kernel,
        grid=(depth, n // tn),
        in_specs=[pl.BlockSpec((nc, d), lambda l, j: (0, 0)),
                  pl.BlockSpec((None, d, tn), lambda l, j: (l, 0, j)),
                  pl.BlockSpec((None, 1, tn), lambda l, j: (l, 0, j))],
        out_specs=pl.BlockSpec((None, nc, tn), lambda l, j: (l, 0, j)),
        out_shape=jax.ShapeDtypeStruct((depth, nc, n), F32),
        compiler_params=_params("arbitrary", "arbitrary"),
    )(c_all, w_ada, b_ada.reshape(depth, 1, n))


def _bucket_maps(sample):
    lane = np.arange(PAIR)[None, :]
    if sample:
        nk = 2 * WINDOW + PAIR
        row = np.arange(nk)[:, None]
        own = row >= 2 * WINDOW
        key_b = np.where(own, (row - 2 * WINDOW) // CHUNK, row // WINDOW)
        key_s = np.where(own, WINDOW + (row - 2 * WINDOW) % CHUNK, row % WINDOW)
        rel = key_s - WINDOW - lane % CHUNK
        visible = key_b == lane // CHUNK
    else:
        nk = WINDOW + PAIR
        row = np.arange(nk)[:, None]
        rel = row - WINDOW - lane
        dc = row // CHUNK - lane // CHUNK
        visible = (dc >= 0) & (dc <= WINDOW // CHUNK)
    half = REL_BUCKETS // 2
    exact = half // 2
    n = np.abs(rel)
    nf = np.maximum(n, 1).astype(np.float32)
    far = exact + (np.log(nf / np.float32(exact)) / np.float32(math.log(REL_MAX_DIST / exact))
                   * np.float32(half - exact)).astype(np.int32)
    far = np.minimum(far, half - 1)
    bucket = np.where(rel > 0, half, 0) + np.where(n < exact, n, far)
    full = np.where(visible, bucket, -1).astype(np.int32)
    first = np.where(row >= WINDOW, full, -1).astype(np.int32)
    return np.stack([full, first])


def _bias_kernel(table_ref, bucket_ref, o_ref):
    h = pl.program_id(1)
    bucket = bucket_ref[...]
    acc = jnp.full(bucket.shape, NEG_INF, F32)
    for b in range(REL_BUCKETS):
        acc = jnp.where(bucket == b, table_ref[b, h] * LOG2E, acc)
    o_ref[...] = acc


def _bias_call(rel_bias, sample):
    buckets = jnp.asarray(_bucket_maps(sample))
    _, nk, _ = buckets.shape
    n_heads = rel_bias.shape[1]
    qpg = n_heads // N_KV_HEADS
    return pl.pallas_call(
        _bias_kernel,
        grid=(2, n_heads),
        in_specs=[pl.BlockSpec(memory_space=pltpu.SMEM),
                  pl.BlockSpec((None, nk, PAIR), lambda s, h: (s, 0, 0))],
        out_specs=pl.BlockSpec((None, None, nk, PAIR), lambda s, h: (s, h // qpg, 0, h % qpg)),
        out_shape=jax.ShapeDtypeStruct((2, N_KV_HEADS, nk, qpg * PAIR), F32),
        compiler_params=_params("arbitrary", "arbitrary"),
    )(rel_bias, buckets)


def _rms(y):
    return y * lax.rsqrt(jnp.mean(y * y, axis=-1, keepdims=True) + EPS)


def _in_proj_kernel(x_ref, shift_ref, scale_ref, zinit_ref, wrm_ref, wt_ref, kg_ref, cw_ref, ogc_ref, gq_ref,
                    u_ref, yc_ref, zt_ref, k_ref, v_ref, qt_ref, vt_ref, zprev, *, widths):
    bb, tr, d = x_ref.shape

    @pl.when(pl.program_id(1) == 0)
    def _():
        zprev[...] = zinit_ref[...]

    x = x_ref[...]
    ms = jnp.mean(x * x, axis=-1, keepdims=True)
    h = x * lax.rsqrt(ms + EPS) * (1.0 + scale_ref[...]) + shift_ref[...]
    h = h.reshape(bb * tr, d).astype(BF16)

    w_ssm, w_conv, w_kv, w_attn = widths

    pt = lax.dot_general(wt_ref[...], h, (((1,), (1,)), ((), ())), preferred_element_type=F32)
    gq = gq_ref[...]
    for hd in range(w_attn // HEAD_DIM):
        q = pt[hd * HEAD_DIM:(hd + 1) * HEAD_DIM]
        ms = jnp.sum(q * q, axis=0, keepdims=True) * (1.0 / HEAD_DIM)
        qt_ref[hd * HEAD_DIM:(hd + 1) * HEAD_DIM, :] = (q * lax.rsqrt(ms + EPS) * gq).astype(qt_ref.dtype)
    vt_ref[...] = pt[w_attn:].astype(vt_ref.dtype)

    pc = jnp.dot(h, wrm_ref[:, 0:3 * w_conv], preferred_element_type=F32)
    gb = pc[:, 0:w_conv].reshape(bb, tr, w_conv)
    z = (pc[:, w_conv:2 * w_conv] * pc[:, 2 * w_conv:3 * w_conv]).reshape(bb, tr, w_conv)

    zp = jnp.concatenate([zprev[...], z], axis=1)
    ztail = z[:, tr - SUBLANES:, :]
    zprev[...] = ztail
    zt_ref[...] = ztail
    cw = cw_ref[...]

    def conv_part(c, n):
        if bb == 1:
            rc = tr // n
            zs, gs = zp[:, c * rc:c * rc + rc + SUBLANES], gb[:, c * rc:(c + 1) * rc]
        else:
            bc, rc = bb // n, tr
            zs, gs = zp[c * bc:(c + 1) * bc], gb[c * bc:(c + 1) * bc]
        conv = (zs[:, SUBLANES - 2:SUBLANES - 2 + rc] * cw[0:1] + zs[:, SUBLANES - 1:SUBLANES - 1 + rc] * cw[1:2]
                + zs[:, SUBLANES:] * cw[2:3])
        y = (_rms(gs * conv) * ogc_ref[...]).astype(yc_ref.dtype)
        rows = y.shape[0] * y.shape[1]
        yc_ref[c * rows:(c + 1) * rows, :] = y.reshape(rows, w_conv)

    n_part = 4
    assert (tr if bb == 1 else bb) % n_part == 0
    col = 3 * w_conv
    for c, width in enumerate((w_ssm // 2, w_ssm // 2, 2 * w_kv)):
        conv_part(c, n_part)
        pp = jnp.dot(h, wrm_ref[:, col:col + width], preferred_element_type=F32)
        col += width
        if c < 2:
            u_ref[:, c * width:(c + 1) * width] = pp
        else:
            k = pp[:, :w_kv]
            v_ref[...] = pp[:, w_kv:]
    conv_part(n_part - 1, n_part)

    lo = lax.broadcasted_iota(jnp.int32, k.shape, 1) < HEAD_DIM
    k2 = k * k
    s_lo = jnp.sum(jnp.where(lo, k2, 0.0), axis=-1, keepdims=True)
    s_hi = jnp.sum(jnp.where(lo, 0.0, k2), axis=-1, keepdims=True)
    ssq = jnp.where(lo, s_lo, s_hi)
    k_ref[...] = k * lax.rsqrt(ssq * (1.0 / HEAD_DIM) + EPS) * kg_ref[...]


TOK_TILE = 512


def _in_proj_call(x3, mod3, zinit, wrm, wt, kg2, cw, ogc, gq, widths):
    nb, r, d = x3.shape
    w_ssm, w_conv, w_kv, w_attn = widths
    bb, tr = _tok_blocks(nb, r, TOK_TILE)
    tmm = bb * tr
    assert tmm == TOK_TILE
    nt = r // tr
    ttot = nb * r
    tok = lambda b, t: (b * nt + t, 0)
    tok_t = lambda b, t: (b * nt + t, 0, 0)
    halo_spec = pl.BlockSpec((bb, SUBLANES, w_conv), lambda b, t: (b, 0, 0))
    out_shape = [jax.ShapeDtypeStruct((ttot, w_ssm), F32), jax.ShapeDtypeStruct((ttot, w_conv), BF16),
                 jax.ShapeDtypeStruct((nb, SUBLANES, w_conv), F32),
                 jax.ShapeDtypeStruct((ttot, w_kv), F32), jax.ShapeDtypeStruct((ttot, w_kv), F32),
                 jax.ShapeDtypeStruct((ttot // tmm, w_attn, tmm), BF16),
                 jax.ShapeDtypeStruct((ttot // tmm, w_kv, tmm), BF16)]
    out_specs = [pl.BlockSpec((tmm, w_ssm), tok), pl.BlockSpec((tmm, w_conv), tok), halo_spec,
                 pl.BlockSpec((tmm, w_kv), tok), pl.BlockSpec((tmm, w_kv), tok),
                 pl.BlockSpec((None, w_attn, tmm), tok_t), pl.BlockSpec((None, w_kv, tmm), tok_t)]
    consts = (wrm, wt, kg2, cw, ogc, gq)
    return pl.pallas_call(
        functools.partial(_in_proj_kernel, widths=widths),
        grid=(nb // bb, nt),
        in_specs=[pl.BlockSpec((bb, tr, d), lambda b, t: (b, t, 0)),
                  pl.BlockSpec((bb, 1, d), lambda b, t: (b, 0, 0)),
                  pl.BlockSpec((bb, 1, d), lambda b, t: (b, 0, 1)),
                  halo_spec] + [_const_spec(c.shape) for c in consts],
        out_specs=out_specs,
        out_shape=out_shape,
        scratch_shapes=[pltpu.VMEM((bb, SUBLANES, w_conv), F32)],
        compiler_params=_params("arbitrary", "arbitrary"),
    )(x3, mod3, mod3, zinit, *consts)


SCAN_LANE_BLOCKS = 8


def _gelu_tanh(x):
    return 0.5 * x * (1.0 + jnp.tanh(math.sqrt(2.0 / math.pi) * (x + 0.044715 * (x * x * x))))


def _segment_perm(ts):
    lseg = ts // SUBLANES
    p = np.zeros((ts, ts), np.float32)
    i, j = np.meshgrid(np.arange(lseg), np.arange(SUBLANES), indexing="ij")
    p[(i * SUBLANES + j).ravel(), (j * lseg + i).ravel()] = 1.0
    return p


def _ssm_kernel(u_ref, h0r_ref, h0i_ref, perm_ref, permt_ref, fb_ref, ab_ref, sg_ref, pw_ref, cb_ref, dskip_ref,
                wglu_ref, og_ref, y_ref, hre_ref, him_ref, bre, bim, car, *, segs_per_seq):
    ts, w = u_ref.shape
    ns = bre.shape[1]
    lseg = ts // SUBLANES
    half_u = w // 2
    half_s = ns // 2
    t = pl.program_id(1)

    @pl.when(t == 0)
    def _():
        car[0:SUBLANES, :] = h0r_ref[...]
        car[SUBLANES:, :] = h0i_ref[...]

    u = u_ref[...]
    up = jnp.dot(perm_ref[...], u.astype(BF16), preferred_element_type=F32).astype(BF16)
    for j in range(2):
        bb = jnp.dot(up[:, j * half_u:(j + 1) * half_u], fb_ref[j], preferred_element_type=F32)
        bre[:, j * half_s:(j + 1) * half_s] = bb[:, :half_s]
        bim[:, j * half_s:(j + 1) * half_s] = bb[:, half_s:]

    seq_start = lax.broadcasted_iota(jnp.int32, (SUBLANES, LANES), 0) % segs_per_seq == 0
    zero = jnp.zeros((SUBLANES, LANES), F32)
    nblk = SCAN_LANE_BLOCKS
    for c0 in range(0, ns // LANES, nblk):
        sls = [pl.ds((c0 + i) * LANES, LANES) for i in range(nblk)]
        ab = [(ab_ref[0, :, sl], ab_ref[1, :, sl]) for sl in sls]

        def pass1(i, carry, sls=sls, ab=ab):
            rows = pl.ds(pl.multiple_of(i * SUBLANES, SUBLANES), SUBLANES)
            new = []
            for sl, (ar, ai), (hr, hi) in zip(sls, ab, carry):
                hr, hi = ar * hr - ai * hi + bre[rows, sl], ar * hi + ai * hr + bim[rows, sl]
                bre[rows, sl] = hr
                bim[rows, sl] = hi
                new.append((hr, hi))
            return tuple(new)

        ends = lax.fori_loop(0, lseg, pass1, tuple((zero, zero) for _ in sls), unroll=True)

        starts = []
        for sl, (er, ei) in zip(sls, ends):
            xr = jnp.where(seq_start, car[0:SUBLANES, sl], pltpu.roll(er, 1, axis=0))
            xi = jnp.where(seq_start, car[SUBLANES:, sl], pltpu.roll(ei, 1, axis=0))
            for k in range(segs_per_seq.bit_length() - 1):
                sr = pltpu.roll(xr, 1 << k, axis=0)
                si = pltpu.roll(xi, 1 << k, axis=0)
                mr, mi = sg_ref[2 * k, :, sl], sg_ref[2 * k + 1, :, sl]
                xr, xi = xr + (mr * sr - mi * si), xi + (mr * si + mi * sr)
            lr, li = sg_ref[6, :, sl], sg_ref[7, :, sl]
            nr = lr * xr - li * xi + er
            ni = lr * xi + li * xr + ei
            hre_ref[:, sl] = nr
            him_ref[:, sl] = ni
            car[0:SUBLANES, sl] = jnp.broadcast_to(nr[SUBLANES - 1:SUBLANES, :], (SUBLANES, LANES))
            car[SUBLANES:, sl] = jnp.broadcast_to(ni[SUBLANES - 1:SUBLANES, :], (SUBLANES, LANES))
            starts.append((xr, xi))

        def pass2(i, c, sls=sls, starts=starts):
            rows = pl.ds(pl.multiple_of(i * SUBLANES, SUBLANES), SUBLANES)
            for sl, (sr, si) in zip(sls, starts):
                pr = pw_ref[0, rows, sl]
                pi = pw_ref[1, rows, sl]
                bre[rows, sl] += pr * sr - pi * si
                bim[rows, sl] += pr * si + pi * sr
            return c

        lax.fori_loop(0, lseg, pass2, 0, unroll=True)

    ys = []
    for j in range(2):
        hcat = jnp.concatenate([bre[:, j * half_s:(j + 1) * half_s].astype(BF16),
                                bim[:, j * half_s:(j + 1) * half_s].astype(BF16)], axis=-1)
        ys.append(jnp.dot(hcat, cb_ref[j], preferred_element_type=F32))
    yp = jnp.concatenate(ys, axis=-1)
    p0 = yp.astype(BF16)
    r1 = yp - p0.astype(F32)
    p1 = r1.astype(BF16)
    p2 = (r1 - p1.astype(F32)).astype(BF16)
    parts = jnp.dot(permt_ref[...], jnp.concatenate([p0, p1, p2], axis=-1), preferred_element_type=F32)
    y = (parts[:, :w] + parts[:, w:2 * w]) + parts[:, 2 * w:] + dskip_ref[...] * u
    y = _gelu_tanh(y)
    y = y * jax.nn.sigmoid(jnp.dot(y.astype(BF16), wglu_ref[...], preferred_element_type=F32))
    y_ref[...] = (_rms(y) * og_ref[...]).astype(y_ref.dtype)


SSM_TILE = 512


def _ssm_call(u, h0r, h0i, fb, scan, cb, dskip, wglu, ogs, r):
    ttot, w = u.shape
    ab, sg, pw = scan
    ns = ab.shape[-1]
    ts = SSM_TILE
    lseg = ts // SUBLANES
    assert pw.shape[1] == ts
    nseq = ttot // r
    if r >= ts:
        assert r % ts == 0
        seq_per_tile, nt = 1, r // ts
    else:
        assert r % lseg == 0 and ts % r == 0 and nseq % (ts // r) == 0
        seq_per_tile, nt = ts // r, 1
    segs_per_seq = SUBLANES // seq_per_tile
    ngrp = nseq // seq_per_tile
    init = lambda h: jnp.repeat(h.reshape(ngrp, seq_per_tile, ns), segs_per_seq, axis=1)
    perm = _segment_perm(ts)
    perm, permt = jnp.asarray(perm, BF16), jnp.asarray(perm.T, BF16)
    st_spec = pl.BlockSpec((None, SUBLANES, ns), lambda b, t: (b, 0, 0))
    consts = (perm, permt, fb, ab, sg, pw, cb, dskip, wglu, ogs)
    y, hre, him = pl.pallas_call(
        functools.partial(_ssm_kernel, segs_per_seq=segs_per_seq),
        grid=(ngrp, nt),
        in_specs=[pl.BlockSpec((ts, w), lambda b, t: (b * nt + t, 0)), st_spec, st_spec]
        + [_const_spec(c.shape) for c in consts],
        out_specs=[pl.BlockSpec((ts, w), lambda b, t: (b * nt + t, 0)), st_spec, st_spec],
        out_shape=[jax.ShapeDtypeStruct((ttot, w), BF16),
                   jax.ShapeDtypeStruct((ngrp, SUBLANES, ns), F32), jax.ShapeDtypeStruct((ngrp, SUBLANES, ns), F32)],
        scratch_shapes=[pltpu.VMEM((ts, ns), F32), pltpu.VMEM((ts, ns), F32), pltpu.VMEM((2 * SUBLANES, ns), F32)],
        compiler_params=_params("arbitrary", "arbitrary"),
    )(u, init(h0r), init(h0i), *consts)
    last = lambda h: h[:, segs_per_seq - 1::segs_per_seq].reshape(nseq, ns)
    return y, last(hre), last(him)


def _scan_consts(a1, lseg, segs_per_seq):
    def cmul(x, y):
        return x[0] * y[0] - x[1] * y[1], x[0] * y[1] + x[1] * y[0]

    tr, ti = a1[0][None], a1[1][None]
    while tr.shape[0] < lseg:
        nr, ni = cmul((tr, ti), (tr[-1], ti[-1]))
        tr, ti = jnp.concatenate([tr, nr]), jnp.concatenate([ti, ni])
    pw = jnp.repeat(jnp.stack([tr[:lseg], ti[:lseg]]), SUBLANES, axis=1)
    ns = a1[0].shape[0]
    ab = jnp.stack([jnp.broadcast_to(a1[0], (SUBLANES, ns)), jnp.broadcast_to(a1[1], (SUBLANES, ns))])
    al = (tr[lseg - 1], ti[lseg - 1])
    al2 = cmul(al, al)
    al4 = cmul(al2, al2)
    row = jnp.arange(SUBLANES)[:, None]
    rows = []
    for k, a in enumerate((al, al2, al4)):
        keep = row % segs_per_seq >= (1 << k)
        rows += [jnp.where(keep, a[0][None, :], 0.0), jnp.where(keep, a[1][None, :], 0.0)]
    rows += [jnp.broadcast_to(al[0], (SUBLANES, ns)), jnp.broadcast_to(al[1], (SUBLANES, ns))]
    return ab, jnp.stack(rows), pw


def _ssm_consts(a_re, a_im, log_dt, b_re, b_im, c_re, c_im):
    g, p = a_re.shape
    hh = b_re.shape[-1]
    ar, ai = a_re.astype(F32), a_im.astype(F32)
    dt = jnp.exp(log_dt.astype(F32))[:, None]
    mag = jnp.exp(dt * ar)
    abar_re, abar_im = mag * jnp.cos(dt * ai), mag * jnp.sin(dt * ai)
    den = ar * ar + ai * ai
    f_re = ((abar_re - 1.0) * ar + abar_im * ai) / den
    f_im = (abar_im * ar - (abar_re - 1.0) * ai) / den
    fb_re = f_re[..., None] * b_re - f_im[..., None] * b_im
    fb_im = f_re[..., None] * b_im + f_im[..., None] * b_re
    gh = g // 2
    eye = jnp.eye(gh, dtype=F32)

    def in_blk(m):
        return jnp.einsum('gph,gk->ghkp', m, eye).reshape(gh * hh, gh * p)

    def out_blk(m):
        return jnp.einsum('ghp,gk->kpgh', m, eye).reshape(gh * p, gh * hh)

    fb = jnp.stack([jnp.concatenate([in_blk(fb_re[j * gh:(j + 1) * gh]), in_blk(fb_im[j * gh:(j + 1) * gh])], axis=1)
                    for j in range(2)]).astype(BF16)
    cb = jnp.stack([jnp.concatenate([out_blk(c_re[j * gh:(j + 1) * gh]), -out_blk(c_im[j * gh:(j + 1) * gh])], axis=0)
                    for j in range(2)]).astype(BF16)

    return fb, (abar_re.reshape(-1), abar_im.reshape(-1)), cb


def _attn_kernel(qt_ref, k_ref, vt_ref, kc_ref, vtc_ref, bias_ref, sink_ref, og_ref, yt_ref,
                 *, sample, pairs_per_seq):
    n_rows, tq = qt_ref.shape
    npairs = tq // PAIR
    qpg = n_rows // (N_KV_HEADS * HEAD_DIM)
    i = pl.program_id(0)
    if sample:
        k_all, vt_all = k_ref[...], vt_ref[...]
    else:
        k_all = jnp.concatenate([kc_ref[...], k_ref[...]], axis=0)
        vt_all = jnp.concatenate([vtc_ref[...], vt_ref[...]], axis=1)
    for p in range(npairs):
        if sample:
            kb = jnp.concatenate([kc_ref[p * 2 * WINDOW:(p + 1) * 2 * WINDOW, :],
                                  k_all[p * PAIR:(p + 1) * PAIR, :]], axis=0)
            vtb = jnp.concatenate([vtc_ref[:, p * 2 * WINDOW:(p + 1) * 2 * WINDOW],
                                   vt_all[:, p * PAIR:(p + 1) * PAIR]], axis=1)
            sel = 0
        else:
            kb = k_all[p * PAIR:p * PAIR + WINDOW + PAIR, :]
            vtb = vt_all[:, p * PAIR:p * PAIR + WINDOW + PAIR]
            sel = ((i * npairs + p) % pairs_per_seq == 0).astype(jnp.int32)
        kb = kb.astype(BF16)
        vtb = vtb.astype(BF16)
        ones = jnp.ones((2 * SUBLANES, vtb.shape[1]), BF16)
        outs = []
        ssq = jnp.zeros((1, PAIR), F32)
        for g in range(N_KV_HEADS):
            qg = jnp.concatenate([qt_ref[(g * qpg + hh) * HEAD_DIM:(g * qpg + hh + 1) * HEAD_DIM,
                                         p * PAIR:(p + 1) * PAIR] for hh in range(qpg)], axis=1)
            zero = jnp.zeros_like(qg)
            qpad = jnp.concatenate([qg, zero] if g == 0 else [zero, qg], axis=0)
            s = jnp.dot(kb, qpad, preferred_element_type=F32) + bias_ref[sel, g]
            sink = sink_ref[g]
            m = jnp.maximum(jnp.max(s, axis=0, keepdims=True), sink)
            e = jnp.exp2(s - m).astype(BF16)
            va = jnp.concatenate([vtb[g * HEAD_DIM:(g + 1) * HEAD_DIM, :], ones], axis=0)
            oa = jnp.dot(va, e, preferred_element_type=F32)
            den = oa[HEAD_DIM:HEAD_DIM + 1, :] + jnp.exp2(sink - m)
            o = oa[:HEAD_DIM, :] / den
            outs.append(o)
            sq = jnp.sum(o * o, axis=0, keepdims=True)
            for hh in range(qpg):
                ssq = ssq + sq[:, hh * PAIR:(hh + 1) * PAIR]
        rn = lax.rsqrt(ssq * (1.0 / n_rows) + EPS)
        rn = jnp.concatenate([rn] * qpg, axis=1)
        for g in range(N_KV_HEADS):
            o = outs[g] * rn * og_ref[g]
            for hh in range(qpg):
                r0 = (g * qpg + hh) * HEAD_DIM
                yt_ref[r0:r0 + HEAD_DIM, p * PAIR:(p + 1) * PAIR] = \
                    o[:, hh * PAIR:(hh + 1) * PAIR].astype(yt_ref.dtype)


def _attn_call(qt, k, vt, kc, vtc, bias, sink, oga, sample, seq_len):
    ntile, n_rows, tq = qt.shape
    kvw = k.shape[1]
    assert sample or seq_len % tq == 0
    npairs = tq // PAIR
    tile3 = lambda i: (i, 0, 0)
    if sample:
        ctx = 2 * WINDOW * npairs
        kc_spec = pl.BlockSpec((ctx, kvw), lambda i: (i, 0))
        vtc_spec = pl.BlockSpec((kvw, ctx), lambda i: (0, i))
    else:
        kc_spec = pl.BlockSpec((WINDOW, kvw), lambda i: (jnp.maximum(i * npairs - 1, 0), 0))
        vtc_spec = pl.BlockSpec((None, kvw, WINDOW), lambda i: (jnp.maximum(i - 1, 0), 0, tq // WINDOW - 1))
    return pl.pallas_call(
        functools.partial(_attn_kernel, sample=sample, pairs_per_seq=seq_len // PAIR),
        grid=(ntile,),
        in_specs=[pl.BlockSpec((None, n_rows, tq), tile3),
                  pl.BlockSpec((tq, kvw), lambda i: (i, 0)),
                  pl.BlockSpec((None, kvw, tq), tile3),
                  kc_spec, vtc_spec,
                  _const_spec(bias.shape), _const_spec(sink.shape),
                  _const_spec(oga.shape)],
        out_specs=pl.BlockSpec((None, n_rows, tq), tile3),
        out_shape=jax.ShapeDtypeStruct((ntile, n_rows, tq), BF16),
        compiler_params=_params("arbitrary"),
    )(qt, k, vt, kc, vtc, bias, sink, oga)


def _out_proj_kernel(x_ref, gate_ref, ys_ref, yt_ref, yc_ref, wo_ref, xo_ref):
    bb, tr, d = x_ref.shape
    ws, wa = ys_ref.shape[1], yt_ref.shape[0]
    o = jnp.dot(ys_ref[...], wo_ref[0:ws, :], preferred_element_type=F32)
    o += lax.dot_general(yt_ref[...], wo_ref[ws:ws + wa, :], (((0,), (0,)), ((), ())),
                         preferred_element_type=F32)
    o += jnp.dot(yc_ref[...], wo_ref[ws + wa:, :], preferred_element_type=F32)
    xo_ref[...] = x_ref[...] + gate_ref[...] * o.reshape(bb, tr, d)


def _out_proj_call(x3, mod3, ys, yt, yc, wo):
    nb, r, d = x3.shape
    ws, wa, wc = ys.shape[1], yt.shape[1], yc.shape[1]
    bb, tr = _tok_blocks(nb, r, TOK_TILE)
    tmm = bb * tr
    assert tmm == yt.shape[2]
    nt = r // tr
    tok3 = lambda b, t: (b, t, 0)
    return pl.pallas_call(
        _out_proj_kernel,
        grid=(nb // bb, nt),
        in_specs=[pl.BlockSpec((bb, tr, d), tok3),
                  pl.BlockSpec((bb, 1, d), lambda b, t: (b, 0, 2)),
                  pl.BlockSpec((tmm, ws), lambda b, t: (b * nt + t, 0)),
                  pl.BlockSpec((None, wa, tmm), lambda b, t: (b * nt + t, 0, 0)),
                  pl.BlockSpec((tmm, wc), lambda b, t: (b * nt + t, 0)),
                  _const_spec(wo.shape)],
        out_specs=pl.BlockSpec((bb, tr, d), tok3),
        out_shape=jax.ShapeDtypeStruct((nb, r, d), F32),
        compiler_params=_params("arbitrary", "arbitrary"),
    )(x3, mod3, ys, yt, yc, wo)


FFN_TILE = 1024
NORM_ROWS = 64


def _ffn_kernel(x_ref, shift_ref, scale_ref, gate_ref, w1_ref, w2_ref, o_ref, h_s, r_s):
    bb, tr, d = x_ref.shape
    j = pl.program_id(2)

    @pl.when(j == 0)
    def _():
        per = tr // NORM_ROWS

        def chunk(c):
            return c // per, pl.ds(pl.multiple_of((c % per) * NORM_ROWS, NORM_ROWS), NORM_ROWS)

        def scales(c, carry):
            b, rows = chunk(c)
            x = x_ref[b, rows, :]
            r = lax.rsqrt(jnp.mean(x * x, axis=-1, keepdims=True) + EPS)
            r_s[pl.ds(pl.multiple_of(c * NORM_ROWS, NORM_ROWS), NORM_ROWS), :] = jnp.broadcast_to(r, (NORM_ROWS, LANES))
            return carry

        def rows_out(c, carry):
            b, rows = chunk(c)
            flat = pl.ds(pl.multiple_of(c * NORM_ROWS, NORM_ROWS), NORM_ROWS)
            r = jnp.concatenate([r_s[flat, :]] * (d // LANES), axis=1)
            h = x_ref[b, rows, :] * r * (1.0 + scale_ref[b]) + shift_ref[b]
            h_s[flat, :] = h.astype(BF16)
            return carry

        n = bb * per
        lax.fori_loop(0, n, scales, 0, unroll=4)
        lax.fori_loop(0, n, rows_out, 0)

    def partial_out():
        a = jnp.maximum(jnp.dot(h_s[...], w1_ref[...], preferred_element_type=F32), 0.0)
        return jnp.dot((a * a).astype(BF16), w2_ref[...], preferred_element_type=F32).reshape(bb, tr, d)

    @pl.when(j == 0)
    def _():
        o_ref[...] = partial_out()

    @pl.when(j > 0)
    def _():
        o_ref[...] += partial_out()

    @pl.when(j == pl.num_programs(2) - 1)
    def _():
        o_ref[...] = x_ref[...] + gate_ref[...] * o_ref[...]


def _ffn_call(x3, mod3, w1, w2, tm):
    nb, r, d = x3.shape
    tf = FFN_TILE
    nj = w1.shape[1] // tf
    bb, tr = _tok_blocks(nb, r, tm)
    tok3 = lambda b, t, j: (b, t, 0)
    mod_spec = lambda c: pl.BlockSpec((bb, 1, d), lambda b, t, j: (b, 0, c))
    return pl.pallas_call(
        _ffn_kernel,
        grid=(nb // bb, r // tr, nj),
        in_specs=[pl.BlockSpec((bb, tr, d), tok3), mod_spec(3), mod_spec(4), mod_spec(5),
                  pl.BlockSpec((d, tf), lambda b, t, j: (0, j)),
                  pl.BlockSpec((tf, d), lambda b, t, j: (j, 0))],
        out_specs=pl.BlockSpec((bb, tr, d), tok3),
        out_shape=jax.ShapeDtypeStruct((nb, r, d), F32),
        scratch_shapes=[pltpu.VMEM((bb * tr, d), BF16), pltpu.VMEM((bb * tr, LANES), F32)],
        compiler_params=_params("arbitrary", "arbitrary", "arbitrary"),
    )(x3, mod3, mod3, mod3, w1, w2)


def _layer(x3, mod3, lw, bias, sample, state):
    nb, r, d = x3.shape
    widths = lw["widths"]
    w_conv = widths[1]
    scan = lw["scan"][1 if sample else 0]
    ns = scan[0].shape[-1]
    if sample:
        cache_k, cache_v, h0r, h0i, conv_buf = state
        h0r = h0r.reshape(nb, ns)
        h0i = h0i.reshape(nb, ns)
        zinit = jnp.pad(conv_buf, ((0, 0), (SUBLANES - (CONV_K - 1), 0), (0, 0)))
    else:
        h0r = h0i = jnp.zeros((nb, ns), F32)
        zinit = jnp.zeros((nb, SUBLANES, w_conv), F32)
    u, yc, ztail, k, v, qt, vt = _in_proj_call(x3, mod3, zinit, lw["wrm"], lw["wt"], lw["kg2"], lw["cw"],
                                               lw["ogc"], lw["gq"], widths)
    ys, hre, him = _ssm_call(u, h0r, h0i, lw["fb"], scan, lw["cb"], lw["dskip"], lw["wglu"], lw["ogs"], r)

    kvw = k.shape[1]
    if sample:
        n_buf = cache_k.shape[1]
        kc = cache_k.reshape(nb * n_buf, kvw)
        vtc = cache_v.reshape(nb * n_buf, kvw).T.astype(vt.dtype)
    else:
        kc, vtc = k, vt
    yt = _attn_call(qt, k, vt, kc, vtc, bias, lw["sink"], lw["oga"], sample, r)

    x3 = _out_proj_call(x3, mod3, ys, yt, yc, lw["wo"])
    x3 = _ffn_call(x3, mod3, lw["w1"], lw["w2"], tm=1024)

    keep = min(r, WINDOW)
    k4 = k.reshape(nb, r, kvw)[:, r - keep:].reshape(nb, keep, N_KV_HEADS, HEAD_DIM)
    v4 = v.reshape(nb, r, kvw)[:, r - keep:].reshape(nb, keep, N_KV_HEADS, HEAD_DIM)
    if sample:
        new_k = jnp.concatenate([cache_k, k4], axis=1)[:, -n_buf:]
        new_v = jnp.concatenate([cache_v, v4], axis=1)[:, -n_buf:]
    else:
        new_k, new_v = k4, v4
    g = ns // SSM_STATE
    return (x3, new_k, new_v, hre.reshape(nb, g, SSM_STATE), him.reshape(nb, g, SSM_STATE),
            ztail[:, -(CONV_K - 1):])


def _layer_weights(scan_segs, w_in, ssm_a_re, ssm_a_im, ssm_log_dt, ssm_b_re, ssm_b_im, ssm_c_re, ssm_c_im, ssm_d,
                   ssm_w_glu, q_norm_g, k_norm_g, attn_sinks, conv_w, out_norm_g, w_out, w_ff1, w_ff2):
    w_ssm = ssm_d.shape[0]
    n_heads = attn_sinks.shape[0]
    w_attn = n_heads * HEAD_DIM
    w_kv = N_KV_HEADS * HEAD_DIM
    w_conv = conv_w.shape[0]
    qpg = n_heads // N_KV_HEADS
    o = np.cumsum([0, w_ssm, w_attn, w_kv, w_kv, w_conv, w_conv, w_conv])
    wu, wq, wk, wv, wgb, wgc, wxc = (w_in[:, o[i]:o[i + 1]] for i in range(7))
    fb, abar, cb = _ssm_consts(ssm_a_re, ssm_a_im, ssm_log_dt, ssm_b_re, ssm_b_im, ssm_c_re, ssm_c_im)
    og = out_norm_g.astype(F32)
    return {
        "wrm": jnp.concatenate([wgb, wgc, wxc, wu, wk, wv], axis=1).astype(BF16),
        "wt": jnp.concatenate([wq, wv], axis=1).T.astype(BF16),
        "kg2": jnp.tile(k_norm_g.astype(F32), N_KV_HEADS)[None, :],
        "fb": fb, "cb": cb,
        "scan": [_scan_consts(abar, SSM_TILE // SUBLANES, segs) for segs in scan_segs],
        "dskip": ssm_d.astype(F32)[None, :],
        "wglu": ssm_w_glu.astype(BF16),
        "gq": jnp.broadcast_to((q_norm_g.astype(F32) * (HEAD_DIM ** -0.5 * LOG2E))[:, None], (HEAD_DIM, TOK_TILE)),
        "sink": jnp.broadcast_to((attn_sinks.astype(F32) * LOG2E).reshape(N_KV_HEADS, 1, qpg, 1),
                                 (N_KV_HEADS, 1, qpg, PAIR)).reshape(N_KV_HEADS, 1, qpg * PAIR),
        "cw": conv_w.astype(F32).T,
        "ogs": og[None, :w_ssm],
        "oga": jnp.broadcast_to(og[w_ssm:w_ssm + w_attn].reshape(N_KV_HEADS, qpg, HEAD_DIM, 1).transpose(0, 2, 1, 3),
                                (N_KV_HEADS, HEAD_DIM, qpg, PAIR)).reshape(N_KV_HEADS, HEAD_DIM, qpg * PAIR),
        "ogc": og[None, w_ssm + w_attn:],
        "wo": w_out.astype(BF16),
        "w1": w_ff1.astype(BF16),
        "w2": w_ff2.astype(BF16),
    }


def kernel(x_prompt, x_sample, cache_k, cache_v, state_ssm_re, state_ssm_im, state_conv, c_prompt, c_sample, rel_bias, w_ada, b_ada, w_in, ssm_a_re, ssm_a_im, ssm_log_dt, ssm_b_re, ssm_b_im, ssm_c_re, ssm_c_im, ssm_d, ssm_w_glu, q_norm_g, k_norm_g, attn_sinks, conv_w, out_norm_g, w_out, w_ff1, w_ff2):
    depth = w_in.shape[0]
    nbp, nbs = x_prompt.shape[0], x_sample.shape[0]
    assert x_sample.shape[1] == CHUNK and cache_k.shape[2] == WINDOW and nbs % 2 == 0

    nc = nbp + nbs
    ncp = -(-nc // SUBLANES) * SUBLANES
    c_all = jnp.pad(jnp.concatenate([c_prompt, c_sample], axis=0), ((0, ncp - nc), (0, 0)))
    mod = _mod_call(c_all, w_ada, b_ada)

    bias_p = _bias_call(rel_bias, sample=False)
    bias_s = _bias_call(rel_bias, sample=True)

    w_ssm, w_conv, w_kv = ssm_d.shape[1], conv_w.shape[1], N_KV_HEADS * HEAD_DIM
    widths = (w_ssm, w_conv, w_kv, attn_sinks.shape[1] * HEAD_DIM)
    scan_segs = tuple(SUBLANES // max(SSM_TILE // x.shape[1], 1) for x in (x_prompt, x_sample))
    weights = jax.vmap(functools.partial(_layer_weights, scan_segs))(
        w_in, ssm_a_re, ssm_a_im, ssm_log_dt, ssm_b_re, ssm_b_im, ssm_c_re, ssm_c_im, ssm_d, ssm_w_glu,
        q_norm_g, k_norm_g, attn_sinks, conv_w, out_norm_g, w_out, w_ff1, w_ff2)

    xp, xs = x_prompt, x_sample
    outs_p, outs_s = [], []
    for l in range(depth):
        lw = jax.tree.map(lambda a: a[l], weights)
        lw["widths"] = widths
        mod_p = mod[l, :nbp][:, None, :]
        mod_s = mod[l, nbp:nc][:, None, :]
        xp, *rest_p = _layer(xp, mod_p, lw, bias_p, False, None)
        xs, *rest_s = _layer(xs, mod_s, lw, bias_s, True,
                             (cache_k[l], cache_v[l], state_ssm_re[l], state_ssm_im[l], state_conv[l]))
        outs_p.append(rest_p)
        outs_s.append(rest_s)
    stack = lambda outs, i: jnp.stack([o[i] for o in outs])
    return (xp, xs,
            *(stack(outs_p, i) for i in range(5)),
            *(stack(outs_s, i) for i in range(5)))
```

```python
import functools
import math

import numpy as np
import jax
import jax.numpy as jnp
from jax import lax
from jax.experimental import pallas as pl
from jax.experimental.pallas import tpu as pltpu

F32 = jnp.float32
BF16 = jnp.bfloat16

CHUNK = 64
SSM_GROUP = 16
SSM_STATE = 64
HEAD_DIM = 64
N_KV_HEADS = 2
WINDOW = 128
CONV_K = 3
REL_BUCKETS = 32
REL_MAX_DIST = 64
EPS = 1e-6
NEG_INF = -1e30
LOG2E = math.log2(math.e)

LANES = 128
SUBLANES = 8
PAIR = 2 * CHUNK
VMEM_LIMIT_BYTES = 60 * 1024 * 1024


def _params(*sem):
    return pltpu.CompilerParams(dimension_semantics=sem, vmem_limit_bytes=VMEM_LIMIT_BYTES)


def _const_spec(shape):
    nd = len(shape)
    return pl.BlockSpec(shape, lambda *_: (0,) * nd, pipeline_mode=pl.Buffered(1))


def _tok_blocks(nb, r, tm):
    if r >= tm:
        assert r % tm == 0
        return 1, tm
    bb = min(tm // r, nb)
    assert nb % bb == 0
    return bb, r


def _mod_kernel(c_ref, w_ref, b_ref, o_ref):
    c = c_ref[...]
    s = (c * jax.nn.sigmoid(c)).astype(BF16)
    o_ref[...] = jnp.dot(s, w_ref[...].astype(BF16), preferred_element_type=F32) + b_ref[...]


def _mod_call(c_all, w_ada, b_ada):
    depth, d, n = w_ada.shape
    nc = c_all.shape[0]
    tn = 1024
    return pl.pallas_call(
        _mod_kernel,
        grid=(depth, n // tn),
        in_specs=[pl.BlockSpec((nc, d), lambda l, j: (0, 0)),
                  pl.BlockSpec((None, d, tn), lambda l, j: (l, 0, j)),
                  pl.BlockSpec((None, 1, tn), lambda l, j: (l, 0, j))],
        out_specs=pl.BlockSpec((None, nc, tn), lambda l, j: (l, 0, j)),
        out_shape=jax.ShapeDtypeStruct((depth, nc, n), F32),
        compiler_params=_params("arbitrary", "arbitrary"),
    )(c_all, w_ada, b_ada.reshape(depth, 1, n))


def _bucket_maps(sample):
    lane = np.arange(PAIR)[None, :]
    if sample:
        nk = 2 * WINDOW + PAIR
        row = np.arange(nk)[:, None]
        own = row >= 2 * WINDOW
        key_b = np.where(own, (row - 2 * WINDOW) // CHUNK, row // WINDOW)
        key_s = np.where(own, WINDOW + (row - 2 * WINDOW) % CHUNK, row % WINDOW)
        rel = key_s - WINDOW - lane % CHUNK
        visible = key_b == lane // CHUNK
    else:
        nk = WINDOW + PAIR
        row = np.arange(nk)[:, None]
        rel = row - WINDOW - lane
        dc = row // CHUNK - lane // CHUNK
        visible = (dc >= 0) & (dc <= WINDOW // CHUNK)
    half = REL_BUCKETS // 2
    exact = half // 2
    n = np.abs(rel)
    nf = np.maximum(n, 1).astype(np.float32)
    far = exact + (np.log(nf / np.float32(exact)) / np.float32(math.log(REL_MAX_DIST / exact))
                   * np.float32(half - exact)).astype(np.int32)
    far = np.minimum(far, half - 1)
    bucket = np.where(rel > 0, half, 0) + np.where(n < exact, n, far)
    full = np.where(visible, bucket, -1).astype(np.int32)
    first = np.where(row >= WINDOW, full, -1).astype(np.int32)
    return np.stack([full, first])


def _bias_kernel(table_ref, bucket_ref, o_ref):
    h = pl.program_id(1)
    bucket = bucket_ref[...]
    acc = jnp.full(bucket.shape, NEG_INF, F32)
    for b in range(REL_BUCKETS):
        acc = jnp.where(bucket == b, table_ref[b, h] * LOG2E, acc)
    o_ref[...] = acc


def _bias_call(rel_bias, sample):
    buckets = jnp.asarray(_bucket_maps(sample))
    _, nk, _ = buckets.shape
    n_heads = rel_bias.shape[1]
    qpg = n_heads // N_KV_HEADS
    return pl.pallas_call(
        _bias_kernel,
        grid=(2, n_heads),
        in_specs=[pl.BlockSpec(memory_space=pltpu.SMEM),
                  pl.BlockSpec((None, nk, PAIR), lambda s, h: (s, 0, 0))],
        out_specs=pl.BlockSpec((None, None, nk, PAIR), lambda s, h: (s, h // qpg, 0, h % qpg)),
        out_shape=jax.ShapeDtypeStruct((2, N_KV_HEADS, nk, qpg * PAIR), F32),
        compiler_params=_params("arbitrary", "arbitrary"),
    )(rel_bias, buckets)


def _rms(y):
    return y * lax.rsqrt(jnp.mean(y * y, axis=-1, keepdims=True) + EPS)


def _in_proj_kernel(x_ref, shift_ref, scale_ref, zinit_ref, wrm_ref, wt_ref, kg_ref, cw_ref, ogc_ref, gq_ref,
                    u_ref, yc_ref, zt_ref, k_ref, v_ref, qt_ref, vt_ref, zprev, *, widths):
    bb, tr, d = x_ref.shape

    @pl.when(pl.program_id(1) == 0)
    def _():
        zprev[...] = zinit_ref[...]

    x = x_ref[...]
    ms = jnp.mean(x * x, axis=-1, keepdims=True)
    h = x * lax.rsqrt(ms + EPS) * (1.0 + scale_ref[...]) + shift_ref[...]
    h = h.reshape(bb * tr, d).astype(BF16)

    w_ssm, w_conv, w_kv, w_attn = widths

    pt = lax.dot_general(wt_ref[...], h, (((1,), (1,)), ((), ())), preferred_element_type=F32)
    gq = gq_ref[...]
    for hd in range(w_attn // HEAD_DIM):
        q = pt[hd * HEAD_DIM:(hd + 1) * HEAD_DIM]
        ms = jnp.sum(q * q, axis=0, keepdims=True) * (1.0 / HEAD_DIM)
        qt_ref[hd * HEAD_DIM:(hd + 1) * HEAD_DIM, :] = (q * lax.rsqrt(ms + EPS) * gq).astype(qt_ref.dtype)
    vt_ref[...] = pt[w_attn:].astype(vt_ref.dtype)

    pc = jnp.dot(h, wrm_ref[:, 0:3 * w_conv], preferred_element_type=F32)
    gb = pc[:, 0:w_conv].reshape(bb, tr, w_conv)
    z = (pc[:, w_conv:2 * w_conv] * pc[:, 2 * w_conv:3 * w_conv]).reshape(bb, tr, w_conv)

    zp = jnp.concatenate([zprev[...], z], axis=1)
    ztail = z[:, tr - SUBLANES:, :]
    zprev[...] = ztail
    zt_ref[...] = ztail
    cw = cw_ref[...]

    def conv_part(c, n):
        if bb == 1:
            rc = tr // n
            zs, gs = zp[:, c * rc:c * rc + rc + SUBLANES], gb[:, c * rc:(c + 1) * rc]
        else:
            bc, rc = bb // n, tr
            zs, gs = zp[c * bc:(c + 1) * bc], gb[c * bc:(c + 1) * bc]
        conv = (zs[:, SUBLANES - 2:SUBLANES - 2 + rc] * cw[0:1] + zs[:, SUBLANES - 1:SUBLANES - 1 + rc] * cw[1:2]
                + zs[:, SUBLANES:] * cw[2:3])
        y = (_rms(gs * conv) * ogc_ref[...]).astype(yc_ref.dtype)
        rows = y.shape[0] * y.shape[1]
        yc_ref[c * rows:(c + 1) * rows, :] = y.reshape(rows, w_conv)

    n_part = 4
    assert (tr if bb == 1 else bb) % n_part == 0
    col = 3 * w_conv
    for c, width in enumerate((w_ssm // 2, w_ssm // 2, 2 * w_kv)):
        conv_part(c, n_part)
        pp = jnp.dot(h, wrm_ref[:, col:col + width], preferred_element_type=F32)
        col += width
        if c < 2:
            u_ref[:, c * width:(c + 1) * width] = pp
        else:
            k = pp[:, :w_kv]
            v_ref[...] = pp[:, w_kv:]
    conv_part(n_part - 1, n_part)

    lo = lax.broadcasted_iota(jnp.int32, k.shape, 1) < HEAD_DIM
    k2 = k * k
    s_lo = jnp.sum(jnp.where(lo, k2, 0.0), axis=-1, keepdims=True)
    s_hi = jnp.sum(jnp.where(lo, 0.0, k2), axis=-1, keepdims=True)
    ssq = jnp.where(lo, s_lo, s_hi)
    k_ref[...] = k * lax.rsqrt(ssq * (1.0 / HEAD_DIM) + EPS) * kg_ref[...]


TOK_TILE = 512


def _in_proj_call(x3, mod3, zinit, wrm, wt, kg2, cw, ogc, gq, widths):
    nb, r, d = x3.shape
    w_ssm, w_conv, w_kv, w_attn = widths
    bb, tr = _tok_blocks(nb, r, TOK_TILE)
    tmm = bb * tr
    assert tmm == TOK_TILE
    nt = r // tr
    ttot = nb * r
    tok = lambda b, t: (b * nt + t, 0)
    tok_t = lambda b, t: (b * nt + t, 0, 0)
    halo_spec = pl.BlockSpec((bb, SUBLANES, w_conv), lambda b, t: (b, 0, 0))
    out_shape = [jax.ShapeDtypeStruct((ttot, w_ssm), F32), jax.ShapeDtypeStruct((ttot, w_conv), BF16),
                 jax.ShapeDtypeStruct((nb, SUBLANES, w_conv), F32),
                 jax.ShapeDtypeStruct((ttot, w_kv), F32), jax.ShapeDtypeStruct((ttot, w_kv), F32),
                 jax.ShapeDtypeStruct((ttot // tmm, w_attn, tmm), BF16),
                 jax.ShapeDtypeStruct((ttot // tmm, w_kv, tmm), BF16)]
    out_specs = [pl.BlockSpec((tmm, w_ssm), tok), pl.BlockSpec((tmm, w_conv), tok), halo_spec,
                 pl.BlockSpec((tmm, w_kv), tok), pl.BlockSpec((tmm, w_kv), tok),
                 pl.BlockSpec((None, w_attn, tmm), tok_t), pl.BlockSpec((None, w_kv, tmm), tok_t)]
    consts = (wrm, wt, kg2, cw, ogc, gq)
    return pl.pallas_call(
        functools.partial(_in_proj_kernel, widths=widths),
        grid=(nb // bb, nt),
        in_specs=[pl.BlockSpec((bb, tr, d), lambda b, t: (b, t, 0)),
                  pl.BlockSpec((bb, 1, d), lambda b, t: (b, 0, 0)),
                  pl.BlockSpec((bb, 1, d), lambda b, t: (b, 0, 1)),
                  halo_spec] + [_const_spec(c.shape) for c in consts],
        out_specs=out_specs,
        out_shape=out_shape,
        scratch_shapes=[pltpu.VMEM((bb, SUBLANES, w_conv), F32)],
        compiler_params=_params("arbitrary", "arbitrary"),
    )(x3, mod3, mod3, zinit, *consts)


SCAN_LANE_BLOCKS = 8


def _gelu_tanh(x):
    return 0.5 * x * (1.0 + jnp.tanh(math.sqrt(2.0 / math.pi) * (x + 0.044715 * (x * x * x))))


def _segment_perm(ts):
    lseg = ts // SUBLANES
    p = np.zeros((ts, ts), np.float32)
    i, j = np.meshgrid(np.arange(lseg), np.arange(SUBLANES), indexing="ij")
    p[(i * SUBLANES + j).ravel(), (j * lseg + i).ravel()] = 1.0
    return p


def _ssm_kernel(u_ref, h0r_ref, h0i_ref, perm_ref, permt_ref, fb_ref, ab_ref, sg_ref, pw_ref, cb_ref, dskip_ref,
                wglu_ref, og_ref, y_ref, hre_ref, him_ref, bre, bim, car, *, segs_per_seq):
    ts, w = u_ref.shape
    ns = bre.shape[1]
    lseg = ts // SUBLANES
    half_u = w // 2
    half_s = ns // 2
    t = pl.program_id(1)

    @pl.when(t == 0)
    def _():
        car[0:SUBLANES, :] = h0r_ref[...]
        car[SUBLANES:, :] = h0i_ref[...]

    u = u_ref[...]
    up = jnp.dot(perm_ref[...], u.astype(BF16), preferred_element_type=F32).astype(BF16)
    for j in range(2):
        bb = jnp.dot(up[:, j * half_u:(j + 1) * half_u], fb_ref[j], preferred_element_type=F32)
        bre[:, j * half_s:(j + 1) * half_s] = bb[:, :half_s]
        bim[:, j * half_s:(j + 1) * half_s] = bb[:, half_s:]

    seq_start = lax.broadcasted_iota(jnp.int32, (SUBLANES, LANES), 0) % segs_per_seq == 0
    zero = jnp.zeros((SUBLANES, LANES), F32)
    nblk = SCAN_LANE_BLOCKS
    for c0 in range(0, ns // LANES, nblk):
        sls = [pl.ds((c0 + i) * LANES, LANES) for i in range(nblk)]
        ab = [(ab_ref[0, :, sl], ab_ref[1, :, sl]) for sl in sls]

        def pass1(i, carry, sls=sls, ab=ab):
            rows = pl.ds(pl.multiple_of(i * SUBLANES, SUBLANES), SUBLANES)
            new = []
            for sl, (ar, ai), (hr, hi) in zip(sls, ab, carry):
                hr, hi = ar * hr - ai * hi + bre[rows, sl], ar * hi + ai * hr + bim[rows, sl]
                bre[rows, sl] = hr
                bim[rows, sl] = hi
                new.append((hr, hi))
            return tuple(new)

        ends = lax.fori_loop(0, lseg, pass1, tuple((zero, zero) for _ in sls), unroll=True)

        starts = []
        for sl, (er, ei) in zip(sls, ends):
            xr = jnp.where(seq_start, car[0:SUBLANES, sl], pltpu.roll(er, 1, axis=0))
            xi = jnp.where(seq_start, car[SUBLANES:, sl], pltpu.roll(ei, 1, axis=0))
            for k in range(segs_per_seq.bit_length() - 1):
                sr = pltpu.roll(xr, 1 << k, axis=0)
                si = pltpu.roll(xi, 1 << k, axis=0)
                mr, mi = sg_ref[2 * k, :, sl], sg_ref[2 * k + 1, :, sl]
                xr, xi = xr + (mr * sr - mi * si), xi + (mr * si + mi * sr)
            lr, li = sg_ref[6, :, sl], sg_ref[7, :, sl]
            nr = lr * xr - li * xi + er
            ni = lr * xi + li * xr + ei
            hre_ref[:, sl] = nr
            him_ref[:, sl] = ni
            car[0:SUBLANES, sl] = jnp.broadcast_to(nr[SUBLANES - 1:SUBLANES, :], (SUBLANES, LANES))
            car[SUBLANES:, sl] = jnp.broadcast_to(ni[SUBLANES - 1:SUBLANES, :], (SUBLANES, LANES))
            starts.append((xr, xi))

        def pass2(i, c, sls=sls, starts=starts):
            rows = pl.ds(pl.multiple_of(i * SUBLANES, SUBLANES), SUBLANES)
            for sl, (sr, si) in zip(sls, starts):
                pr = pw_ref[0, rows, sl]
                pi = pw_ref[1, rows, sl]
                bre[rows, sl] += pr * sr - pi * si
                bim[rows, sl] += pr * si + pi * sr
            return c

        lax.fori_loop(0, lseg, pass2, 0, unroll=True)

    ys = []
    for j in range(2):
        hcat = jnp.concatenate([bre[:, j * half_s:(j + 1) * half_s].astype(BF16),
                                bim[:, j * half_s:(j + 1) * half_s].astype(BF16)], axis=-1)
        ys.append(jnp.dot(hcat, cb_ref[j], preferred_element_type=F32))
    yp = jnp.concatenate(ys, axis=-1)
    p0 = yp.astype(BF16)
    r1 = yp - p0.astype(F32)
    p1 = r1.astype(BF16)
    p2 = (r1 - p1.astype(F32)).astype(BF16)
    parts = jnp.dot(permt_ref[...], jnp.concatenate([p0, p1, p2], axis=-1), preferred_element_type=F32)
    y = (parts[:, :w] + parts[:, w:2 * w]) + parts[:, 2 * w:] + dskip_ref[...] * u
    y = _gelu_tanh(y)
    y = y * jax.nn.sigmoid(jnp.dot(y.astype(BF16), wglu_ref[...], preferred_element_type=F32))
    y_ref[...] = (_rms(y) * og_ref[...]).astype(y_ref.dtype)


SSM_TILE = 512


def _ssm_call(u, h0r, h0i, fb, scan, cb, dskip, wglu, ogs, r):
    ttot, w = u.shape
    ab, sg, pw = scan
    ns = ab.shape[-1]
    ts = SSM_TILE
    lseg = ts // SUBLANES
    assert pw.shape[1] == ts
    nseq = ttot // r
    if r >= ts:
        assert r % ts == 0
        seq_per_tile, nt = 1, r // ts
    else:
        assert r % lseg == 0 and ts % r == 0 and nseq % (ts // r) == 0
        seq_per_tile, nt = ts // r, 1
    segs_per_seq = SUBLANES // seq_per_tile
    ngrp = nseq // seq_per_tile
    init = lambda h: jnp.repeat(h.reshape(ngrp, seq_per_tile, ns), segs_per_seq, axis=1)
    perm = _segment_perm(ts)
    perm, permt = jnp.asarray(perm, BF16), jnp.asarray(perm.T, BF16)
    st_spec = pl.BlockSpec((None, SUBLANES, ns), lambda b, t: (b, 0, 0))
    consts = (perm, permt, fb, ab, sg, pw, cb, dskip, wglu, ogs)
    y, hre, him = pl.pallas_call(
        functools.partial(_ssm_kernel, segs_per_seq=segs_per_seq),
        grid=(ngrp, nt),
        in_specs=[pl.BlockSpec((ts, w), lambda b, t: (b * nt + t, 0)), st_spec, st_spec]
        + [_const_spec(c.shape) for c in consts],
        out_specs=[pl.BlockSpec((ts, w), lambda b, t: (b * nt + t, 0)), st_spec, st_spec],
        out_shape=[jax.ShapeDtypeStruct((ttot, w), BF16),
                   jax.ShapeDtypeStruct((ngrp, SUBLANES, ns), F32), jax.ShapeDtypeStruct((ngrp, SUBLANES, ns), F32)],
        scratch_shapes=[pltpu.VMEM((ts, ns), F32), pltpu.VMEM((ts, ns), F32), pltpu.VMEM((2 * SUBLANES, ns), F32)],
        compiler_params=_params("arbitrary", "arbitrary"),
    )(u, init(h0r), init(h0i), *consts)
    last = lambda h: h[:, segs_per_seq - 1::segs_per_seq].reshape(nseq, ns)
    return y, last(hre), last(him)


def _scan_consts(a1, lseg, segs_per_seq):
    def cmul(x, y):
        return x[0] * y[0] - x[1] * y[1], x[0] * y[1] + x[1] * y[0]

    tr, ti = a1[0][None], a1[1][None]
    while tr.shape[0] < lseg:
        nr, ni = cmul((tr, ti), (tr[-1], ti[-1]))
        tr, ti = jnp.concatenate([tr, nr]), jnp.concatenate([ti, ni])
    pw = jnp.repeat(jnp.stack([tr[:lseg], ti[:lseg]]), SUBLANES, axis=1)
    ns = a1[0].shape[0]
    ab = jnp.stack([jnp.broadcast_to(a1[0], (SUBLANES, ns)), jnp.broadcast_to(a1[1], (SUBLANES, ns))])
    al = (tr[lseg - 1], ti[lseg - 1])
    al2 = cmul(al, al)
    al4 = cmul(al2, al2)
    row = jnp.arange(SUBLANES)[:, None]
    rows = []
    for k, a in enumerate((al, al2, al4)):
        keep = row % segs_per_seq >= (1 << k)
        rows += [jnp.where(keep, a[0][None, :], 0.0), jnp.where(keep, a[1][None, :], 0.0)]
    rows += [jnp.broadcast_to(al[0], (SUBLANES, ns)), jnp.broadcast_to(al[1], (SUBLANES, ns))]
    return ab, jnp.stack(rows), pw


def _ssm_consts(a_re, a_im, log_dt, b_re, b_im, c_re, c_im):
    g, p = a_re.shape
    hh = b_re.shape[-1]
    ar, ai = a_re.astype(F32), a_im.astype(F32)
    dt = jnp.exp(log_dt.astype(F32))[:, None]
    mag = jnp.exp(dt * ar)
    abar_re, abar_im = mag * jnp.cos(dt * ai), mag * jnp.sin(dt * ai)
    den = ar * ar + ai * ai
    f_re = ((abar_re - 1.0) * ar + abar_im * ai) / den
    f_im = (abar_im * ar - (abar_re - 1.0) * ai) / den
    fb_re = f_re[..., None] * b_re - f_im[..., None] * b_im
    fb_im = f_re[..., None] * b_im + f_im[..., None] * b_re
    gh = g // 2
    eye = jnp.eye(gh, dtype=F32)

    def in_blk(m):
        return jnp.einsum('gph,gk->ghkp', m, eye).reshape(gh * hh, gh * p)

    def out_blk(m):
        return jnp.einsum('ghp,gk->kpgh', m, eye).reshape(gh * p, gh * hh)

    fb = jnp.stack([jnp.concatenate([in_blk(fb_re[j * gh:(j + 1) * gh]), in_blk(fb_im[j * gh:(j + 1) * gh])], axis=1)
                    for j in range(2)]).astype(BF16)
    cb = jnp.stack([jnp.concatenate([out_blk(c_re[j * gh:(j + 1) * gh]), -out_blk(c_im[j * gh:(j + 1) * gh])], axis=0)
                    for j in range(2)]).astype(BF16)

    return fb, (abar_re.reshape(-1), abar_im.reshape(-1)), cb


def _attn_kernel(qt_ref, k_ref, vt_ref, kc_ref, vtc_ref, bias_ref, sink_ref, og_ref, yt_ref,
                 *, sample, pairs_per_seq):
    n_rows, tq = qt_ref.shape
    npairs = tq // PAIR
    qpg = n_rows // (N_KV_HEADS * HEAD_DIM)
    i = pl.program_id(0)
    if sample:
        k_all, vt_all = k_ref[...], vt_ref[...]
    else:
        k_all = jnp.concatenate([kc_ref[...], k_ref[...]], axis=0)
        vt_all = jnp.concatenate([vtc_ref[...], vt_ref[...]], axis=1)
    for p in range(npairs):
        if sample:
            kb = jnp.concatenate([kc_ref[p * 2 * WINDOW:(p + 1) * 2 * WINDOW, :],
                                  k_all[p * PAIR:(p + 1) * PAIR, :]], axis=0)
            vtb = jnp.concatenate([vtc_ref[:, p * 2 * WINDOW:(p + 1) * 2 * WINDOW],
                                   vt_all[:, p * PAIR:(p + 1) * PAIR]], axis=1)
            sel = 0
        else:
            kb = k_all[p * PAIR:p * PAIR + WINDOW + PAIR, :]
            vtb = vt_all[:, p * PAIR:p * PAIR + WINDOW + PAIR]
            sel = ((i * npairs + p) % pairs_per_seq == 0).astype(jnp.int32)
        kb = kb.astype(BF16)
        vtb = vtb.astype(BF16)
        ones = jnp.ones((2 * SUBLANES, vtb.shape[1]), BF16)
        outs = []
        ssq = jnp.zeros((1, PAIR), F32)
        for g in range(N_KV_HEADS):
            qg = jnp.concatenate([qt_ref[(g * qpg + hh) * HEAD_DIM:(g * qpg + hh + 1) * HEAD_DIM,
                                         p * PAIR:(p + 1) * PAIR] for hh in range(qpg)], axis=1)
            zero = jnp.zeros_like(qg)
            qpad = jnp.concatenate([qg, zero] if g == 0 else [zero, qg], axis=0)
            s = jnp.dot(kb, qpad, preferred_element_type=F32) + bias_ref[sel, g]
            sink = sink_ref[g]
            m = jnp.maximum(jnp.max(s, axis=0, keepdims=True), sink)
            e = jnp.exp2(s - m).astype(BF16)
            va = jnp.concatenate([vtb[g * HEAD_DIM:(g + 1) * HEAD_DIM, :], ones], axis=0)
            oa = jnp.dot(va, e, preferred_element_type=F32)
            den = oa[HEAD_DIM:HEAD_DIM + 1, :] + jnp.exp2(sink - m)
            o = oa[:HEAD_DIM, :] / den
            outs.append(o)
            sq = jnp.sum(o * o, axis=0, keepdims=True)
            for hh in range(qpg):
                ssq = ssq + sq[:, hh * PAIR:(hh + 1) * PAIR]
        rn = lax.rsqrt(ssq * (1.0 / n_rows) + EPS)
        rn = jnp.concatenate([rn] * qpg, axis=1)
        for g in range(N_KV_HEADS):
            o = outs[g] * rn * og_ref[g]
            for hh in range(qpg):
                r0 = (g * qpg + hh) * HEAD_DIM
                yt_ref[r0:r0 + HEAD_DIM, p * PAIR:(p + 1) * PAIR] = \
                    o[:, hh * PAIR:(hh + 1) * PAIR].astype(yt_ref.dtype)


def _attn_call(qt, k, vt, kc, vtc, bias, sink, oga, sample, seq_len):
    ntile, n_rows, tq = qt.shape
    kvw = k.shape[1]
    assert sample or seq_len % tq == 0
    npairs = tq // PAIR
    tile3 = lambda i: (i, 0, 0)
    if sample:
        ctx = 2 * WINDOW * npairs
        kc_spec = pl.BlockSpec((ctx, kvw), lambda i: (i, 0))
        vtc_spec = pl.BlockSpec((kvw, ctx), lambda i: (0, i))
    else:
        kc_spec = pl.BlockSpec((WINDOW, kvw), lambda i: (jnp.maximum(i * npairs - 1, 0), 0))
        vtc_spec = pl.BlockSpec((None, kvw, WINDOW), lambda i: (jnp.maximum(i - 1, 0), 0, tq // WINDOW - 1))
    return pl.pallas_call(
        functools.partial(_attn_kernel, sample=sample, pairs_per_seq=seq_len // PAIR),
        grid=(ntile,),
        in_specs=[pl.BlockSpec((None, n_rows, tq), tile3),
                  pl.BlockSpec((tq, kvw), lambda i: (i, 0)),
                  pl.BlockSpec((None, kvw, tq), tile3),
                  kc_spec, vtc_spec,
                  _const_spec(bias.shape), _const_spec(sink.shape),
                  _const_spec(oga.shape)],
        out_specs=pl.BlockSpec((None, n_rows, tq), tile3),
        out_shape=jax.ShapeDtypeStruct((ntile, n_rows, tq), BF16),
        compiler_params=_params("arbitrary"),
    )(qt, k, vt, kc, vtc, bias, sink, oga)


def _out_proj_kernel(x_ref, gate_ref, ys_ref, yt_ref, yc_ref, wo_ref, xo_ref):
    bb, tr, d = x_ref.shape
    ws, wa = ys_ref.shape[1], yt_ref.shape[0]
    o = jnp.dot(ys_ref[...], wo_ref[0:ws, :], preferred_element_type=F32)
    o += lax.dot_general(yt_ref[...], wo_ref[ws:ws + wa, :], (((0,), (0,)), ((), ())),
                         preferred_element_type=F32)
    o += jnp.dot(yc_ref[...], wo_ref[ws + wa:, :], preferred_element_type=F32)
    xo_ref[...] = x_ref[...] + gate_ref[...] * o.reshape(bb, tr, d)


def _out_proj_call(x3, mod3, ys, yt, yc, wo):
    nb, r, d = x3.shape
    ws, wa, wc = ys.shape[1], yt.shape[1], yc.shape[1]
    bb, tr = _tok_blocks(nb, r, TOK_TILE)
    tmm = bb * tr
    assert tmm == yt.shape[2]
    nt = r // tr
    tok3 = lambda b, t: (b, t, 0)
    return pl.pallas_call(
        _out_proj_kernel,
        grid=(nb // bb, nt),
        in_specs=[pl.BlockSpec((bb, tr, d), tok3),
                  pl.BlockSpec((bb, 1, d), lambda b, t: (b, 0, 2)),
                  pl.BlockSpec((tmm, ws), lambda b, t: (b * nt + t, 0)),
                  pl.BlockSpec((None, wa, tmm), lambda b, t: (b * nt + t, 0, 0)),
                  pl.BlockSpec((tmm, wc), lambda b, t: (b * nt + t, 0)),
                  _const_spec(wo.shape)],
        out_specs=pl.BlockSpec((bb, tr, d), tok3),
        out_shape=jax.ShapeDtypeStruct((nb, r, d), F32),
        compiler_params=_params("arbitrary", "arbitrary"),
    )(x3, mod3, ys, yt, yc, wo)


FFN_TILE = 1024
NORM_ROWS = 64


def _ffn_kernel(x_ref, shift_ref, scale_ref, gate_ref, w1_hbm, w2_hbm, o_ref, h_s, r_s, w1_buf, w2_buf, sem):
    bb, tr, d = x_ref.shape
    tf = w1_buf.shape[2]
    nj = w1_hbm.shape[1] // tf
    step = pl.program_id(0) * pl.num_programs(1) + pl.program_id(1)
    n_steps = pl.num_programs(0) * pl.num_programs(1)

    def w_copies(j, slot):
        cols = pl.ds(pl.multiple_of(j * tf, tf), tf)
        return (pltpu.make_async_copy(w1_hbm.at[:, cols], w1_buf.at[slot], sem.at[0, slot]),
                pltpu.make_async_copy(w2_hbm.at[cols, :], w2_buf.at[slot], sem.at[1, slot]))

    @pl.when(step == 0)
    def _():
        for c in w_copies(0, 0):
            c.start()

    per = tr // NORM_ROWS

    def chunk(c):
        return c // per, pl.ds(pl.multiple_of((c % per) * NORM_ROWS, NORM_ROWS), NORM_ROWS)

    def scales(c, carry):
        b, rows = chunk(c)
        x = x_ref[b, rows, :]
        r = lax.rsqrt(jnp.mean(x * x, axis=-1, keepdims=True) + EPS)
        r_s[pl.ds(pl.multiple_of(c * NORM_ROWS, NORM_ROWS), NORM_ROWS), :] = jnp.broadcast_to(r, (NORM_ROWS, LANES))
        o_ref[b, rows, :] = jnp.zeros((NORM_ROWS, d), F32)
        return carry

    def rows_out(c, carry):
        b, rows = chunk(c)
        flat = pl.ds(pl.multiple_of(c * NORM_ROWS, NORM_ROWS), NORM_ROWS)
        r = jnp.concatenate([r_s[flat, :]] * (d // LANES), axis=1)
        h = x_ref[b, rows, :] * r * (1.0 + scale_ref[b]) + shift_ref[b]
        h_s[flat, :] = h.astype(BF16)
        return carry

    lax.fori_loop(0, bb * per, scales, 0, unroll=4)
    lax.fori_loop(0, bb * per, rows_out, 0)

    def tile_pair(jj, carry):
        for slot in (0, 1):
            j = 2 * jj + slot
            for c in w_copies(j, slot):
                c.wait()

            @pl.when(j + 1 < nj)
            def _():
                for c in w_copies(j + 1, 1 - slot):
                    c.start()

            if slot == 1:
                @pl.when(jnp.logical_and(j + 1 == nj, step + 1 < n_steps))
                def _():
                    for c in w_copies(0, 0):
                        c.start()

            a = jnp.maximum(jnp.dot(h_s[...], w1_buf[slot], preferred_element_type=F32), 0.0)
            o_ref[...] += jnp.dot((a * a).astype(BF16), w2_buf[slot], preferred_element_type=F32).reshape(bb, tr, d)
        return carry

    lax.fori_loop(0, nj // 2, tile_pair, 0)
    o_ref[...] = x_ref[...] + gate_ref[...] * o_ref[...]


def _ffn_call(x3, mod3, w1, w2, tm):
    nb, r, d = x3.shape
    tf = FFN_TILE
    assert w1.shape[1] % (2 * tf) == 0
    bb, tr = _tok_blocks(nb, r, tm)
    tok3 = lambda b, t: (b, t, 0)
    mod_spec = lambda c: pl.BlockSpec((bb, 1, d), lambda b, t: (b, 0, c))
    return pl.pallas_call(
        _ffn_kernel,
        grid=(nb // bb, r // tr),
        in_specs=[pl.BlockSpec((bb, tr, d), tok3), mod_spec(3), mod_spec(4), mod_spec(5),
                  pl.BlockSpec(memory_space=pl.ANY), pl.BlockSpec(memory_space=pl.ANY)],
        out_specs=pl.BlockSpec((bb, tr, d), tok3),
        out_shape=jax.ShapeDtypeStruct((nb, r, d), F32),
        scratch_shapes=[pltpu.VMEM((bb * tr, d), BF16), pltpu.VMEM((bb * tr, LANES), F32),
                        pltpu.VMEM((2, d, tf), BF16), pltpu.VMEM((2, tf, d), BF16),
                        pltpu.SemaphoreType.DMA((2, 2))],
        compiler_params=_params("arbitrary", "arbitrary"),
    )(x3, mod3, mod3, mod3, w1, w2)


def _layer(x3, mod3, lw, bias, sample, state):
    nb, r, d = x3.shape
    widths = lw["widths"]
    w_conv = widths[1]
    scan = lw["scan"][1 if sample else 0]
    ns = scan[0].shape[-1]
    if sample:
        cache_k, cache_v, h0r, h0i, conv_buf = state
        h0r = h0r.reshape(nb, ns)
        h0i = h0i.reshape(nb, ns)
        zinit = jnp.pad(conv_buf, ((0, 0), (SUBLANES - (CONV_K - 1), 0), (0, 0)))
    else:
        h0r = h0i = jnp.zeros((nb, ns), F32)
        zinit = jnp.zeros((nb, SUBLANES, w_conv), F32)
    u, yc, ztail, k, v, qt, vt = _in_proj_call(x3, mod3, zinit, lw["wrm"], lw["wt"], lw["kg2"], lw["cw"],
                                               lw["ogc"], lw["gq"], widths)
    ys, hre, him = _ssm_call(u, h0r, h0i, lw["fb"], scan, lw["cb"], lw["dskip"], lw["wglu"], lw["ogs"], r)

    kvw = k.shape[1]
    if sample:
        n_buf = cache_k.shape[1]
        kc = cache_k.reshape(nb * n_buf, kvw)
        vtc = cache_v.reshape(nb * n_buf, kvw).T.astype(vt.dtype)
    else:
        kc, vtc = k, vt
    yt = _attn_call(qt, k, vt, kc, vtc, bias, lw["sink"], lw["oga"], sample, r)

    x3 = _out_proj_call(x3, mod3, ys, yt, yc, lw["wo"])
    x3 = _ffn_call(x3, mod3, lw["w1"], lw["w2"], tm=1024)

    keep = min(r, WINDOW)
    k4 = k.reshape(nb, r, kvw)[:, r - keep:].reshape(nb, keep, N_KV_HEADS, HEAD_DIM)
    v4 = v.reshape(nb, r, kvw)[:, r - keep:].reshape(nb, keep, N_KV_HEADS, HEAD_DIM)
    if sample:
        new_k = jnp.concatenate([cache_k, k4], axis=1)[:, -n_buf:]
        new_v = jnp.concatenate([cache_v, v4], axis=1)[:, -n_buf:]
    else:
        new_k, new_v = k4, v4
    g = ns // SSM_STATE
    return (x3, new_k, new_v, hre.reshape(nb, g, SSM_STATE), him.reshape(nb, g, SSM_STATE),
            ztail[:, -(CONV_K - 1):])


def _layer_weights(scan_segs, w_in, ssm_a_re, ssm_a_im, ssm_log_dt, ssm_b_re, ssm_b_im, ssm_c_re, ssm_c_im, ssm_d,
                   ssm_w_glu, q_norm_g, k_norm_g, attn_sinks, conv_w, out_norm_g, w_out, w_ff1, w_ff2):
    w_ssm = ssm_d.shape[0]
    n_heads = attn_sinks.shape[0]
    w_attn = n_heads * HEAD_DIM
    w_kv = N_KV_HEADS * HEAD_DIM
    w_conv = conv_w.shape[0]
    qpg = n_heads // N_KV_HEADS
    o = np.cumsum([0, w_ssm, w_attn, w_kv, w_kv, w_conv, w_conv, w_conv])
    wu, wq, wk, wv, wgb, wgc, wxc = (w_in[:, o[i]:o[i + 1]] for i in range(7))
    fb, abar, cb = _ssm_consts(ssm_a_re, ssm_a_im, ssm_log_dt, ssm_b_re, ssm_b_im, ssm_c_re, ssm_c_im)
    og = out_norm_g.astype(F32)
    return {
        "wrm": jnp.concatenate([wgb, wgc, wxc, wu, wk, wv], axis=1).astype(BF16),
        "wt": jnp.concatenate([wq, wv], axis=1).T.astype(BF16),
        "kg2": jnp.tile(k_norm_g.astype(F32), N_KV_HEADS)[None, :],
        "fb": fb, "cb": cb,
        "scan": [_scan_consts(abar, SSM_TILE // SUBLANES, segs) for segs in scan_segs],
        "dskip": ssm_d.astype(F32)[None, :],
        "wglu": ssm_w_glu.astype(BF16),
        "gq": jnp.broadcast_to((q_norm_g.astype(F32) * (HEAD_DIM ** -0.5 * LOG2E))[:, None], (HEAD_DIM, TOK_TILE)),
        "sink": jnp.broadcast_to((attn_sinks.astype(F32) * LOG2E).reshape(N_KV_HEADS, 1, qpg, 1),
                                 (N_KV_HEADS, 1, qpg, PAIR)).reshape(N_KV_HEADS, 1, qpg * PAIR),
        "cw": conv_w.astype(F32).T,
        "ogs": og[None, :w_ssm],
        "oga": jnp.broadcast_to(og[w_ssm:w_ssm + w_attn].reshape(N_KV_HEADS, qpg, HEAD_DIM, 1).transpose(0, 2, 1, 3),
                                (N_KV_HEADS, HEAD_DIM, qpg, PAIR)).reshape(N_KV_HEADS, HEAD_DIM, qpg * PAIR),
        "ogc": og[None, w_ssm + w_attn:],
        "wo": w_out.astype(BF16),
        "w1": w_ff1.astype(BF16),
        "w2": w_ff2.astype(BF16),
    }


def kernel(x_prompt, x_sample, cache_k, cache_v, state_ssm_re, state_ssm_im, state_conv, c_prompt, c_sample, rel_bias, w_ada, b_ada, w_in, ssm_a_re, ssm_a_im, ssm_log_dt, ssm_b_re, ssm_b_im, ssm_c_re, ssm_c_im, ssm_d, ssm_w_glu, q_norm_g, k_norm_g, attn_sinks, conv_w, out_norm_g, w_out, w_ff1, w_ff2):
    depth = w_in.shape[0]
    nbp, nbs = x_prompt.shape[0], x_sample.shape[0]
    assert x_sample.shape[1] == CHUNK and cache_k.shape[2] == WINDOW and nbs % 2 == 0

    nc = nbp + nbs
    ncp = -(-nc // SUBLANES) * SUBLANES
    c_all = jnp.pad(jnp.concatenate([c_prompt, c_sample], axis=0), ((0, ncp - nc), (0, 0)))
    mod = _mod_call(c_all, w_ada, b_ada)

    bias_p = _bias_call(rel_bias, sample=False)
    bias_s = _bias_call(rel_bias, sample=True)

    w_ssm, w_conv, w_kv = ssm_d.shape[1], conv_w.shape[1], N_KV_HEADS * HEAD_DIM
    widths = (w_ssm, w_conv, w_kv, attn_sinks.shape[1] * HEAD_DIM)
    scan_segs = tuple(SUBLANES // max(SSM_TILE // x.shape[1], 1) for x in (x_prompt, x_sample))
    weights = jax.vmap(functools.partial(_layer_weights, scan_segs))(
        w_in, ssm_a_re, ssm_a_im, ssm_log_dt, ssm_b_re, ssm_b_im, ssm_c_re, ssm_c_im, ssm_d, ssm_w_glu,
        q_norm_g, k_norm_g, attn_sinks, conv_w, out_norm_g, w_out, w_ff1, w_ff2)

    xp, xs = x_prompt, x_sample
    outs_p, outs_s = [], []
    for l in range(depth):
        lw = jax.tree.map(lambda a: a[l], weights)
        lw["widths"] = widths
        mod_p = mod[l, :nbp][:, None, :]
        mod_s = mod[l, nbp:nc][:, None, :]
        xp, *rest_p = _layer(xp, mod_p, lw, bias_p, False, None)
        xs, *rest_s = _layer(xs, mod_s, lw, bias_s, True,
                             (cache_k[l], cache_v[l], state_ssm_re[l], state_ssm_im[l], state_conv[l]))
        outs_p.append(rest_p)
        outs_s.append(rest_s)
    stack = lambda outs, i: jnp.stack([o[i] for o in outs])
    return (xp, xs,
            *(stack(outs_p, i) for i in range(5)),
            *(stack(outs_s, i) for i in range(5)))
```

```python
import functools
import math

import numpy as np
import jax
import jax.numpy as jnp
from jax import lax
from jax.experimental import pallas as pl
from jax.experimental.pallas import tpu as pltpu

F32 = jnp.float32
BF16 = jnp.bfloat16

CHUNK = 64
SSM_GROUP = 16
SSM_STATE = 64
HEAD_DIM = 64
N_KV_HEADS = 2
WINDOW = 128
CONV_K = 3
REL_BUCKETS = 32
REL_MAX_DIST = 64
EPS = 1e-6
NEG_INF = -1e30
LOG2E = math.log2(math.e)

LANES = 128
SUBLANES = 8
PAIR = 2 * CHUNK
VMEM_LIMIT_BYTES = 60 * 1024 * 1024


def _params(*sem):
    return pltpu.CompilerParams(dimension_semantics=sem, vmem_limit_bytes=VMEM_LIMIT_BYTES)


def _const_spec(shape):
    nd = len(shape)
    return pl.BlockSpec(shape, lambda *_: (0,) * nd, pipeline_mode=pl.Buffered(1))


def _tok_blocks(nb, r, tm):
    if r >= tm:
        assert r % tm == 0
        return 1, tm
    bb = min(tm // r, nb)
    assert nb % bb == 0
    return bb, r


def _mod_kernel(c_ref, w_ref, b_ref, o_ref):
    c = c_ref[...]
    s = (c * jax.nn.sigmoid(c)).astype(BF16)
    o_ref[...] = jnp.dot(s, w_ref[...].astype(BF16), preferred_element_type=F32) + b_ref[...]


def _mod_call(c_all, w_ada, b_ada):
    depth, d, n = w_ada.shape
    nc = c_all.shape[0]
    tn = 1024
    return pl.pallas_call(
        _mod_kernel,
        grid=(depth, n // tn),
        in_specs=[pl.BlockSpec((nc, d), lambda l, j: (0, 0)),
                  pl.BlockSpec((None, d, tn), lambda l, j: (l, 0, j)),
                  pl.BlockSpec((None, 1, tn), lambda l, j: (l, 0, j))],
        out_specs=pl.BlockSpec((None, nc, tn), lambda l, j: (l, 0, j)),
        out_shape=jax.ShapeDtypeStruct((depth, nc, n), F32),
        compiler_params=_params("arbitrary", "arbitrary"),
    )(c_all, w_ada, b_ada.reshape(depth, 1, n))


def _bucket_maps(sample):
    lane = np.arange(PAIR)[None, :]
    if sample:
        nk = 2 * WINDOW + PAIR
        row = np.arange(nk)[:, None]
        own = row >= 2 * WINDOW
        key_b = np.where(own, (row - 2 * WINDOW) // CHUNK, row // WINDOW)
        key_s = np.where(own, WINDOW + (row - 2 * WINDOW) % CHUNK, row % WINDOW)
        rel = key_s - WINDOW - lane % CHUNK
        visible = key_b == lane // CHUNK
    else:
        nk = WINDOW + PAIR
        row = np.arange(nk)[:, None]
        rel = row - WINDOW - lane
        dc = row // CHUNK - lane // CHUNK
        visible = (dc >= 0) & (dc <= WINDOW // CHUNK)
    half = REL_BUCKETS // 2
    exact = half // 2
    n = np.abs(rel)
    nf = np.maximum(n, 1).astype(np.float32)
    far = exact + (np.log(nf / np.float32(exact)) / np.float32(math.log(REL_MAX_DIST / exact))
                   * np.float32(half - exact)).astype(np.int32)
    far = np.minimum(far, half - 1)
    bucket = np.where(rel > 0, half, 0) + np.where(n < exact, n, far)
    full = np.where(visible, bucket, -1).astype(np.int32)
    first = np.where(row >= WINDOW, full, -1).astype(np.int32)
    return np.stack([full, first])


def _bias_kernel(table_ref, bucket_ref, o_ref):
    h = pl.program_id(1)
    bucket = bucket_ref[...]
    acc = jnp.full(bucket.shape, NEG_INF, F32)
    for b in range(REL_BUCKETS):
        acc = jnp.where(bucket == b, table_ref[b, h] * LOG2E, acc)
    o_ref[...] = acc


def _bias_call(rel_bias, sample):
    buckets = jnp.asarray(_bucket_maps(sample))
    _, nk, _ = buckets.shape
    n_heads = rel_bias.shape[1]
    qpg = n_heads // N_KV_HEADS
    return pl.pallas_call(
        _bias_kernel,
        grid=(2, n_heads),
        in_specs=[pl.BlockSpec(memory_space=pltpu.SMEM),
                  pl.BlockSpec((None, nk, PAIR), lambda s, h: (s, 0, 0))],
        out_specs=pl.BlockSpec((None, None, nk, PAIR), lambda s, h: (s, h // qpg, 0, h % qpg)),
        out_shape=jax.ShapeDtypeStruct((2, N_KV_HEADS, nk, qpg * PAIR), F32),
        compiler_params=_params("arbitrary", "arbitrary"),
    )(rel_bias, buckets)


def _rms(y):
    return y * lax.rsqrt(jnp.mean(y * y, axis=-1, keepdims=True) + EPS)


def _in_proj_kernel(x_ref, shift_ref, scale_ref, zinit_ref, wrm_ref, wt_ref, kg_ref, cw_ref, ogc_ref, gq_ref,
                    u_ref, yc_ref, zt_ref, k_ref, v_ref, qt_ref, vt_ref, zprev, *, widths):
    bb, tr, d = x_ref.shape

    @pl.when(pl.program_id(1) == 0)
    def _():
        zprev[...] = zinit_ref[...]

    x = x_ref[...]
    ms = jnp.mean(x * x, axis=-1, keepdims=True)
    h = x * lax.rsqrt(ms + EPS) * (1.0 + scale_ref[...]) + shift_ref[...]
    h = h.reshape(bb * tr, d).astype(BF16)

    w_ssm, w_conv, w_kv, w_attn = widths

    pt = lax.dot_general(wt_ref[...], h, (((1,), (1,)), ((), ())), preferred_element_type=F32)
    gq = gq_ref[...]
    for hd in range(w_attn // HEAD_DIM):
        q = pt[hd * HEAD_DIM:(hd + 1) * HEAD_DIM]
        ms = jnp.sum(q * q, axis=0, keepdims=True) * (1.0 / HEAD_DIM)
        qt_ref[hd * HEAD_DIM:(hd + 1) * HEAD_DIM, :] = (q * lax.rsqrt(ms + EPS) * gq).astype(qt_ref.dtype)
    vt_ref[...] = pt[w_attn:].astype(vt_ref.dtype)

    pc = jnp.dot(h, wrm_ref[:, 0:3 * w_conv], preferred_element_type=F32)
    gb = pc[:, 0:w_conv].reshape(bb, tr, w_conv)
    z = (pc[:, w_conv:2 * w_conv] * pc[:, 2 * w_conv:3 * w_conv]).reshape(bb, tr, w_conv)

    zp = jnp.concatenate([zprev[...], z], axis=1)
    ztail = z[:, tr - SUBLANES:, :]
    zprev[...] = ztail
    zt_ref[...] = ztail
    cw = cw_ref[...]

    def conv_part(c, n):
        if bb == 1:
            rc = tr // n
            zs, gs = zp[:, c * rc:c * rc + rc + SUBLANES], gb[:, c * rc:(c + 1) * rc]
        else:
            bc, rc = bb // n, tr
            zs, gs = zp[c * bc:(c + 1) * bc], gb[c * bc:(c + 1) * bc]
        conv = (zs[:, SUBLANES - 2:SUBLANES - 2 + rc] * cw[0:1] + zs[:, SUBLANES - 1:SUBLANES - 1 + rc] * cw[1:2]
                + zs[:, SUBLANES:] * cw[2:3])
        y = (_rms(gs * conv) * ogc_ref[...]).astype(yc_ref.dtype)
        rows = y.shape[0] * y.shape[1]
        yc_ref[c * rows:(c + 1) * rows, :] = y.reshape(rows, w_conv)

    n_part = 4
    assert (tr if bb == 1 else bb) % n_part == 0
    col = 3 * w_conv
    for c, width in enumerate((w_ssm // 2, w_ssm // 2, 2 * w_kv)):
        conv_part(c, n_part)
        pp = jnp.dot(h, wrm_ref[:, col:col + width], preferred_element_type=F32)
        col += width
        if c < 2:
            u_ref[:, c * width:(c + 1) * width] = pp
        else:
            k = pp[:, :w_kv]
            v_ref[...] = pp[:, w_kv:]
    conv_part(n_part - 1, n_part)

    lo = lax.broadcasted_iota(jnp.int32, k.shape, 1) < HEAD_DIM
    k2 = k * k
    s_lo = jnp.sum(jnp.where(lo, k2, 0.0), axis=-1, keepdims=True)
    s_hi = jnp.sum(jnp.where(lo, 0.0, k2), axis=-1, keepdims=True)
    ssq = jnp.where(lo, s_lo, s_hi)
    k_ref[...] = k * lax.rsqrt(ssq * (1.0 / HEAD_DIM) + EPS) * kg_ref[...]


TOK_TILE = 512


def _in_proj_call(x3, mod3, zinit, wrm, wt, kg2, cw, ogc, gq, widths):
    nb, r, d = x3.shape
    w_ssm, w_conv, w_kv, w_attn = widths
    bb, tr = _tok_blocks(nb, r, TOK_TILE)
    tmm = bb * tr
    assert tmm == TOK_TILE
    nt = r // tr
    ttot = nb * r
    tok = lambda b, t: (b * nt + t, 0)
    tok_t = lambda b, t: (b * nt + t, 0, 0)
    halo_spec = pl.BlockSpec((bb, SUBLANES, w_conv), lambda b, t: (b, 0, 0))
    out_shape = [jax.ShapeDtypeStruct((ttot, w_ssm), F32), jax.ShapeDtypeStruct((ttot, w_conv), BF16),
                 jax.ShapeDtypeStruct((nb, SUBLANES, w_conv), F32),
                 jax.ShapeDtypeStruct((ttot, w_kv), F32), jax.ShapeDtypeStruct((ttot, w_kv), F32),
                 jax.ShapeDtypeStruct((ttot // tmm, w_attn, tmm), BF16),
                 jax.ShapeDtypeStruct((ttot // tmm, w_kv, tmm), BF16)]
    out_specs = [pl.BlockSpec((tmm, w_ssm), tok), pl.BlockSpec((tmm, w_conv), tok), halo_spec,
                 pl.BlockSpec((tmm, w_kv), tok), pl.BlockSpec((tmm, w_kv), tok),
                 pl.BlockSpec((None, w_attn, tmm), tok_t), pl.BlockSpec((None, w_kv, tmm), tok_t)]
    consts = (wrm, wt, kg2, cw, ogc, gq)
    return pl.pallas_call(
        functools.partial(_in_proj_kernel, widths=widths),
        grid=(nb // bb, nt),
        in_specs=[pl.BlockSpec((bb, tr, d), lambda b, t: (b, t, 0)),
                  pl.BlockSpec((bb, 1, d), lambda b, t: (b, 0, 0)),
                  pl.BlockSpec((bb, 1, d), lambda b, t: (b, 0, 1)),
                  halo_spec] + [_const_spec(c.shape) for c in consts],
        out_specs=out_specs,
        out_shape=out_shape,
        scratch_shapes=[pltpu.VMEM((bb, SUBLANES, w_conv), F32)],
        compiler_params=_params("arbitrary", "arbitrary"),
    )(x3, mod3, mod3, zinit, *consts)


SCAN_LANE_BLOCKS = 8


def _gelu_tanh(x):
    return 0.5 * x * (1.0 + jnp.tanh(math.sqrt(2.0 / math.pi) * (x + 0.044715 * (x * x * x))))


def _segment_perm(ts):
    lseg = ts // SUBLANES
    p = np.zeros((ts, ts), np.float32)
    i, j = np.meshgrid(np.arange(lseg), np.arange(SUBLANES), indexing="ij")
    p[(i * SUBLANES + j).ravel(), (j * lseg + i).ravel()] = 1.0
    return p


def _ssm_kernel(u_ref, h0r_ref, h0i_ref, perm_ref, permt_ref, fb_ref, ab_ref, sg_ref, pw_ref, cb_ref, dskip_ref,
                wglu_ref, og_ref, y_ref, hre_ref, him_ref, bre, bim, car, *, segs_per_seq):
    ts, w = u_ref.shape
    ns = bre.shape[1]
    lseg = ts // SUBLANES
    half_u = w // 2
    half_s = ns // 2
    t = pl.program_id(1)

    @pl.when(t == 0)
    def _():
        car[0:SUBLANES, :] = h0r_ref[...]
        car[SUBLANES:, :] = h0i_ref[...]

    u = u_ref[...]
    up = jnp.dot(perm_ref[...], u.astype(BF16), preferred_element_type=F32).astype(BF16)
    for j in range(2):
        bb = jnp.dot(up[:, j * half_u:(j + 1) * half_u], fb_ref[j], preferred_element_type=F32)
        bre[:, j * half_s:(j + 1) * half_s] = bb[:, :half_s]
        bim[:, j * half_s:(j + 1) * half_s] = bb[:, half_s:]

    seq_start = lax.broadcasted_iota(jnp.int32, (SUBLANES, LANES), 0) % segs_per_seq == 0
    zero = jnp.zeros((SUBLANES, LANES), F32)
    nblk = SCAN_LANE_BLOCKS
    for c0 in range(0, ns // LANES, nblk):
        sls = [pl.ds((c0 + i) * LANES, LANES) for i in range(nblk)]
        ab = [(ab_ref[0, :, sl], ab_ref[1, :, sl]) for sl in sls]

        def pass1(i, carry, sls=sls, ab=ab):
            rows = pl.ds(pl.multiple_of(i * SUBLANES, SUBLANES), SUBLANES)
            new = []
            for sl, (ar, ai), (hr, hi) in zip(sls, ab, carry):
                hr, hi = ar * hr - ai * hi + bre[rows, sl], ar * hi + ai * hr + bim[rows, sl]
                bre[rows, sl] = hr
                bim[rows, sl] = hi
                new.append((hr, hi))
            return tuple(new)

        ends = lax.fori_loop(0, lseg, pass1, tuple((zero, zero) for _ in sls), unroll=True)

        starts = []
        for sl, (er, ei) in zip(sls, ends):
            xr = jnp.where(seq_start, car[0:SUBLANES, sl], pltpu.roll(er, 1, axis=0))
            xi = jnp.where(seq_start, car[SUBLANES:, sl], pltpu.roll(ei, 1, axis=0))
            for k in range(segs_per_seq.bit_length() - 1):
                sr = pltpu.roll(xr, 1 << k, axis=0)
                si = pltpu.roll(xi, 1 << k, axis=0)
                mr, mi = sg_ref[2 * k, :, sl], sg_ref[2 * k + 1, :, sl]
                xr, xi = xr + (mr * sr - mi * si), xi + (mr * si + mi * sr)
            lr, li = sg_ref[6, :, sl], sg_ref[7, :, sl]
            nr = lr * xr - li * xi + er
            ni = lr * xi + li * xr + ei
            hre_ref[:, sl] = nr
            him_ref[:, sl] = ni
            car[0:SUBLANES, sl] = jnp.broadcast_to(nr[SUBLANES - 1:SUBLANES, :], (SUBLANES, LANES))
            car[SUBLANES:, sl] = jnp.broadcast_to(ni[SUBLANES - 1:SUBLANES, :], (SUBLANES, LANES))
            starts.append((xr, xi))

        def pass2(i, c, sls=sls, starts=starts):
            rows = pl.ds(pl.multiple_of(i * SUBLANES, SUBLANES), SUBLANES)
            for sl, (sr, si) in zip(sls, starts):
                pr = pw_ref[0, rows, sl]
                pi = pw_ref[1, rows, sl]
                bre[rows, sl] += pr * sr - pi * si
                bim[rows, sl] += pr * si + pi * sr
            return c

        lax.fori_loop(0, lseg, pass2, 0, unroll=True)

    ys = []
    for j in range(2):
        hcat = jnp.concatenate([bre[:, j * half_s:(j + 1) * half_s].astype(BF16),
                                bim[:, j * half_s:(j + 1) * half_s].astype(BF16)], axis=-1)
        ys.append(jnp.dot(hcat, cb_ref[j], preferred_element_type=F32))
    yp = jnp.concatenate(ys, axis=-1)
    p0 = yp.astype(BF16)
    r1 = yp - p0.astype(F32)
    p1 = r1.astype(BF16)
    p2 = (r1 - p1.astype(F32)).astype(BF16)
    parts = jnp.dot(permt_ref[...], jnp.concatenate([p0, p1, p2], axis=-1), preferred_element_type=F32)
    y = (parts[:, :w] + parts[:, w:2 * w]) + parts[:, 2 * w:] + dskip_ref[...] * u
    y = _gelu_tanh(y)
    y = y * jax.nn.sigmoid(jnp.dot(y.astype(BF16), wglu_ref[...], preferred_element_type=F32))
    y_ref[...] = (_rms(y) * og_ref[...]).astype(y_ref.dtype)


SSM_TILE = 512


def _ssm_call(u, h0r, h0i, fb, scan, cb, dskip, wglu, ogs, r):
    ttot, w = u.shape
    ab, sg, pw = scan
    ns = ab.shape[-1]
    ts = SSM_TILE
    lseg = ts // SUBLANES
    assert pw.shape[1] == ts
    nseq = ttot // r
    if r >= ts:
        assert r % ts == 0
        seq_per_tile, nt = 1, r // ts
    else:
        assert r % lseg == 0 and ts % r == 0 and nseq % (ts // r) == 0
        seq_per_tile, nt = ts // r, 1
    segs_per_seq = SUBLANES // seq_per_tile
    ngrp = nseq // seq_per_tile
    init = lambda h: jnp.repeat(h.reshape(ngrp, seq_per_tile, ns), segs_per_seq, axis=1)
    perm = _segment_perm(ts)
    perm, permt = jnp.asarray(perm, BF16), jnp.asarray(perm.T, BF16)
    st_spec = pl.BlockSpec((None, SUBLANES, ns), lambda b, t: (b, 0, 0))
    consts = (perm, permt, fb, ab, sg, pw, cb, dskip, wglu, ogs)
    y, hre, him = pl.pallas_call(
        functools.partial(_ssm_kernel, segs_per_seq=segs_per_seq),
        grid=(ngrp, nt),
        in_specs=[pl.BlockSpec((ts, w), lambda b, t: (b * nt + t, 0)), st_spec, st_spec]
        + [_const_spec(c.shape) for c in consts],
        out_specs=[pl.BlockSpec((ts, w), lambda b, t: (b * nt + t, 0)), st_spec, st_spec],
        out_shape=[jax.ShapeDtypeStruct((ttot, w), BF16),
                   jax.ShapeDtypeStruct((ngrp, SUBLANES, ns), F32), jax.ShapeDtypeStruct((ngrp, SUBLANES, ns), F32)],
        scratch_shapes=[pltpu.VMEM((ts, ns), F32), pltpu.VMEM((ts, ns), F32), pltpu.VMEM((2 * SUBLANES, ns), F32)],
        compiler_params=_params("arbitrary", "arbitrary"),
    )(u, init(h0r), init(h0i), *consts)
    last = lambda h: h[:, segs_per_seq - 1::segs_per_seq].reshape(nseq, ns)
    return y, last(hre), last(him)


def _scan_consts(a1, lseg, segs_per_seq):
    def cmul(x, y):
        return x[0] * y[0] - x[1] * y[1], x[0] * y[1] + x[1] * y[0]

    tr, ti = a1[0][None], a1[1][None]
    while tr.shape[0] < lseg:
        nr, ni = cmul((tr, ti), (tr[-1], ti[-1]))
        tr, ti = jnp.concatenate([tr, nr]), jnp.concatenate([ti, ni])
    pw = jnp.repeat(jnp.stack([tr[:lseg], ti[:lseg]]), SUBLANES, axis=1)
    ns = a1[0].shape[0]
    ab = jnp.stack([jnp.broadcast_to(a1[0], (SUBLANES, ns)), jnp.broadcast_to(a1[1], (SUBLANES, ns))])
    al = (tr[lseg - 1], ti[lseg - 1])
    al2 = cmul(al, al)
    al4 = cmul(al2, al2)
    row = jnp.arange(SUBLANES)[:, None]
    rows = []
    for k, a in enumerate((al, al2, al4)):
        keep = row % segs_per_seq >= (1 << k)
        rows += [jnp.where(keep, a[0][None, :], 0.0), jnp.where(keep, a[1][None, :], 0.0)]
    rows += [jnp.broadcast_to(al[0], (SUBLANES, ns)), jnp.broadcast_to(al[1], (SUBLANES, ns))]
    return ab, jnp.stack(rows), pw


def _ssm_consts(a_re, a_im, log_dt, b_re, b_im, c_re, c_im):
    g, p = a_re.shape
    hh = b_re.shape[-1]
    ar, ai = a_re.astype(F32), a_im.astype(F32)
    dt = jnp.exp(log_dt.astype(F32))[:, None]
    mag = jnp.exp(dt * ar)
    abar_re, abar_im = mag * jnp.cos(dt * ai), mag * jnp.sin(dt * ai)
    den = ar * ar + ai * ai
    f_re = ((abar_re - 1.0) * ar + abar_im * ai) / den
    f_im = (abar_im * ar - (abar_re - 1.0) * ai) / den
    fb_re = f_re[..., None] * b_re - f_im[..., None] * b_im
    fb_im = f_re[..., None] * b_im + f_im[..., None] * b_re
    gh = g // 2
    eye = jnp.eye(gh, dtype=F32)

    def in_blk(m):
        return jnp.einsum('gph,gk->ghkp', m, eye).reshape(gh * hh, gh * p)

    def out_blk(m):
        return jnp.einsum('ghp,gk->kpgh', m, eye).reshape(gh * p, gh * hh)

    fb = jnp.stack([jnp.concatenate([in_blk(fb_re[j * gh:(j + 1) * gh]), in_blk(fb_im[j * gh:(j + 1) * gh])], axis=1)
                    for j in range(2)]).astype(BF16)
    cb = jnp.stack([jnp.concatenate([out_blk(c_re[j * gh:(j + 1) * gh]), -out_blk(c_im[j * gh:(j + 1) * gh])], axis=0)
                    for j in range(2)]).astype(BF16)

    return fb, (abar_re.reshape(-1), abar_im.reshape(-1)), cb


def _attn_kernel(qt_ref, k_ref, vt_ref, kc_ref, vtc_ref, bias_ref, sink_ref, og_ref, yt_ref,
                 *, sample, pairs_per_seq):
    n_rows, tq = qt_ref.shape
    npairs = tq // PAIR
    qpg = n_rows // (N_KV_HEADS * HEAD_DIM)
    i = pl.program_id(0)
    if sample:
        k_all, vt_all = k_ref[...], vt_ref[...]
    else:
        k_all = jnp.concatenate([kc_ref[...], k_ref[...]], axis=0)
        vt_all = jnp.concatenate([vtc_ref[...], vt_ref[...]], axis=1)
    for p in range(npairs):
        if sample:
            kb = jnp.concatenate([kc_ref[p * 2 * WINDOW:(p + 1) * 2 * WINDOW, :],
                                  k_all[p * PAIR:(p + 1) * PAIR, :]], axis=0)
            vtb = jnp.concatenate([vtc_ref[:, p * 2 * WINDOW:(p + 1) * 2 * WINDOW],
                                   vt_all[:, p * PAIR:(p + 1) * PAIR]], axis=1)
            sel = 0
        else:
            kb = k_all[p * PAIR:p * PAIR + WINDOW + PAIR, :]
            vtb = vt_all[:, p * PAIR:p * PAIR + WINDOW + PAIR]
            sel = ((i * npairs + p) % pairs_per_seq == 0).astype(jnp.int32)
        kb = kb.astype(BF16)
        vtb = vtb.astype(BF16)
        ones = jnp.ones((2 * SUBLANES, vtb.shape[1]), BF16)
        outs = []
        ssq = jnp.zeros((1, PAIR), F32)
        qgs = [jnp.concatenate([qt_ref[(g * qpg + hh) * HEAD_DIM:(g * qpg + hh + 1) * HEAD_DIM,
                                       p * PAIR:(p + 1) * PAIR] for hh in range(qpg)], axis=1)
               for g in range(N_KV_HEADS)]
        zero = jnp.zeros_like(qgs[0])
        qblk = jnp.concatenate([jnp.concatenate([qgs[g] if gg == g else zero for gg in range(N_KV_HEADS)], axis=1)
                                for g in range(N_KV_HEADS)], axis=0)
        bias = jnp.concatenate([bias_ref[sel, g] for g in range(N_KV_HEADS)], axis=1)
        s = jnp.dot(kb, qblk, preferred_element_type=F32) + bias
        sink = jnp.concatenate([sink_ref[g] for g in range(N_KV_HEADS)], axis=1)
        m = jnp.maximum(jnp.max(s, axis=0, keepdims=True), sink)
        e = jnp.exp2(s - m).astype(BF16)
        esink = jnp.exp2(sink - m)
        for g in range(N_KV_HEADS):
            lanes = slice(g * qpg * PAIR, (g + 1) * qpg * PAIR)
            va = jnp.concatenate([vtb[g * HEAD_DIM:(g + 1) * HEAD_DIM, :], ones], axis=0)
            oa = jnp.dot(va, e[:, lanes], preferred_element_type=F32)
            den = oa[HEAD_DIM:HEAD_DIM + 1, :] + esink[:, lanes]
            o = oa[:HEAD_DIM, :] / den
            outs.append(o)
            sq = jnp.sum(o * o, axis=0, keepdims=True)
            for hh in range(qpg):
                ssq = ssq + sq[:, hh * PAIR:(hh + 1) * PAIR]
        rn = lax.rsqrt(ssq * (1.0 / n_rows) + EPS)
        rn = jnp.concatenate([rn] * qpg, axis=1)
        for g in range(N_KV_HEADS):
            o = outs[g] * rn * og_ref[g]
            for hh in range(qpg):
                r0 = (g * qpg + hh) * HEAD_DIM
                yt_ref[r0:r0 + HEAD_DIM, p * PAIR:(p + 1) * PAIR] = \
                    o[:, hh * PAIR:(hh + 1) * PAIR].astype(yt_ref.dtype)


def _attn_call(qt, k, vt, kc, vtc, bias, sink, oga, sample, seq_len):
    ntile, n_rows, tq = qt.shape
    kvw = k.shape[1]
    assert sample or seq_len % tq == 0
    npairs = tq // PAIR
    tile3 = lambda i: (i, 0, 0)
    if sample:
        ctx = 2 * WINDOW * npairs
        kc_spec = pl.BlockSpec((ctx, kvw), lambda i: (i, 0))
        vtc_spec = pl.BlockSpec((kvw, ctx), lambda i: (0, i))
    else:
        kc_spec = pl.BlockSpec((WINDOW, kvw), lambda i: (jnp.maximum(i * npairs - 1, 0), 0))
        vtc_spec = pl.BlockSpec((None, kvw, WINDOW), lambda i: (jnp.maximum(i - 1, 0), 0, tq // WINDOW - 1))
    return pl.pallas_call(
        functools.partial(_attn_kernel, sample=sample, pairs_per_seq=seq_len // PAIR),
        grid=(ntile,),
        in_specs=[pl.BlockSpec((None, n_rows, tq), tile3),
                  pl.BlockSpec((tq, kvw), lambda i: (i, 0)),
                  pl.BlockSpec((None, kvw, tq), tile3),
                  kc_spec, vtc_spec,
                  _const_spec(bias.shape), _const_spec(sink.shape),
                  _const_spec(oga.shape)],
        out_specs=pl.BlockSpec((None, n_rows, tq), tile3),
        out_shape=jax.ShapeDtypeStruct((ntile, n_rows, tq), BF16),
        compiler_params=_params("arbitrary"),
    )(qt, k, vt, kc, vtc, bias, sink, oga)


def _out_proj_kernel(x_ref, gate_ref, ys_ref, yt_ref, yc_ref, wo_ref, xo_ref):
    bb, tr, d = x_ref.shape
    ws, wa = ys_ref.shape[1], yt_ref.shape[0]
    o = jnp.dot(ys_ref[...], wo_ref[0:ws, :], preferred_element_type=F32)
    o += lax.dot_general(yt_ref[...], wo_ref[ws:ws + wa, :], (((0,), (0,)), ((), ())),
                         preferred_element_type=F32)
    o += jnp.dot(yc_ref[...], wo_ref[ws + wa:, :], preferred_element_type=F32)
    xo_ref[...] = x_ref[...] + gate_ref[...] * o.reshape(bb, tr, d)


def _out_proj_call(x3, mod3, ys, yt, yc, wo):
    nb, r, d = x3.shape
    ws, wa, wc = ys.shape[1], yt.shape[1], yc.shape[1]
    bb, tr = _tok_blocks(nb, r, TOK_TILE)
    tmm = bb * tr
    assert tmm == yt.shape[2]
    nt = r // tr
    tok3 = lambda b, t: (b, t, 0)
    return pl.pallas_call(
        _out_proj_kernel,
        grid=(nb // bb, nt),
        in_specs=[pl.BlockSpec((bb, tr, d), tok3),
                  pl.BlockSpec((bb, 1, d), lambda b, t: (b, 0, 2)),
                  pl.BlockSpec((tmm, ws), lambda b, t: (b * nt + t, 0)),
                  pl.BlockSpec((None, wa, tmm), lambda b, t: (b * nt + t, 0, 0)),
                  pl.BlockSpec((tmm, wc), lambda b, t: (b * nt + t, 0)),
                  _const_spec(wo.shape)],
        out_specs=pl.BlockSpec((bb, tr, d), tok3),
        out_shape=jax.ShapeDtypeStruct((nb, r, d), F32),
        compiler_params=_params("arbitrary", "arbitrary"),
    )(x3, mod3, ys, yt, yc, wo)


FFN_TILE = 1024
NORM_ROWS = 64


def _ffn_kernel(x_ref, shift_ref, scale_ref, gate_ref, w1_hbm, w2_hbm, o_ref, h_s, r_s, w1_buf, w2_buf, sem):
    bb, tr, d = x_ref.shape
    tf = w1_buf.shape[2]
    nj = w1_hbm.shape[1] // tf
    step = pl.program_id(0) * pl.num_programs(1) + pl.program_id(1)
    n_steps = pl.num_programs(0) * pl.num_programs(1)

    def w_copies(j, slot):
        cols = pl.ds(pl.multiple_of(j * tf, tf), tf)
        return (pltpu.make_async_copy(w1_hbm.at[:, cols], w1_buf.at[slot], sem.at[0, slot]),
                pltpu.make_async_copy(w2_hbm.at[cols, :], w2_buf.at[slot], sem.at[1, slot]))

    @pl.when(step == 0)
    def _():
        for c in w_copies(0, 0):
            c.start()

    per = tr // NORM_ROWS

    def chunk(c):
        return c // per, pl.ds(pl.multiple_of((c % per) * NORM_ROWS, NORM_ROWS), NORM_ROWS)

    def scales(c, carry):
        b, rows = chunk(c)
        x = x_ref[b, rows, :]
        r = lax.rsqrt(jnp.mean(x * x, axis=-1, keepdims=True) + EPS)
        r_s[pl.ds(pl.multiple_of(c * NORM_ROWS, NORM_ROWS), NORM_ROWS), :] = jnp.broadcast_to(r, (NORM_ROWS, LANES))
        o_ref[b, rows, :] = jnp.zeros((NORM_ROWS, d), F32)
        return carry

    def rows_out(c, carry):
        b, rows = chunk(c)
        flat = pl.ds(pl.multiple_of(c * NORM_ROWS, NORM_ROWS), NORM_ROWS)
        r = jnp.concatenate([r_s[flat, :]] * (d // LANES), axis=1)
        h = x_ref[b, rows, :] * r * (1.0 + scale_ref[b]) + shift_ref[b]
        h_s[flat, :] = h.astype(BF16)
        return carry

    lax.fori_loop(0, bb * per, scales, 0, unroll=4)
    lax.fori_loop(0, bb * per, rows_out, 0)

    def tile_pair(jj, carry):
        for slot in (0, 1):
            j = 2 * jj + slot
            for c in w_copies(j, slot):
                c.wait()

            @pl.when(j + 1 < nj)
            def _():
                for c in w_copies(j + 1, 1 - slot):
                    c.start()

            if slot == 1:
                @pl.when(jnp.logical_and(j + 1 == nj, step + 1 < n_steps))
                def _():
                    for c in w_copies(0, 0):
                        c.start()

            a = jnp.maximum(jnp.dot(h_s[...], w1_buf[slot], preferred_element_type=F32), 0.0)
            o_ref[...] += jnp.dot((a * a).astype(BF16), w2_buf[slot], preferred_element_type=F32).reshape(bb, tr, d)
        return carry

    lax.fori_loop(0, nj // 2, tile_pair, 0)
    o_ref[...] = x_ref[...] + gate_ref[...] * o_ref[...]


def _ffn_call(x3, mod3, w1, w2, tm):
    nb, r, d = x3.shape
    tf = FFN_TILE
    assert w1.shape[1] % (2 * tf) == 0
    bb, tr = _tok_blocks(nb, r, tm)
    tok3 = lambda b, t: (b, t, 0)
    mod_spec = lambda c: pl.BlockSpec((bb, 1, d), lambda b, t: (b, 0, c))
    return pl.pallas_call(
        _ffn_kernel,
        grid=(nb // bb, r // tr),
        in_specs=[pl.BlockSpec((bb, tr, d), tok3), mod_spec(3), mod_spec(4), mod_spec(5),
                  pl.BlockSpec(memory_space=pl.ANY), pl.BlockSpec(memory_space=pl.ANY)],
        out_specs=pl.BlockSpec((bb, tr, d), tok3),
        out_shape=jax.ShapeDtypeStruct((nb, r, d), F32),
        scratch_shapes=[pltpu.VMEM((bb * tr, d), BF16), pltpu.VMEM((bb * tr, LANES), F32),
                        pltpu.VMEM((2, d, tf), BF16), pltpu.VMEM((2, tf, d), BF16),
                        pltpu.SemaphoreType.DMA((2, 2))],
        compiler_params=_params("arbitrary", "arbitrary"),
    )(x3, mod3, mod3, mod3, w1, w2)


def _layer(x3, mod3, lw, bias, sample, state):
    nb, r, d = x3.shape
    widths = lw["widths"]
    w_conv = widths[1]
    scan = lw["scan"][1 if sample else 0]
    ns = scan[0].shape[-1]
    if sample:
        cache_k, cache_v, h0r, h0i, conv_buf = state
        h0r = h0r.reshape(nb, ns)
        h0i = h0i.reshape(nb, ns)
        zinit = jnp.pad(conv_buf, ((0, 0), (SUBLANES - (CONV_K - 1), 0), (0, 0)))
    else:
        h0r = h0i = jnp.zeros((nb, ns), F32)
        zinit = jnp.zeros((nb, SUBLANES, w_conv), F32)
    u, yc, ztail, k, v, qt, vt = _in_proj_call(x3, mod3, zinit, lw["wrm"], lw["wt"], lw["kg2"], lw["cw"],
                                               lw["ogc"], lw["gq"], widths)
    ys, hre, him = _ssm_call(u, h0r, h0i, lw["fb"], scan, lw["cb"], lw["dskip"], lw["wglu"], lw["ogs"], r)

    kvw = k.shape[1]
    if sample:
        n_buf = cache_k.shape[1]
        kc = cache_k.reshape(nb * n_buf, kvw)
        vtc = cache_v.reshape(nb * n_buf, kvw).T.astype(vt.dtype)
    else:
        kc, vtc = k, vt
    yt = _attn_call(qt, k, vt, kc, vtc, bias, lw["sink"], lw["oga"], sample, r)

    x3 = _out_proj_call(x3, mod3, ys, yt, yc, lw["wo"])
    x3 = _ffn_call(x3, mod3, lw["w1"], lw["w2"], tm=1024)

    keep = min(r, WINDOW)
    k4 = k.reshape(nb, r, kvw)[:, r - keep:].reshape(nb, keep, N_KV_HEADS, HEAD_DIM)
    v4 = v.reshape(nb, r, kvw)[:, r - keep:].reshape(nb, keep, N_KV_HEADS, HEAD_DIM)
    if sample:
        new_k = jnp.concatenate([cache_k, k4], axis=1)[:, -n_buf:]
        new_v = jnp.concatenate([cache_v, v4], axis=1)[:, -n_buf:]
    else:
        new_k, new_v = k4, v4
    g = ns // SSM_STATE
    return (x3, new_k, new_v, hre.reshape(nb, g, SSM_STATE), him.reshape(nb, g, SSM_STATE),
            ztail[:, -(CONV_K - 1):])


def _layer_weights(scan_segs, w_in, ssm_a_re, ssm_a_im, ssm_log_dt, ssm_b_re, ssm_b_im, ssm_c_re, ssm_c_im, ssm_d,
                   ssm_w_glu, q_norm_g, k_norm_g, attn_sinks, conv_w, out_norm_g, w_out, w_ff1, w_ff2):
    w_ssm = ssm_d.shape[0]
    n_heads = attn_sinks.shape[0]
    w_attn = n_heads * HEAD_DIM
    w_kv = N_KV_HEADS * HEAD_DIM
    w_conv = conv_w.shape[0]
    qpg = n_heads // N_KV_HEADS
    o = np.cumsum([0, w_ssm, w_attn, w_kv, w_kv, w_conv, w_conv, w_conv])
    wu, wq, wk, wv, wgb, wgc, wxc = (w_in[:, o[i]:o[i + 1]] for i in range(7))
    fb, abar, cb = _ssm_consts(ssm_a_re, ssm_a_im, ssm_log_dt, ssm_b_re, ssm_b_im, ssm_c_re, ssm_c_im)
    og = out_norm_g.astype(F32)
    return {
        "wrm": jnp.concatenate([wgb, wgc, wxc, wu, wk, wv], axis=1).astype(BF16),
        "wt": jnp.concatenate([wq, wv], axis=1).T.astype(BF16),
        "kg2": jnp.tile(k_norm_g.astype(F32), N_KV_HEADS)[None, :],
        "fb": fb, "cb": cb,
        "scan": [_scan_consts(abar, SSM_TILE // SUBLANES, segs) for segs in scan_segs],
        "dskip": ssm_d.astype(F32)[None, :],
        "wglu": ssm_w_glu.astype(BF16),
        "gq": jnp.broadcast_to((q_norm_g.astype(F32) * (HEAD_DIM ** -0.5 * LOG2E))[:, None], (HEAD_DIM, TOK_TILE)),
        "sink": jnp.broadcast_to((attn_sinks.astype(F32) * LOG2E).reshape(N_KV_HEADS, 1, qpg, 1),
                                 (N_KV_HEADS, 1, qpg, PAIR)).reshape(N_KV_HEADS, 1, qpg * PAIR),
        "cw": conv_w.astype(F32).T,
        "ogs": og[None, :w_ssm],
        "oga": jnp.broadcast_to(og[w_ssm:w_ssm + w_attn].reshape(N_KV_HEADS, qpg, HEAD_DIM, 1).transpose(0, 2, 1, 3),
                                (N_KV_HEADS, HEAD_DIM, qpg, PAIR)).reshape(N_KV_HEADS, HEAD_DIM, qpg * PAIR),
        "ogc": og[None, w_ssm + w_attn:],
        "wo": w_out.astype(BF16),
        "w1": w_ff1.astype(BF16),
        "w2": w_ff2.astype(BF16),
    }


def kernel(x_prompt, x_sample, cache_k, cache_v, state_ssm_re, state_ssm_im, state_conv, c_prompt, c_sample, rel_bias, w_ada, b_ada, w_in, ssm_a_re, ssm_a_im, ssm_log_dt, ssm_b_re, ssm_b_im, ssm_c_re, ssm_c_im, ssm_d, ssm_w_glu, q_norm_g, k_norm_g, attn_sinks, conv_w, out_norm_g, w_out, w_ff1, w_ff2):
    depth = w_in.shape[0]
    nbp, nbs = x_prompt.shape[0], x_sample.shape[0]
    assert x_sample.shape[1] == CHUNK and cache_k.shape[2] == WINDOW and nbs % 2 == 0

    nc = nbp + nbs
    ncp = -(-nc // SUBLANES) * SUBLANES
    c_all = jnp.pad(jnp.concatenate([c_prompt, c_sample], axis=0), ((0, ncp - nc), (0, 0)))
    mod = _mod_call(c_all, w_ada, b_ada)

    bias_p = _bias_call(rel_bias, sample=False)
    bias_s = _bias_call(rel_bias, sample=True)

    w_ssm, w_conv, w_kv = ssm_d.shape[1], conv_w.shape[1], N_KV_HEADS * HEAD_DIM
    widths = (w_ssm, w_conv, w_kv, attn_sinks.shape[1] * HEAD_DIM)
    scan_segs = tuple(SUBLANES // max(SSM_TILE // x.shape[1], 1) for x in (x_prompt, x_sample))
    weights = jax.vmap(functools.partial(_layer_weights, scan_segs))(
        w_in, ssm_a_re, ssm_a_im, ssm_log_dt, ssm_b_re, ssm_b_im, ssm_c_re, ssm_c_im, ssm_d, ssm_w_glu,
        q_norm_g, k_norm_g, attn_sinks, conv_w, out_norm_g, w_out, w_ff1, w_ff2)

    xp, xs = x_prompt, x_sample
    outs_p, outs_s = [], []
    for l in range(depth):
        lw = jax.tree.map(lambda a: a[l], weights)
        lw["widths"] = widths
        mod_p = mod[l, :nbp][:, None, :]
        mod_s = mod[l, nbp:nc][:, None, :]
        xp, *rest_p = _layer(xp, mod_p, lw, bias_p, False, None)
        xs, *rest_s = _layer(xs, mod_s, lw, bias_s, True,
                             (cache_k[l], cache_v[l], state_ssm_re[l], state_ssm_im[l], state_conv[l]))
        outs_p.append(rest_p)
        outs_s.append(rest_s)
    stack = lambda outs, i: jnp.stack([o[i] for o in outs])
    return (xp, xs,
            *(stack(outs_p, i) for i in range(5)),
            *(stack(outs_s, i) for i in range(5)))
```

```python
import functools
import math

import numpy as np
import jax
import jax.numpy as jnp
from jax import lax
from jax.experimental import pallas as pl
from jax.experimental.pallas import tpu as pltpu

F32 = jnp.float32
BF16 = jnp.bfloat16

CHUNK = 64
SSM_GROUP = 16
SSM_STATE = 64
HEAD_DIM = 64
N_KV_HEADS = 2
WINDOW = 128
CONV_K = 3
REL_BUCKETS = 32
REL_MAX_DIST = 64
EPS = 1e-6
NEG_INF = -1e30
LOG2E = math.log2(math.e)

LANES = 128
SUBLANES = 8
PAIR = 2 * CHUNK
VMEM_LIMIT_BYTES = 60 * 1024 * 1024


def _params(*sem):
    return pltpu.CompilerParams(dimension_semantics=sem, vmem_limit_bytes=VMEM_LIMIT_BYTES)


def _const_spec(shape):
    nd = len(shape)
    return pl.BlockSpec(shape, lambda *_: (0,) * nd, pipeline_mode=pl.Buffered(1))


def _tok_blocks(nb, r, tm):
    if r >= tm:
        assert r % tm == 0
        return 1, tm
    bb = min(tm // r, nb)
    assert nb % bb == 0
    return bb, r


MOD_TILE = 1024


def _mod_kernel(c_ref, w_ref, b_ref, o_ref):
    c = c_ref[...]
    s = (c * jax.nn.sigmoid(c)).astype(BF16)
    o_ref[...] = jnp.dot(s, w_ref[...].astype(BF16), preferred_element_type=F32) + b_ref[...]


def _mod_call(c_all, w_ada, b_ada):
    depth, d, n = w_ada.shape
    nc = c_all.shape[0]
    tn = MOD_TILE
    return pl.pallas_call(
        _mod_kernel,
        grid=(depth, n // tn),
        in_specs=[pl.BlockSpec((nc, d), lambda l, j: (0, 0)),
                  pl.BlockSpec((None, d, tn), lambda l, j: (l, 0, j)),
                  pl.BlockSpec((None, 1, tn), lambda l, j: (l, 0, j))],
        out_specs=pl.BlockSpec((None, nc, tn), lambda l, j: (l, 0, j)),
        out_shape=jax.ShapeDtypeStruct((depth, nc, n), F32),
        compiler_params=_params("arbitrary", "arbitrary"),
    )(c_all, w_ada, b_ada.reshape(depth, 1, n))


def _bucket_maps(sample):
    lane = np.arange(PAIR)[None, :]
    if sample:
        nk = 2 * WINDOW + PAIR
        row = np.arange(nk)[:, None]
        own = row >= 2 * WINDOW
        key_b = np.where(own, (row - 2 * WINDOW) // CHUNK, row // WINDOW)
        key_s = np.where(own, WINDOW + (row - 2 * WINDOW) % CHUNK, row % WINDOW)
        rel = key_s - WINDOW - lane % CHUNK
        visible = key_b == lane // CHUNK
    else:
        nk = WINDOW + PAIR
        row = np.arange(nk)[:, None]
        rel = row - WINDOW - lane
        dc = row // CHUNK - lane // CHUNK
        visible = (dc >= 0) & (dc <= WINDOW // CHUNK)
    half = REL_BUCKETS // 2
    exact = half // 2
    n = np.abs(rel)
    nf = np.maximum(n, 1).astype(np.float32)
    far = exact + (np.log(nf / np.float32(exact)) / np.float32(math.log(REL_MAX_DIST / exact))
                   * np.float32(half - exact)).astype(np.int32)
    far = np.minimum(far, half - 1)
    bucket = np.where(rel > 0, half, 0) + np.where(n < exact, n, far)
    full = np.where(visible, bucket, -1).astype(np.int32)
    first = np.where(row >= WINDOW, full, -1).astype(np.int32)
    return np.stack([full, first])


def _bias_kernel(table_ref, bucket_ref, o_ref):
    h = pl.program_id(1)
    bucket = bucket_ref[...]
    acc = jnp.full(bucket.shape, NEG_INF, F32)
    for b in range(REL_BUCKETS):
        acc = jnp.where(bucket == b, table_ref[b, h] * LOG2E, acc)
    o_ref[...] = acc


def _bias_call(rel_bias, sample):
    buckets = jnp.asarray(_bucket_maps(sample))
    _, nk, _ = buckets.shape
    n_heads = rel_bias.shape[1]
    qpg = n_heads // N_KV_HEADS
    return pl.pallas_call(
        _bias_kernel,
        grid=(2, n_heads),
        in_specs=[pl.BlockSpec(memory_space=pltpu.SMEM),
                  pl.BlockSpec((None, nk, PAIR), lambda s, h: (s, 0, 0))],
        out_specs=pl.BlockSpec((None, None, nk, PAIR), lambda s, h: (s, h // qpg, 0, h % qpg)),
        out_shape=jax.ShapeDtypeStruct((2, N_KV_HEADS, nk, qpg * PAIR), F32),
        compiler_params=_params("arbitrary", "arbitrary"),
    )(rel_bias, buckets)


def _rms(y):
    return y * lax.rsqrt(jnp.mean(y * y, axis=-1, keepdims=True) + EPS)


def _in_proj_kernel(x_ref, shift_ref, scale_ref, zinit_ref, wrm_ref, wt_ref, kg_ref, cw_ref, ogc_ref, gq_ref,
                    u_ref, yc_ref, zt_ref, k_ref, v_ref, qt_ref, vt_ref, zprev, *, widths):
    bb, tr, d = x_ref.shape

    @pl.when(pl.program_id(1) == 0)
    def _():
        zprev[...] = zinit_ref[...]

    x = x_ref[...]
    ms = jnp.mean(x * x, axis=-1, keepdims=True)
    h = x * lax.rsqrt(ms + EPS) * (1.0 + scale_ref[...]) + shift_ref[...]
    h = h.reshape(bb * tr, d).astype(BF16)

    w_ssm, w_conv, w_kv, w_attn = widths

    pt = lax.dot_general(wt_ref[...], h, (((1,), (1,)), ((), ())), preferred_element_type=F32)
    gq = gq_ref[...]
    for hd in range(w_attn // HEAD_DIM):
        q = pt[hd * HEAD_DIM:(hd + 1) * HEAD_DIM]
        ms = jnp.sum(q * q, axis=0, keepdims=True) * (1.0 / HEAD_DIM)
        qt_ref[hd * HEAD_DIM:(hd + 1) * HEAD_DIM, :] = (q * lax.rsqrt(ms + EPS) * gq).astype(qt_ref.dtype)
    vt_ref[...] = pt[w_attn:].astype(vt_ref.dtype)

    pc = jnp.dot(h, wrm_ref[:, 0:3 * w_conv], preferred_element_type=F32)
    gb = pc[:, 0:w_conv].reshape(bb, tr, w_conv)
    z = (pc[:, w_conv:2 * w_conv] * pc[:, 2 * w_conv:3 * w_conv]).reshape(bb, tr, w_conv)

    zp = jnp.concatenate([zprev[...], z], axis=1)
    ztail = z[:, tr - SUBLANES:, :]
    zprev[...] = ztail
    zt_ref[...] = ztail
    cw = cw_ref[...]

    def conv_part(c, n):
        if bb == 1:
            rc = tr // n
            zs, gs = zp[:, c * rc:c * rc + rc + SUBLANES], gb[:, c * rc:(c + 1) * rc]
        else:
            bc, rc = bb // n, tr
            zs, gs = zp[c * bc:(c + 1) * bc], gb[c * bc:(c + 1) * bc]
        conv = (zs[:, SUBLANES - 2:SUBLANES - 2 + rc] * cw[0:1] + zs[:, SUBLANES - 1:SUBLANES - 1 + rc] * cw[1:2]
                + zs[:, SUBLANES:] * cw[2:3])
        y = (_rms(gs * conv) * ogc_ref[...]).astype(yc_ref.dtype)
        rows = y.shape[0] * y.shape[1]
        yc_ref[c * rows:(c + 1) * rows, :] = y.reshape(rows, w_conv)

    n_part = 4
    assert (tr if bb == 1 else bb) % n_part == 0
    col = 3 * w_conv
    for c, width in enumerate((w_ssm // 2, w_ssm // 2, 2 * w_kv)):
        conv_part(c, n_part)
        pp = jnp.dot(h, wrm_ref[:, col:col + width], preferred_element_type=F32)
        col += width
        if c < 2:
            u_ref[:, c * width:(c + 1) * width] = pp
        else:
            k = pp[:, :w_kv]
            v_ref[...] = pp[:, w_kv:]
    conv_part(n_part - 1, n_part)

    lo = lax.broadcasted_iota(jnp.int32, k.shape, 1) < HEAD_DIM
    k2 = k * k
    s_lo = jnp.sum(jnp.where(lo, k2, 0.0), axis=-1, keepdims=True)
    s_hi = jnp.sum(jnp.where(lo, 0.0, k2), axis=-1, keepdims=True)
    ssq = jnp.where(lo, s_lo, s_hi)
    k_ref[...] = k * lax.rsqrt(ssq * (1.0 / HEAD_DIM) + EPS) * kg_ref[...]


TOK_TILE = 512


def _in_proj_call(x3, mod3, zinit, wrm, wt, kg2, cw, ogc, gq, widths):
    nb, r, d = x3.shape
    w_ssm, w_conv, w_kv, w_attn = widths
    bb, tr = _tok_blocks(nb, r, TOK_TILE)
    tmm = bb * tr
    assert tmm == TOK_TILE
    nt = r // tr
    ttot = nb * r
    tok = lambda b, t: (b * nt + t, 0)
    tok_t = lambda b, t: (b * nt + t, 0, 0)
    halo_spec = pl.BlockSpec((bb, SUBLANES, w_conv), lambda b, t: (b, 0, 0))
    out_shape = [jax.ShapeDtypeStruct((ttot, w_ssm), F32), jax.ShapeDtypeStruct((ttot, w_conv), BF16),
                 jax.ShapeDtypeStruct((nb, SUBLANES, w_conv), F32),
                 jax.ShapeDtypeStruct((ttot, w_kv), F32), jax.ShapeDtypeStruct((ttot, w_kv), F32),
                 jax.ShapeDtypeStruct((ttot // tmm, w_attn, tmm), BF16),
                 jax.ShapeDtypeStruct((ttot // tmm, w_kv, tmm), BF16)]
    out_specs = [pl.BlockSpec((tmm, w_ssm), tok), pl.BlockSpec((tmm, w_conv), tok), halo_spec,
                 pl.BlockSpec((tmm, w_kv), tok), pl.BlockSpec((tmm, w_kv), tok),
                 pl.BlockSpec((None, w_attn, tmm), tok_t), pl.BlockSpec((None, w_kv, tmm), tok_t)]
    consts = (wrm, wt, kg2, cw, ogc, gq)
    return pl.pallas_call(
        functools.partial(_in_proj_kernel, widths=widths),
        grid=(nb // bb, nt),
        in_specs=[pl.BlockSpec((bb, tr, d), lambda b, t: (b, t, 0)),
                  pl.BlockSpec((bb, 1, d), lambda b, t: (b, 0, 0)),
                  pl.BlockSpec((bb, 1, d), lambda b, t: (b, 0, 1)),
                  halo_spec] + [_const_spec(c.shape) for c in consts],
        out_specs=out_specs,
        out_shape=out_shape,
        scratch_shapes=[pltpu.VMEM((bb, SUBLANES, w_conv), F32)],
        compiler_params=_params("arbitrary", "arbitrary"),
    )(x3, mod3, mod3, zinit, *consts)


SCAN_LANE_BLOCKS = 8


def _gelu_tanh(x):
    return 0.5 * x * (1.0 + jnp.tanh(math.sqrt(2.0 / math.pi) * (x + 0.044715 * (x * x * x))))


def _segment_perm(ts):
    lseg = ts // SUBLANES
    p = np.zeros((ts, ts), np.float32)
    i, j = np.meshgrid(np.arange(lseg), np.arange(SUBLANES), indexing="ij")
    p[(i * SUBLANES + j).ravel(), (j * lseg + i).ravel()] = 1.0
    return p


def _ssm_kernel(u_ref, h0r_ref, h0i_ref, perm_ref, permt_ref, fb_ref, ab_ref, sg_ref, pw_ref, cb_ref, dskip_ref,
                wglu_ref, og_ref, y_ref, hre_ref, him_ref, bre, bim, car, *, segs_per_seq):
    ts, w = u_ref.shape
    ns = bre.shape[1]
    lseg = ts // SUBLANES
    half_u = w // 2
    half_s = ns // 2
    t = pl.program_id(1)

    @pl.when(t == 0)
    def _():
        car[0:SUBLANES, :] = h0r_ref[...]
        car[SUBLANES:, :] = h0i_ref[...]

    u = u_ref[...]
    up = jnp.dot(perm_ref[...], u.astype(BF16), preferred_element_type=F32).astype(BF16)
    for j in range(2):
        bb = jnp.dot(up[:, j * half_u:(j + 1) * half_u], fb_ref[j], preferred_element_type=F32)
        bre[:, j * half_s:(j + 1) * half_s] = bb[:, :half_s]
        bim[:, j * half_s:(j + 1) * half_s] = bb[:, half_s:]

    seq_start = lax.broadcasted_iota(jnp.int32, (SUBLANES, LANES), 0) % segs_per_seq == 0
    zero = jnp.zeros((SUBLANES, LANES), F32)
    nblk = SCAN_LANE_BLOCKS
    for c0 in range(0, ns // LANES, nblk):
        sls = [pl.ds((c0 + i) * LANES, LANES) for i in range(nblk)]
        ab = [(ab_ref[0, :, sl], ab_ref[1, :, sl]) for sl in sls]

        def pass1(i, carry, sls=sls, ab=ab):
            rows = pl.ds(pl.multiple_of(i * SUBLANES, SUBLANES), SUBLANES)
            new = []
            for sl, (ar, ai), (hr, hi) in zip(sls, ab, carry):
                hr, hi = ar * hr - ai * hi + bre[rows, sl], ar * hi + ai * hr + bim[rows, sl]
                bre[rows, sl] = hr
                bim[rows, sl] = hi
                new.append((hr, hi))
            return tuple(new)

        ends = lax.fori_loop(0, lseg, pass1, tuple((zero, zero) for _ in sls), unroll=True)

        starts = []
        for sl, (er, ei) in zip(sls, ends):
            xr = jnp.where(seq_start, car[0:SUBLANES, sl], pltpu.roll(er, 1, axis=0))
            xi = jnp.where(seq_start, car[SUBLANES:, sl], pltpu.roll(ei, 1, axis=0))
            for k in range(segs_per_seq.bit_length() - 1):
                sr = pltpu.roll(xr, 1 << k, axis=0)
                si = pltpu.roll(xi, 1 << k, axis=0)
                mr, mi = sg_ref[2 * k, :, sl], sg_ref[2 * k + 1, :, sl]
                xr, xi = xr + (mr * sr - mi * si), xi + (mr * si + mi * sr)
            lr, li = sg_ref[6, :, sl], sg_ref[7, :, sl]
            nr = lr * xr - li * xi + er
            ni = lr * xi + li * xr + ei
            hre_ref[:, sl] = nr
            him_ref[:, sl] = ni
            car[0:SUBLANES, sl] = jnp.broadcast_to(nr[SUBLANES - 1:SUBLANES, :], (SUBLANES, LANES))
            car[SUBLANES:, sl] = jnp.broadcast_to(ni[SUBLANES - 1:SUBLANES, :], (SUBLANES, LANES))
            starts.append((xr, xi))

        def pass2(i, c, sls=sls, starts=starts):
            rows = pl.ds(pl.multiple_of(i * SUBLANES, SUBLANES), SUBLANES)
            for sl, (sr, si) in zip(sls, starts):
                pr = pw_ref[0, rows, sl]
                pi = pw_ref[1, rows, sl]
                bre[rows, sl] += pr * sr - pi * si
                bim[rows, sl] += pr * si + pi * sr
            return c

        lax.fori_loop(0, lseg, pass2, 0, unroll=True)

    ys = []
    for j in range(2):
        hcat = jnp.concatenate([bre[:, j * half_s:(j + 1) * half_s].astype(BF16),
                                bim[:, j * half_s:(j + 1) * half_s].astype(BF16)], axis=-1)
        ys.append(jnp.dot(hcat, cb_ref[j], preferred_element_type=F32))
    yp = jnp.concatenate(ys, axis=-1)
    p0 = yp.astype(BF16)
    r1 = yp - p0.astype(F32)
    p1 = r1.astype(BF16)
    p2 = (r1 - p1.astype(F32)).astype(BF16)
    parts = jnp.dot(permt_ref[...], jnp.concatenate([p0, p1, p2], axis=-1), preferred_element_type=F32)
    y = (parts[:, :w] + parts[:, w:2 * w]) + parts[:, 2 * w:] + dskip_ref[...] * u
    y = _gelu_tanh(y)
    y = y * jax.nn.sigmoid(jnp.dot(y.astype(BF16), wglu_ref[...], preferred_element_type=F32))
    y_ref[...] = (_rms(y) * og_ref[...]).astype(y_ref.dtype)


SSM_TILE = 512


def _ssm_call(u, h0r, h0i, fb, scan, cb, dskip, wglu, ogs, r):
    ttot, w = u.shape
    ab, sg, pw = scan
    ns = ab.shape[-1]
    ts = SSM_TILE
    lseg = ts // SUBLANES
    assert pw.shape[1] == ts
    nseq = ttot // r
    if r >= ts:
        assert r % ts == 0
        seq_per_tile, nt = 1, r // ts
    else:
        assert r % lseg == 0 and ts % r == 0 and nseq % (ts // r) == 0
        seq_per_tile, nt = ts // r, 1
    segs_per_seq = SUBLANES // seq_per_tile
    ngrp = nseq // seq_per_tile
    init = lambda h: jnp.repeat(h.reshape(ngrp, seq_per_tile, ns), segs_per_seq, axis=1)
    perm = _segment_perm(ts)
    perm, permt = jnp.asarray(perm, BF16), jnp.asarray(perm.T, BF16)
    st_spec = pl.BlockSpec((None, SUBLANES, ns), lambda b, t: (b, 0, 0))
    consts = (perm, permt, fb, ab, sg, pw, cb, dskip, wglu, ogs)
    y, hre, him = pl.pallas_call(
        functools.partial(_ssm_kernel, segs_per_seq=segs_per_seq),
        grid=(ngrp, nt),
        in_specs=[pl.BlockSpec((ts, w), lambda b, t: (b * nt + t, 0)), st_spec, st_spec]
        + [_const_spec(c.shape) for c in consts],
        out_specs=[pl.BlockSpec((ts, w), lambda b, t: (b * nt + t, 0)), st_spec, st_spec],
        out_shape=[jax.ShapeDtypeStruct((ttot, w), BF16),
                   jax.ShapeDtypeStruct((ngrp, SUBLANES, ns), F32), jax.ShapeDtypeStruct((ngrp, SUBLANES, ns), F32)],
        scratch_shapes=[pltpu.VMEM((ts, ns), F32), pltpu.VMEM((ts, ns), F32), pltpu.VMEM((2 * SUBLANES, ns), F32)],
        compiler_params=_params("arbitrary", "arbitrary"),
    )(u, init(h0r), init(h0i), *consts)
    last = lambda h: h[:, segs_per_seq - 1::segs_per_seq].reshape(nseq, ns)
    return y, last(hre), last(him)


def _scan_consts(a1, lseg, segs_per_seq):
    def cmul(x, y):
        return x[0] * y[0] - x[1] * y[1], x[0] * y[1] + x[1] * y[0]

    tr, ti = a1[0][None], a1[1][None]
    while tr.shape[0] < lseg:
        nr, ni = cmul((tr, ti), (tr[-1], ti[-1]))
        tr, ti = jnp.concatenate([tr, nr]), jnp.concatenate([ti, ni])
    pw = jnp.repeat(jnp.stack([tr[:lseg], ti[:lseg]]), SUBLANES, axis=1)
    ns = a1[0].shape[0]
    ab = jnp.stack([jnp.broadcast_to(a1[0], (SUBLANES, ns)), jnp.broadcast_to(a1[1], (SUBLANES, ns))])
    al = (tr[lseg - 1], ti[lseg - 1])
    al2 = cmul(al, al)
    al4 = cmul(al2, al2)
    row = jnp.arange(SUBLANES)[:, None]
    rows = []
    for k, a in enumerate((al, al2, al4)):
        keep = row % segs_per_seq >= (1 << k)
        rows += [jnp.where(keep, a[0][None, :], 0.0), jnp.where(keep, a[1][None, :], 0.0)]
    rows += [jnp.broadcast_to(al[0], (SUBLANES, ns)), jnp.broadcast_to(al[1], (SUBLANES, ns))]
    return ab, jnp.stack(rows), pw


def _ssm_consts(a_re, a_im, log_dt, b_re, b_im, c_re, c_im):
    g, p = a_re.shape
    hh = b_re.shape[-1]
    ar, ai = a_re.astype(F32), a_im.astype(F32)
    dt = jnp.exp(log_dt.astype(F32))[:, None]
    mag = jnp.exp(dt * ar)
    abar_re, abar_im = mag * jnp.cos(dt * ai), mag * jnp.sin(dt * ai)
    den = ar * ar + ai * ai
    f_re = ((abar_re - 1.0) * ar + abar_im * ai) / den
    f_im = (abar_im * ar - (abar_re - 1.0) * ai) / den
    fb_re = f_re[..., None] * b_re - f_im[..., None] * b_im
    fb_im = f_re[..., None] * b_im + f_im[..., None] * b_re
    gh = g // 2
    eye = jnp.eye(gh, dtype=F32)

    def in_blk(m):
        return jnp.einsum('gph,gk->ghkp', m, eye).reshape(gh * hh, gh * p)

    def out_blk(m):
        return jnp.einsum('ghp,gk->kpgh', m, eye).reshape(gh * p, gh * hh)

    fb = jnp.stack([jnp.concatenate([in_blk(fb_re[j * gh:(j + 1) * gh]), in_blk(fb_im[j * gh:(j + 1) * gh])], axis=1)
                    for j in range(2)]).astype(BF16)
    cb = jnp.stack([jnp.concatenate([out_blk(c_re[j * gh:(j + 1) * gh]), -out_blk(c_im[j * gh:(j + 1) * gh])], axis=0)
                    for j in range(2)]).astype(BF16)

    return fb, (abar_re.reshape(-1), abar_im.reshape(-1)), cb


def _attn_kernel(qt_ref, k_ref, vt_ref, kc_ref, vtc_ref, bias_ref, sink_ref, og_ref, yt_ref,
                 *, sample, pairs_per_seq):
    n_rows, tq = qt_ref.shape
    npairs = tq // PAIR
    qpg = n_rows // (N_KV_HEADS * HEAD_DIM)
    i = pl.program_id(0)
    if sample:
        k_all, vt_all = k_ref[...], vt_ref[...]
    else:
        k_all = jnp.concatenate([kc_ref[...], k_ref[...]], axis=0)
        vt_all = jnp.concatenate([vtc_ref[...], vt_ref[...]], axis=1)
    def pair(p):
        if sample:
            kb = jnp.concatenate([kc_ref[p * 2 * WINDOW:(p + 1) * 2 * WINDOW, :],
                                  k_all[p * PAIR:(p + 1) * PAIR, :]], axis=0)
            vtb = jnp.concatenate([vtc_ref[:, p * 2 * WINDOW:(p + 1) * 2 * WINDOW],
                                   vt_all[:, p * PAIR:(p + 1) * PAIR]], axis=1)
            sel = 0
        else:
            kb = k_all[p * PAIR:p * PAIR + WINDOW + PAIR, :]
            vtb = vt_all[:, p * PAIR:p * PAIR + WINDOW + PAIR]
            sel = ((i * npairs + p) % pairs_per_seq == 0).astype(jnp.int32)
        kb = kb.astype(BF16)
        vtb = vtb.astype(BF16)
        ones = jnp.ones((2 * SUBLANES, vtb.shape[1]), BF16)
        outs = []
        ssq = jnp.zeros((1, PAIR), F32)
        qgs = [jnp.concatenate([qt_ref[(g * qpg + hh) * HEAD_DIM:(g * qpg + hh + 1) * HEAD_DIM,
                                       p * PAIR:(p + 1) * PAIR] for hh in range(qpg)], axis=1)
               for g in range(N_KV_HEADS)]
        zero = jnp.zeros_like(qgs[0])
        qblk = jnp.concatenate([jnp.concatenate([qgs[g] if gg == g else zero for gg in range(N_KV_HEADS)], axis=1)
                                for g in range(N_KV_HEADS)], axis=0)
        bias = jnp.concatenate([bias_ref[sel, g] for g in range(N_KV_HEADS)], axis=1)
        s = jnp.dot(kb, qblk, preferred_element_type=F32) + bias
        yield
        sink = jnp.concatenate([sink_ref[g] for g in range(N_KV_HEADS)], axis=1)
        m = jnp.maximum(jnp.max(s, axis=0, keepdims=True), sink)
        e = jnp.exp2(s - m).astype(BF16)
        esink = jnp.exp2(sink - m)
        yield
        for g in range(N_KV_HEADS):
            lanes = slice(g * qpg * PAIR, (g + 1) * qpg * PAIR)
            va = jnp.concatenate([vtb[g * HEAD_DIM:(g + 1) * HEAD_DIM, :], ones], axis=0)
            oa = jnp.dot(va, e[:, lanes], preferred_element_type=F32)
            den = oa[HEAD_DIM:HEAD_DIM + 1, :] + esink[:, lanes]
            o = oa[:HEAD_DIM, :] / den
            outs.append(o)
            sq = jnp.sum(o * o, axis=0, keepdims=True)
            for hh in range(qpg):
                ssq = ssq + sq[:, hh * PAIR:(hh + 1) * PAIR]
        rn = lax.rsqrt(ssq * (1.0 / n_rows) + EPS)
        rn = jnp.concatenate([rn] * qpg, axis=1)
        for g in range(N_KV_HEADS):
            o = outs[g] * rn * og_ref[g]
            for hh in range(qpg):
                r0 = (g * qpg + hh) * HEAD_DIM
                yt_ref[r0:r0 + HEAD_DIM, p * PAIR:(p + 1) * PAIR] = \
                    o[:, hh * PAIR:(hh + 1) * PAIR].astype(yt_ref.dtype)
        yield

    pairs = [pair(p) for p in range(npairs)]
    next(pairs[0])
    for p in range(npairs):
        if p + 1 < npairs:
            next(pairs[p + 1])
        next(pairs[p])
        next(pairs[p])


def _attn_call(qt, k, vt, kc, vtc, bias, sink, oga, sample, seq_len):
    ntile, n_rows, tq = qt.shape
    kvw = k.shape[1]
    assert sample or seq_len % tq == 0
    npairs = tq // PAIR
    tile3 = lambda i: (i, 0, 0)
    if sample:
        ctx = 2 * WINDOW * npairs
        kc_spec = pl.BlockSpec((ctx, kvw), lambda i: (i, 0))
        vtc_spec = pl.BlockSpec((kvw, ctx), lambda i: (0, i))
    else:
        kc_spec = pl.BlockSpec((WINDOW, kvw), lambda i: (jnp.maximum(i * npairs - 1, 0), 0))
        vtc_spec = pl.BlockSpec((None, kvw, WINDOW), lambda i: (jnp.maximum(i - 1, 0), 0, tq // WINDOW - 1))
    return pl.pallas_call(
        functools.partial(_attn_kernel, sample=sample, pairs_per_seq=seq_len // PAIR),
        grid=(ntile,),
        in_specs=[pl.BlockSpec((None, n_rows, tq), tile3),
                  pl.BlockSpec((tq, kvw), lambda i: (i, 0)),
                  pl.BlockSpec((None, kvw, tq), tile3),
                  kc_spec, vtc_spec,
                  _const_spec(bias.shape), _const_spec(sink.shape),
                  _const_spec(oga.shape)],
        out_specs=pl.BlockSpec((None, n_rows, tq), tile3),
        out_shape=jax.ShapeDtypeStruct((ntile, n_rows, tq), BF16),
        compiler_params=_params("arbitrary"),
    )(qt, k, vt, kc, vtc, bias, sink, oga)


def _out_proj_kernel(x_ref, gate_ref, ys_ref, yt_ref, yc_ref, wo_ref, xo_ref):
    bb, tr, d = x_ref.shape
    ws, wa = ys_ref.shape[1], yt_ref.shape[0]
    o = jnp.dot(ys_ref[...], wo_ref[0:ws, :], preferred_element_type=F32)
    o += lax.dot_general(yt_ref[...], wo_ref[ws:ws + wa, :], (((0,), (0,)), ((), ())),
                         preferred_element_type=F32)
    o += jnp.dot(yc_ref[...], wo_ref[ws + wa:, :], preferred_element_type=F32)
    xo_ref[...] = x_ref[...] + gate_ref[...] * o.reshape(bb, tr, d)


def _out_proj_call(x3, mod3, ys, yt, yc, wo):
    nb, r, d = x3.shape
    ws, wa, wc = ys.shape[1], yt.shape[1], yc.shape[1]
    bb, tr = _tok_blocks(nb, r, TOK_TILE)
    tmm = bb * tr
    assert tmm == yt.shape[2]
    nt = r // tr
    tok3 = lambda b, t: (b, t, 0)
    return pl.pallas_call(
        _out_proj_kernel,
        grid=(nb // bb, nt),
        in_specs=[pl.BlockSpec((bb, tr, d), tok3),
                  pl.BlockSpec((bb, 1, d), lambda b, t: (b, 0, 2)),
                  pl.BlockSpec((tmm, ws), lambda b, t: (b * nt + t, 0)),
                  pl.BlockSpec((None, wa, tmm), lambda b, t: (b * nt + t, 0, 0)),
                  pl.BlockSpec((tmm, wc), lambda b, t: (b * nt + t, 0)),
                  _const_spec(wo.shape)],
        out_specs=pl.BlockSpec((bb, tr, d), tok3),
        out_shape=jax.ShapeDtypeStruct((nb, r, d), F32),
        compiler_params=_params("arbitrary", "arbitrary"),
    )(x3, mod3, ys, yt, yc, wo)


FFN_TILE = 1024
FFN_ROWS = 1024
NORM_ROWS = 64


def _ffn_kernel(x_ref, shift_ref, scale_ref, gate_ref, w1_hbm, w2_hbm, o_ref, h_s, r_s, w1_buf, w2_buf, sem):
    bb, tr, d = x_ref.shape
    tf = w1_buf.shape[2]
    nj = w1_hbm.shape[1] // tf
    step = pl.program_id(0) * pl.num_programs(1) + pl.program_id(1)
    n_steps = pl.num_programs(0) * pl.num_programs(1)

    def w_copies(j, slot):
        cols = pl.ds(pl.multiple_of(j * tf, tf), tf)
        return (pltpu.make_async_copy(w1_hbm.at[:, cols], w1_buf.at[slot], sem.at[0, slot]),
                pltpu.make_async_copy(w2_hbm.at[cols, :], w2_buf.at[slot], sem.at[1, slot]))

    @pl.when(step == 0)
    def _():
        for c in w_copies(0, 0):
            c.start()

    per = tr // NORM_ROWS

    def chunk(c):
        return c // per, pl.ds(pl.multiple_of((c % per) * NORM_ROWS, NORM_ROWS), NORM_ROWS)

    def scales(c, carry):
        b, rows = chunk(c)
        x = x_ref[b, rows, :]
        r = lax.rsqrt(jnp.mean(x * x, axis=-1, keepdims=True) + EPS)
        r_s[pl.ds(pl.multiple_of(c * NORM_ROWS, NORM_ROWS), NORM_ROWS), :] = jnp.broadcast_to(r, (NORM_ROWS, LANES))
        o_ref[b, rows, :] = jnp.zeros((NORM_ROWS, d), F32)
        return carry

    def rows_out(c, carry):
        b, rows = chunk(c)
        flat = pl.ds(pl.multiple_of(c * NORM_ROWS, NORM_ROWS), NORM_ROWS)
        r = jnp.concatenate([r_s[flat, :]] * (d // LANES), axis=1)
        h = x_ref[b, rows, :] * r * (1.0 + scale_ref[b]) + shift_ref[b]
        h_s[flat, :] = h.astype(BF16)
        return carry

    lax.fori_loop(0, bb * per, scales, 0, unroll=4)
    lax.fori_loop(0, bb * per, rows_out, 0)

    def tile_pair(jj, carry):
        for slot in (0, 1):
            j = 2 * jj + slot
            for c in w_copies(j, slot):
                c.wait()

            @pl.when(j + 1 < nj)
            def _():
                for c in w_copies(j + 1, 1 - slot):
                    c.start()

            if slot == 1:
                @pl.when(jnp.logical_and(j + 1 == nj, step + 1 < n_steps))
                def _():
                    for c in w_copies(0, 0):
                        c.start()

            a = jnp.maximum(jnp.dot(h_s[...], w1_buf[slot], preferred_element_type=F32), 0.0)
            o_ref[...] += jnp.dot((a * a).astype(BF16), w2_buf[slot], preferred_element_type=F32).reshape(bb, tr, d)
        return carry

    lax.fori_loop(0, nj // 2, tile_pair, 0)
    o_ref[...] = x_ref[...] + gate_ref[...] * o_ref[...]


def _ffn_call(x3, mod3, w1, w2):
    nb, r, d = x3.shape
    tf = FFN_TILE
    assert w1.shape[1] % (2 * tf) == 0
    bb, tr = _tok_blocks(nb, r, FFN_ROWS)
    tok3 = lambda b, t: (b, t, 0)
    mod_spec = lambda c: pl.BlockSpec((bb, 1, d), lambda b, t: (b, 0, c))
    return pl.pallas_call(
        _ffn_kernel,
        grid=(nb // bb, r // tr),
        in_specs=[pl.BlockSpec((bb, tr, d), tok3), mod_spec(3), mod_spec(4), mod_spec(5),
                  pl.BlockSpec(memory_space=pl.ANY), pl.BlockSpec(memory_space=pl.ANY)],
        out_specs=pl.BlockSpec((bb, tr, d), tok3),
        out_shape=jax.ShapeDtypeStruct((nb, r, d), F32),
        scratch_shapes=[pltpu.VMEM((bb * tr, d), BF16), pltpu.VMEM((bb * tr, LANES), F32),
                        pltpu.VMEM((2, d, tf), BF16), pltpu.VMEM((2, tf, d), BF16),
                        pltpu.SemaphoreType.DMA((2, 2))],
        compiler_params=_params("arbitrary", "arbitrary"),
    )(x3, mod3, mod3, mod3, w1, w2)


def _layer(x3, mod3, lw, bias, sample, state):
    nb, r, d = x3.shape
    widths = lw["widths"]
    w_conv = widths[1]
    scan = lw["scan"][1 if sample else 0]
    ns = scan[0].shape[-1]
    if sample:
        cache_k, cache_v, h0r, h0i, conv_buf = state
        h0r = h0r.reshape(nb, ns)
        h0i = h0i.reshape(nb, ns)
        zinit = jnp.pad(conv_buf, ((0, 0), (SUBLANES - (CONV_K - 1), 0), (0, 0)))
    else:
        h0r = h0i = jnp.zeros((nb, ns), F32)
        zinit = jnp.zeros((nb, SUBLANES, w_conv), F32)
    u, yc, ztail, k, v, qt, vt = _in_proj_call(x3, mod3, zinit, lw["wrm"], lw["wt"], lw["kg2"], lw["cw"],
                                               lw["ogc"], lw["gq"], widths)
    ys, hre, him = _ssm_call(u, h0r, h0i, lw["fb"], scan, lw["cb"], lw["dskip"], lw["wglu"], lw["ogs"], r)

    kvw = k.shape[1]
    if sample:
        n_buf = cache_k.shape[1]
        kc = cache_k.reshape(nb * n_buf, kvw)
        vtc = cache_v.reshape(nb * n_buf, kvw).T.astype(vt.dtype)
    else:
        kc, vtc = k, vt
    yt = _attn_call(qt, k, vt, kc, vtc, bias, lw["sink"], lw["oga"], sample, r)

    x3 = _out_proj_call(x3, mod3, ys, yt, yc, lw["wo"])
    x3 = _ffn_call(x3, mod3, lw["w1"], lw["w2"])

    keep = min(r, WINDOW)
    k4 = k.reshape(nb, r, kvw)[:, r - keep:].reshape(nb, keep, N_KV_HEADS, HEAD_DIM)
    v4 = v.reshape(nb, r, kvw)[:, r - keep:].reshape(nb, keep, N_KV_HEADS, HEAD_DIM)
    if sample:
        new_k = jnp.concatenate([cache_k, k4], axis=1)[:, -n_buf:]
        new_v = jnp.concatenate([cache_v, v4], axis=1)[:, -n_buf:]
    else:
        new_k, new_v = k4, v4
    g = ns // SSM_STATE
    return (x3, new_k, new_v, hre.reshape(nb, g, SSM_STATE), him.reshape(nb, g, SSM_STATE),
            ztail[:, -(CONV_K - 1):])


def _layer_weights(scan_segs, w_in, ssm_a_re, ssm_a_im, ssm_log_dt, ssm_b_re, ssm_b_im, ssm_c_re, ssm_c_im, ssm_d,
                   ssm_w_glu, q_norm_g, k_norm_g, attn_sinks, conv_w, out_norm_g, w_out, w_ff1, w_ff2):
    w_ssm = ssm_d.shape[0]
    n_heads = attn_sinks.shape[0]
    w_attn = n_heads * HEAD_DIM
    w_kv = N_KV_HEADS * HEAD_DIM
    w_conv = conv_w.shape[0]
    qpg = n_heads // N_KV_HEADS
    o = np.cumsum([0, w_ssm, w_attn, w_kv, w_kv, w_conv, w_conv, w_conv])
    wu, wq, wk, wv, wgb, wgc, wxc = (w_in[:, o[i]:o[i + 1]] for i in range(7))
    fb, abar, cb = _ssm_consts(ssm_a_re, ssm_a_im, ssm_log_dt, ssm_b_re, ssm_b_im, ssm_c_re, ssm_c_im)
    og = out_norm_g.astype(F32)
    return {
        "wrm": jnp.concatenate([wgb, wgc, wxc, wu, wk, wv], axis=1).astype(BF16),
        "wt": jnp.concatenate([wq, wv], axis=1).T.astype(BF16),
        "kg2": jnp.tile(k_norm_g.astype(F32), N_KV_HEADS)[None, :],
        "fb": fb, "cb": cb,
        "scan": [_scan_consts(abar, SSM_TILE // SUBLANES, segs) for segs in scan_segs],
        "dskip": ssm_d.astype(F32)[None, :],
        "wglu": ssm_w_glu.astype(BF16),
        "gq": jnp.broadcast_to((q_norm_g.astype(F32) * (HEAD_DIM ** -0.5 * LOG2E))[:, None], (HEAD_DIM, TOK_TILE)),
        "sink": jnp.broadcast_to((attn_sinks.astype(F32) * LOG2E).reshape(N_KV_HEADS, 1, qpg, 1),
                                 (N_KV_HEADS, 1, qpg, PAIR)).reshape(N_KV_HEADS, 1, qpg * PAIR),
        "cw": conv_w.astype(F32).T,
        "ogs": og[None, :w_ssm],
        "oga": jnp.broadcast_to(og[w_ssm:w_ssm + w_attn].reshape(N_KV_HEADS, qpg, HEAD_DIM, 1).transpose(0, 2, 1, 3),
                                (N_KV_HEADS, HEAD_DIM, qpg, PAIR)).reshape(N_KV_HEADS, HEAD_DIM, qpg * PAIR),
        "ogc": og[None, w_ssm + w_attn:],
        "wo": w_out.astype(BF16),
        "w1": w_ff1.astype(BF16),
        "w2": w_ff2.astype(BF16),
    }


def kernel(x_prompt, x_sample, cache_k, cache_v, state_ssm_re, state_ssm_im, state_conv, c_prompt, c_sample, rel_bias, w_ada, b_ada, w_in, ssm_a_re, ssm_a_im, ssm_log_dt, ssm_b_re, ssm_b_im, ssm_c_re, ssm_c_im, ssm_d, ssm_w_glu, q_norm_g, k_norm_g, attn_sinks, conv_w, out_norm_g, w_out, w_ff1, w_ff2):
    depth = w_in.shape[0]
    nbp, nbs = x_prompt.shape[0], x_sample.shape[0]
    assert x_sample.shape[1] == CHUNK and cache_k.shape[2] == WINDOW and nbs % 2 == 0

    nc = nbp + nbs
    ncp = -(-nc // SUBLANES) * SUBLANES
    c_all = jnp.pad(jnp.concatenate([c_prompt, c_sample], axis=0), ((0, ncp - nc), (0, 0)))
    mod = _mod_call(c_all, w_ada, b_ada)

    bias_p = _bias_call(rel_bias, sample=False)
    bias_s = _bias_call(rel_bias, sample=True)

    w_ssm, w_conv, w_kv = ssm_d.shape[1], conv_w.shape[1], N_KV_HEADS * HEAD_DIM
    widths = (w_ssm, w_conv, w_kv, attn_sinks.shape[1] * HEAD_DIM)
    scan_segs = tuple(SUBLANES // max(SSM_TILE // x.shape[1], 1) for x in (x_prompt, x_sample))
    weights = jax.vmap(functools.partial(_layer_weights, scan_segs))(
        w_in, ssm_a_re, ssm_a_im, ssm_log_dt, ssm_b_re, ssm_b_im, ssm_c_re, ssm_c_im, ssm_d, ssm_w_glu,
        q_norm_g, k_norm_g, attn_sinks, conv_w, out_norm_g, w_out, w_ff1, w_ff2)

    xp, xs = x_prompt, x_sample
    outs_p, outs_s = [], []
    for l in range(depth):
        lw = jax.tree.map(lambda a: a[l], weights)
        lw["widths"] = widths
        mod_p = mod[l, :nbp][:, None, :]
        mod_s = mod[l, nbp:nc][:, None, :]
        xp, *rest_p = _layer(xp, mod_p, lw, bias_p, False, None)
        xs, *rest_s = _layer(xs, mod_s, lw, bias_s, True,
                             (cache_k[l], cache_v[l], state_ssm_re[l], state_ssm_im[l], state_conv[l]))
        outs_p.append(rest_p)
        outs_s.append(rest_s)
    stack = lambda outs, i: jnp.stack([o[i] for o in outs])
    return (xp, xs,
            *(stack(outs_p, i) for i in range(5)),
            *(stack(outs_s, i) for i in range(5)))
```
